```python
import jax, jax.numpy as jnp
from jax import lax
import numpy as np

D_MODEL = 1024
BATCH = 16
SEQ = 2048
DEPTH = 1

ROPE_THETA = 500000.0
EPS = 1e-6
Q_BLOCK = 128
A_HEADS = 8
A_KV_HEADS = 2
A_HEAD_DIM = 64
A_ROT_DIM = A_HEAD_DIM // 4
IDX_HEADS = 8
IDX_DIM = 64
IDX_ROT_DIM = IDX_DIM // 4
INDEX_TOPK_MAX = 256
B_HEADS = 8
B_Q_RANK = 256
B_KV_RANK = 128
B_NOPE_DIM = 64
B_ROPE_DIM = 32
B_V_DIM = 64
D_FF = 2816
CONV_WIDTH = 3

A_Q_W = A_HEADS * A_HEAD_DIM
A_KV_W = A_KV_HEADS * A_HEAD_DIM
IDX_Q_W = IDX_HEADS * IDX_DIM
B_OUT_W = B_HEADS * B_V_DIM
IN_SPLITS = [A_Q_W, A_KV_W, A_KV_W, IDX_Q_W, IDX_DIM, IDX_HEADS,
             B_Q_RANK, B_KV_RANK, B_ROPE_DIM, D_MODEL, D_MODEL]
IN_PROJ_W = int(sum(IN_SPLITS))
IN_OFFSETS = [int(o) for o in np.cumsum(IN_SPLITS)[:-1]]

kernel_name = "hybrid_dsa_mla_gated_convffn"


def rmsnorm(x, g):
    xf = x.astype(jnp.float32)
    y = xf * lax.rsqrt(jnp.mean(xf * xf, axis=-1, keepdims=True) + EPS)
    return (y * g.astype(jnp.float32)).astype(x.dtype)


def rope_tables(positions, rot_dim):
    inv = ROPE_THETA ** (-jnp.arange(0, rot_dim, 2, dtype=jnp.float32) / rot_dim)
    ang = positions.astype(jnp.float32)[..., None] * inv
    return jnp.cos(ang), jnp.sin(ang)


def apply_rope(x, cos, sin, rot_dim):
    xr = x[..., :rot_dim].astype(jnp.float32)
    x1, x2 = jnp.split(xr, 2, axis=-1)
    c = cos[:, :, None, :]
    s = sin[:, :, None, :]
    rot = jnp.concatenate([x1 * c - x2 * s, x1 * s + x2 * c], axis=-1).astype(x.dtype)
    return jnp.concatenate([rot, x[..., rot_dim:]], axis=-1)


def to_blocks(t, nb):
    b = t.shape[0]
    return jnp.moveaxis(t.reshape(b, nb, Q_BLOCK, *t.shape[2:]), 1, 0)


def indexed_sparse_attention(q, k, v, qi, ki, wi, pos, topk):
    b, s = pos.shape
    nb = s // Q_BLOCK
    group = A_HEADS // A_KV_HEADS
    scale = A_HEAD_DIM ** -0.5
    bidx = jnp.arange(b)[:, None, None]

    def one_block(args):
        qb, qib, wib, posb = args
        dots = jnp.einsum('bqhd,bsd->bqhs', qib, ki, preferred_element_type=jnp.float32)
        iscore = jnp.einsum('bqh,bqhs->bqs', wib.astype(jnp.float32), jax.nn.relu(dots))
        causal = pos[:, None, :] <= posb[:, :, None]
        iscore = jnp.where(causal, iscore, -jnp.inf)
        _, idx = lax.top_k(iscore, topk)
        ksel = k[bidx, idx]
        vsel = v[bidx, idx]
        valid = pos[bidx, idx] <= posb[:, :, None]
        qg = qb.reshape(b, Q_BLOCK, A_KV_HEADS, group, A_HEAD_DIM)
        sc = jnp.einsum('bqgrd,bqkgd->bqgrk', qg, ksel, preferred_element_type=jnp.float32) * scale
        sc = jnp.where(valid[:, :, None, None, :], sc, -jnp.inf)
        p = jax.nn.softmax(sc, axis=-1)
        o = jnp.einsum('bqgrk,bqkgd->bqgrd', p.astype(v.dtype), vsel)
        return o.reshape(b, Q_BLOCK, A_HEADS * A_HEAD_DIM)

    out = lax.map(one_block, (to_blocks(q, nb), to_blocks(qi, nb), to_blocks(wi, nb), to_blocks(pos, nb)))
    return jnp.moveaxis(out, 0, 1).reshape(b, s, A_HEADS * A_HEAD_DIM)


def latent_attention(q_nope, q_rope, k_nope, k_rope, v, pos):
    b, s = pos.shape
    nb = s // Q_BLOCK
    scale = (B_NOPE_DIM + B_ROPE_DIM) ** -0.5

    def one_block(args):
        qn, qr, posb = args
        sc = (jnp.einsum('bqhd,bshd->bhqs', qn, k_nope, preferred_element_type=jnp.float32)
              + jnp.einsum('bqhd,bsd->bhqs', qr, k_rope, preferred_element_type=jnp.float32)) * scale
        causal = pos[:, None, None, :] <= posb[:, None, :, None]
        sc = jnp.where(causal, sc, -jnp.inf)
        p = jax.nn.softmax(sc, axis=-1)
        o = jnp.einsum('bhqs,bshd->bqhd', p.astype(v.dtype), v)
        return o.reshape(b, Q_BLOCK, B_HEADS * B_V_DIM)

    out = lax.map(one_block, (to_blocks(q_nope, nb), to_blocks(q_rope, nb), to_blocks(pos, nb)))
    return jnp.moveaxis(out, 0, 1).reshape(b, s, B_HEADS * B_V_DIM)


def causal_depthwise_conv(u, w, bias):
    s = u.shape[1]
    up = jnp.pad(u, ((0, 0), (CONV_WIDTH - 1, 0), (0, 0)))
    y = bias
    for j in range(CONV_WIDTH):
        y = y + w[j] * up[:, j:j + s]
    return y


def setup_inputs(seed: int = 0) -> dict:
    key = jax.random.key(seed)
    ks = jax.random.split(key, 20)
    f32 = jnp.float32
    L = DEPTH

    def nrm(k, shape, fan_in):
        return jax.random.normal(k, shape, f32) * (fan_in ** -0.5)

    def gain(k, shape):
        return 1.0 + 0.02 * jax.random.normal(k, shape, f32)

    return {
        "x": jax.random.normal(ks[0], (BATCH, SEQ, D_MODEL), f32),
        "positions": jnp.broadcast_to(jnp.arange(SEQ, dtype=jnp.int32), (BATCH, SEQ)),
        "norm_mix_g": gain(ks[1], (L, D_MODEL)),
        "w_in": nrm(ks[2], (L, D_MODEL, IN_PROJ_W), D_MODEL),
        "idx_k_norm_g": gain(ks[3], (L, IDX_DIM)),
        "q_a_norm_g": gain(ks[4], (L, B_Q_RANK)),
        "kv_a_norm_g": gain(ks[5], (L, B_KV_RANK)),
        "w_uq": nrm(ks[6], (L, B_Q_RANK, B_HEADS, B_NOPE_DIM + B_ROPE_DIM), B_Q_RANK),
        "w_uk": nrm(ks[7], (L, B_KV_RANK, B_HEADS, B_NOPE_DIM), B_KV_RANK),
        "w_uv": nrm(ks[8], (L, B_KV_RANK, B_HEADS, B_V_DIM), B_KV_RANK),
        "w_branch_a": nrm(ks[9], (L, A_Q_W, D_MODEL), A_Q_W),
        "w_branch_b": nrm(ks[10], (L, B_OUT_W, D_MODEL), B_OUT_W),
        "w_out": nrm(ks[11], (L, D_MODEL, D_MODEL), D_MODEL),
        "norm_ffn_g": gain(ks[12], (L, D_MODEL)),
        "w_up": nrm(ks[13], (L, D_MODEL, 2 * D_FF), D_MODEL),
        "conv_w": nrm(ks[14], (L, CONV_WIDTH, 2 * D_FF), CONV_WIDTH),
        "conv_b": 0.01 * jax.random.normal(ks[15], (L, 2 * D_FF), f32),
        "w_down": nrm(ks[16], (L, D_FF, D_MODEL), D_FF),
        "norm_final_g": gain(ks[17], (D_MODEL,)),
    }


def reference(x, positions, norm_mix_g, w_in, idx_k_norm_g, q_a_norm_g, kv_a_norm_g,
              w_uq, w_uk, w_uv, w_branch_a, w_branch_b, w_out, norm_ffn_g,
              w_up, conv_w, conv_b, w_down, norm_final_g):
    b, s, _ = x.shape
    topk = min(INDEX_TOPK_MAX, s // 4)
    cos_a, sin_a = rope_tables(positions, A_ROT_DIM)
    cos_i, sin_i = rope_tables(positions, IDX_ROT_DIM)
    cos_b, sin_b = rope_tables(positions, B_ROPE_DIM)
    h = x
    for l in range(DEPTH):
        hn = rmsnorm(h, norm_mix_g[l])
        proj = jnp.einsum('bsd,de->bse', hn, w_in[l])
        (a_q, a_k, a_v, i_q, i_k, i_w, b_cq, b_ckv, b_kr, g_a, g_b) = jnp.split(proj, IN_OFFSETS, axis=-1)

        a_q = apply_rope(a_q.reshape(b, s, A_HEADS, A_HEAD_DIM), cos_a, sin_a, A_ROT_DIM)
        a_k = apply_rope(a_k.reshape(b, s, A_KV_HEADS, A_HEAD_DIM), cos_a, sin_a, A_ROT_DIM)
        a_v = a_v.reshape(b, s, A_KV_HEADS, A_HEAD_DIM)
        i_q = apply_rope(i_q.reshape(b, s, IDX_HEADS, IDX_DIM), cos_i, sin_i, IDX_ROT_DIM) * (IDX_DIM ** -0.5)
        i_k = apply_rope(rmsnorm(i_k, idx_k_norm_g[l])[:, :, None, :], cos_i, sin_i, IDX_ROT_DIM)[:, :, 0, :]
        i_w = i_w * (IDX_HEADS ** -0.5)
        o_a = indexed_sparse_attention(a_q, a_k, a_v, i_q, i_k, i_w, positions, topk)

        c_q = rmsnorm(b_cq, q_a_norm_g[l])
        q_b = jnp.einsum('bsr,rhd->bshd', c_q, w_uq[l])
        q_nope = q_b[..., :B_NOPE_DIM]
        q_rope = apply_rope(q_b[..., B_NOPE_DIM:], cos_b, sin_b, B_ROPE_DIM)
        c_kv = rmsnorm(b_ckv, kv_a_norm_g[l])
        k_nope = jnp.einsum('bsr,rhd->bshd', c_kv, w_uk[l])
        v_b = jnp.einsum('bsr,rhd->bshd', c_kv, w_uv[l])
        k_rope = apply_rope(b_kr[:, :, None, :], cos_b, sin_b, B_ROPE_DIM)[:, :, 0, :]
        o_b = latent_attention(q_nope, q_rope, k_nope, k_rope, v_b, positions)

        y_a = jnp.einsum('bsc,cd->bsd', o_a, w_branch_a[l])
        y_b = jnp.einsum('bsc,cd->bsd', o_b, w_branch_b[l])
        merged = jax.nn.sigmoid(g_a) * y_a + jax.nn.sigmoid(g_b) * y_b
        h = h + jnp.einsum('bsd,de->bse', merged, w_out[l])

        hn = rmsnorm(h, norm_ffn_g[l])
        u = jnp.einsum('bsd,df->bsf', hn, w_up[l])
        u = causal_depthwise_conv(u, conv_w[l], conv_b[l])
        gate, val = jnp.split(u, 2, axis=-1)
        h = h + jnp.einsum('bsf,fd->bsd', jax.nn.silu(gate) * val, w_down[l])
    return rmsnorm(h, norm_final_g)
```

```python
import functools

import jax
import jax.numpy as jnp
import numpy as np
from jax import lax
from jax.experimental import pallas as pl
from jax.experimental.pallas import tpu as pltpu

ROPE_THETA = 500000.0
EPS = 1e-6
A_HEADS = 8
A_KV_HEADS = 2
A_GROUP = A_HEADS // A_KV_HEADS
A_HEAD_DIM = 64
A_ROT_DIM = 16
IDX_HEADS = 8
IDX_DIM = 64
INDEX_TOPK_MAX = 256
B_HEADS = 8
B_Q_RANK = 256
B_KV_RANK = 128
B_NOPE_DIM = 64
B_ROPE_DIM = 32
B_V_DIM = 64
CONV_WIDTH = 3

LANES = 128
SUBLANES = 8
VMEM_LIMIT_BYTES = 56 * 1024 * 1024

_O_AQ, _O_AK, _O_AV, _O_IQ, _O_CQ, _O_CKV, _O_GA, _O_GB, _O_TAIL = (
    0, 512, 640, 768, 1280, 1536, 1664, 2688, 3712)
_W_CAT = 3840
_T_KR, _T_IW = 64, 96

F32 = jnp.float32
BF16 = jnp.bfloat16
NEG_INF = float("-inf")


def _dot(a, b):
    return jnp.dot(a, b, preferred_element_type=F32)


def _dot_nt(a, b):
    return lax.dot_general(a, b, (((1,), (1,)), ((), ())), preferred_element_type=F32)


def _rope_lanes(x, cos_t, sin_t, take_up, half):
    up = pltpu.roll(x, LANES - half, axis=1)
    dn = pltpu.roll(x, half, axis=1)
    return x * cos_t + jnp.where(take_up, up, dn) * sin_t


def _proj_kernel(x_ref, ca_ref, sa_ref, cb_ref, sb_ref, gmix_ref, w_ref, gik_ref,
                 gq_ref, gkv_ref, wuq_ref, wuk_ref, wuv_ref,
                 aq_ref, ak_ref, av_ref, iq_ref, ik_ref, iw_ref, qb_ref, kb_ref,
                 vb_ref, ga_ref, gb_ref):
    x = x_ref[0]
    tm = x.shape[0]
    ms = jnp.mean(x * x, axis=-1, keepdims=True)
    hn = (x * lax.rsqrt(ms + EPS) * gmix_ref[...]).astype(BF16)

    lane = lax.broadcasted_iota(jnp.int32, (tm, LANES), 1)
    ca, sa, cb, sb = ca_ref[0], sa_ref[0], cb_ref[0], sb_ref[0]
    up_a = (lane % A_HEAD_DIM) < (A_ROT_DIM // 2)
    up_b = lane < (B_NOPE_DIM + B_ROPE_DIM // 2)
    v_one = jnp.where(lane == A_HEAD_DIM, 1.0, 0.0).astype(F32)
    low_half = lane < A_HEAD_DIM

    def proj(off, width):
        return _dot(hn, w_ref[:, off:off + width])

    def store_heads(ref, tile128, first_head, scale):
        t = tile128 * scale if scale != 1.0 else tile128
        ref[0, first_head] = t[:, :A_HEAD_DIM].astype(ref.dtype)
        ref[0, first_head + 1] = t[:, A_HEAD_DIM:].astype(ref.dtype)

    for off, ref, scale in ((_O_AQ, aq_ref, A_HEAD_DIM ** -0.5),
                            (_O_IQ, iq_ref, IDX_DIM ** -0.5)):
        full = proj(off, 4 * LANES)
        for j in range(4):
            t = _rope_lanes(full[:, j * LANES:(j + 1) * LANES], ca, sa, up_a,
                            A_ROT_DIM // 2)
            store_heads(ref, t, 2 * j, scale)

    akv = proj(_O_AK, 2 * LANES)
    store_heads(ak_ref, _rope_lanes(akv[:, :LANES], ca, sa, up_a, A_ROT_DIM // 2),
                0, 1.0)
    v2 = akv[:, LANES:]
    av_ref[0, 0] = (jnp.where(low_half, v2, 0.0) + v_one).astype(BF16)
    v2r = pltpu.roll(v2, A_HEAD_DIM, axis=1)
    av_ref[0, 1] = (jnp.where(low_half, v2r, 0.0) + v_one).astype(BF16)

    tail = proj(_O_TAIL, LANES)
    ik_sq = jnp.where(low_half, tail * tail, 0.0)
    ik_ms = jnp.sum(ik_sq, axis=-1, keepdims=True) * (1.0 / IDX_DIM)
    ik_n = tail * lax.rsqrt(ik_ms + EPS) * gik_ref[...]
    ik_r = _rope_lanes(ik_n, ca, sa, up_a, A_ROT_DIM // 2)
    ik_ref[0] = ik_r[:, :IDX_DIM].astype(BF16)
    iw_ref[0] = tail * (IDX_HEADS ** -0.5)
    kr = _rope_lanes(tail, cb, sb, up_b, B_ROPE_DIM // 2)
    kr = jnp.where((lane >= B_NOPE_DIM) & (lane < B_NOPE_DIM + B_ROPE_DIM), kr, 0.0)

    cq = proj(_O_CQ, B_Q_RANK)
    cq_n = (cq * lax.rsqrt(jnp.mean(cq * cq, axis=-1, keepdims=True) + EPS)
            * gq_ref[...]).astype(BF16)
    ckv = proj(_O_CKV, B_KV_RANK)
    ckv_n = (ckv * lax.rsqrt(jnp.mean(ckv * ckv, axis=-1, keepdims=True) + EPS)
             * gkv_ref[...]).astype(BF16)
    b_scale = (B_NOPE_DIM + B_ROPE_DIM) ** -0.5
    for h in range(B_HEADS):
        q = _dot(cq_n, wuq_ref[:, h * LANES:(h + 1) * LANES])
        q = _rope_lanes(q, cb, sb, up_b, B_ROPE_DIM // 2) * b_scale
        qb_ref[0, h] = q.astype(BF16)
        k = _dot(ckv_n, wuk_ref[:, h * LANES:(h + 1) * LANES]) + kr
        kb_ref[0, h] = k.astype(BF16)
        v = _dot(ckv_n, wuv_ref[:, h * LANES:(h + 1) * LANES]) + v_one
        vb_ref[0, h] = v.astype(BF16)

    ga_ref[0] = jax.nn.sigmoid(proj(_O_GA, 1024)).astype(BF16)
    gb_ref[0] = jax.nn.sigmoid(proj(_O_GB, 1024)).astype(BF16)


def _rope_tables(positions):
    pos = positions.astype(F32)[..., None]

    def cs(rot):
        inv = ROPE_THETA ** (-jnp.arange(0, rot, 2, dtype=F32) / rot)
        ang = pos * inv
        return jnp.cos(ang), jnp.sin(ang)

    b, s = positions.shape
    ca8, sa8 = cs(A_ROT_DIM)
    ones = jnp.ones((b, s, A_HEAD_DIM - A_ROT_DIM), F32)
    zeros = jnp.zeros((b, s, A_HEAD_DIM - A_ROT_DIM), F32)
    ca = jnp.concatenate([ca8, ca8, ones] * 2, axis=-1)
    sa = jnp.concatenate([-sa8, sa8, zeros] * 2, axis=-1)
    cb16, sb16 = cs(B_ROPE_DIM)
    one64 = jnp.ones((b, s, B_NOPE_DIM), F32)
    zero64 = jnp.zeros((b, s, B_NOPE_DIM), F32)
    pad = LANES - B_NOPE_DIM - B_ROPE_DIM
    cb = jnp.concatenate([one64, cb16, cb16, jnp.ones((b, s, pad), F32)], axis=-1)
    sb = jnp.concatenate([zero64, -sb16, sb16, jnp.zeros((b, s, pad), F32)], axis=-1)
    return ca, sa, cb, sb


def _regroup_w_in(w_in):
    splits = [512, 128, 128, 512, 64, 8, 256, 128, 32, 1024, 1024]
    offs = np.concatenate([[0], np.cumsum(splits)])
    seg = [w_in[:, offs[i]:offs[i + 1]] for i in range(len(splits))]
    a_q, a_k, a_v, i_q, i_k, i_w, b_cq, b_ckv, b_kr, g_a, g_b = seg
    pad = jnp.zeros((w_in.shape[0], LANES - 64 - 32 - 8), w_in.dtype)
    return jnp.concatenate(
        [a_q, a_k, a_v, i_q, b_cq, b_ckv, g_a, g_b, i_k, b_kr, i_w, pad], axis=1)


def _pad_heads(w, width):
    r, h, d = w.shape
    return jnp.pad(w, ((0, 0), (0, 0), (0, width - d))).reshape(r, h * width)


def _const_spec(shape):
    nd = len(shape)
    return pl.BlockSpec(shape, lambda *_: (0,) * nd)


def _proj_call(x, tables, gmix, wcat, gik, gq, gkv, wuq, wuk, wuv, tm):
    b, s, d = x.shape
    grid = (b, s // tm)
    tok = lambda w: pl.BlockSpec((1, tm, w), lambda i, j: (i, j, 0))
    head = lambda h, w: pl.BlockSpec((1, h, tm, w), lambda i, j: (i, 0, j, 0))
    in_specs = [tok(d)] + [tok(LANES)] * 4 + [
        _const_spec(gmix.shape), _const_spec(wcat.shape), _const_spec(gik.shape),
        _const_spec(gq.shape), _const_spec(gkv.shape), _const_spec(wuq.shape),
        _const_spec(wuk.shape), _const_spec(wuv.shape)]
    sds = jax.ShapeDtypeStruct
    out_shape = [
        sds((b, A_HEADS, s, A_HEAD_DIM), BF16),
        sds((b, A_KV_HEADS, s, A_HEAD_DIM), BF16),
        sds((b, A_KV_HEADS, s, LANES), BF16),
        sds((b, IDX_HEADS, s, IDX_DIM), BF16),
        sds((b, s, IDX_DIM), BF16),
        sds((b, s, LANES), F32),
        sds((b, B_HEADS, s, LANES), BF16),
        sds((b, B_HEADS, s, LANES), BF16),
        sds((b, B_HEADS, s, LANES), BF16),
        sds((b, s, d), BF16),
        sds((b, s, d), BF16),
    ]
    out_specs = [
        head(A_HEADS, A_HEAD_DIM), head(A_KV_HEADS, A_HEAD_DIM),
        head(A_KV_HEADS, LANES), head(IDX_HEADS, IDX_DIM), tok(IDX_DIM),
        tok(LANES), head(B_HEADS, LANES), head(B_HEADS, LANES),
        head(B_HEADS, LANES), tok(d), tok(d)]
    return pl.pallas_call(
        _proj_kernel,
        grid=grid,
        in_specs=in_specs,
        out_specs=out_specs,
        out_shape=out_shape,
        compiler_params=pltpu.CompilerParams(
            dimension_semantics=("arbitrary", "arbitrary"),
            vmem_limit_bytes=VMEM_LIMIT_BYTES),
        name="proj",
    )(x, *tables, gmix, wcat, gik, gq, gkv, wuq, wuk, wuv)


def _prepare_proj_weights(norm_mix_g, w_in, idx_k_norm_g, q_a_norm_g, kv_a_norm_g,
                          w_uq, w_uk, w_uv):
    gmix = norm_mix_g.reshape(1, -1)
    wcat = _regroup_w_in(w_in).astype(BF16)
    gik = jnp.pad(idx_k_norm_g, (0, LANES - IDX_DIM)).reshape(1, LANES)
    gq = q_a_norm_g.reshape(1, -1)
    gkv = kv_a_norm_g.reshape(1, -1)
    wuq = _pad_heads(w_uq, LANES).astype(BF16)
    wuk = _pad_heads(w_uk, LANES).astype(BF16)
    wuv = _pad_heads(w_uv, LANES).astype(BF16)
    return gmix, wcat, gik, gq, gkv, wuq, wuk, wuv


_INT_MIN = -2 ** 31
_ORD_LOWEST_FINITE = 0x00800000


def _ordinal_to_float(u):
    o = u ^ jnp.int32(_INT_MIN)
    bits = o ^ ((o >> 31) & jnp.int32(0x7FFFFFFF))
    return lax.bitcast_convert_type(bits, F32)


def _softmax_step(s, v, m, acc):
    m_new = jnp.maximum(m, jnp.max(s, axis=-1, keepdims=True))
    m_safe = jnp.where(m_new == NEG_INF, 0.0, m_new)
    p = jnp.exp(s - m_safe)
    alpha = jnp.exp(m - m_safe)
    acc = alpha * acc + _dot(p.astype(BF16), v)
    return m_new, acc


def _finish_heads(acc_even, acc_odd, lane):
    o0 = acc_even * (1.0 / acc_even[:, B_V_DIM:B_V_DIM + 1])
    o1 = acc_odd * (1.0 / acc_odd[:, B_V_DIM:B_V_DIM + 1])
    return jnp.where(lane < B_V_DIM, o0, pltpu.roll(o1, B_V_DIM, axis=1))


def _dsa_kernel(ik_ref, iq_ref, iw_ref, aq_ref, ak_ref, av_ref, posk_ref, posq_ref,
                oa_ref, isc_ref, bias_ref, wt_ref, *, topk):
    qi = pl.program_id(1)
    tq = aq_ref.shape[2]
    nk = qi + 1
    half = tq // 2
    wt_ref[...] = iw_ref[0].T[_T_IW:_T_IW + IDX_HEADS, :]
    posq = posq_ref[0]

    def idx_chunk(c, carry):
        for sub in range(2):
            k0 = pl.multiple_of(c * tq + sub * half, half)
            ikc = ik_ref[0, pl.ds(k0, half), :]
            acc = jnp.zeros((half, tq), F32)
            for h in range(IDX_HEADS):
                d = _dot_nt(ikc, iq_ref[0, h])
                acc = acc + wt_ref[h:h + 1, :] * jnp.maximum(d, 0.0)
            causal = posk_ref[0, pl.ds(k0, half), :] <= posq
            isc_ref[c, sub * half:(sub + 1) * half, :] = jnp.where(causal, acc, NEG_INF)
        return carry

    lax.fori_loop(0, nk, idx_chunk, 0)

    def count_ge(thr):
        def body(c, acc):
            hit = jnp.where(isc_ref[c] >= thr, 1.0, 0.0)
            return acc + jnp.sum(hit.reshape(tq // SUBLANES, SUBLANES, tq), axis=0)
        acc = lax.fori_loop(0, nk, body, jnp.zeros((SUBLANES, tq), F32))
        return jnp.sum(acc, axis=0, keepdims=True)

    def search_iter(it, carry):
        t_u, n_ge = carry
        cand = t_u | lax.shift_left(jnp.int32(1), 31 - it)
        cnt = count_ge(_ordinal_to_float(cand))
        ok = cnt >= topk
        return jnp.where(ok, cand, t_u), jnp.where(ok, cnt, n_ge)

    t_u, n_ge = lax.fori_loop(
        0, 32, search_iter,
        (jnp.zeros((1, tq), jnp.int32), jnp.zeros((1, tq), F32)))
    short = (t_u >= 0) & (t_u < _ORD_LOWEST_FINITE)
    thr = _ordinal_to_float(jnp.where(short, jnp.int32(_ORD_LOWEST_FINITE), t_u))
    tied = jnp.logical_and(n_ge > topk, jnp.logical_not(short))

    def write_bias(c, carry):
        sel = isc_ref[c] >= thr
        bias_ref[c] = jnp.where(sel, 0.0, NEG_INF).T
        return carry

    lax.fori_loop(0, nk, write_bias, 0)

    @pl.when(jnp.max(jnp.where(tied, 1.0, 0.0)) > 0.0)
    def _():
        def count_gt(c, acc):
            hit = jnp.where(isc_ref[c] > thr, 1.0, 0.0)
            return acc + jnp.sum(hit, axis=0, keepdims=True)
        n_gt = lax.fori_loop(0, nk, count_gt, jnp.zeros((1, tq), F32))
        need = topk - n_gt
        row = lax.broadcasted_iota(jnp.int32, (tq, tq), 0)
        col = lax.broadcasted_iota(jnp.int32, (tq, tq), 1)
        below = jnp.where(col < row, 1.0, 0.0).astype(BF16)

        def fix(c, seen):
            x = isc_ref[c]
            eq = jnp.where(x == thr, 1.0, 0.0)
            rank = _dot(below, eq.astype(BF16)) + seen
            sel = (x > thr) | ((x == thr) & (rank < need))
            bias_ref[c] = jnp.where(sel, 0.0, NEG_INF).T
            return seen + jnp.sum(eq, axis=0, keepdims=True)

        lax.fori_loop(0, nk, fix, jnp.zeros((1, tq), F32))

    lane = lax.broadcasted_iota(jnp.int32, (tq, LANES), 1)
    for pair in range(A_HEADS // 2):
        accs = []
        for h in (2 * pair, 2 * pair + 1):
            g = h // A_GROUP
            q = aq_ref[0, h]

            def att(c, carry, q=q, g=g):
                k0 = pl.multiple_of(c * tq, tq)
                s = _dot_nt(q, ak_ref[0, g, pl.ds(k0, tq), :]) + bias_ref[c]
                return _softmax_step(s, av_ref[0, g, pl.ds(k0, tq), :], *carry)

            _, acc = lax.fori_loop(
                0, nk, att,
                (jnp.full((tq, 1), NEG_INF, F32), jnp.zeros((tq, LANES), F32)))
            accs.append(acc)
        oa_ref[0, :, pair * LANES:(pair + 1) * LANES] = _finish_heads(
            accs[0], accs[1], lane).astype(oa_ref.dtype)


def _dsa_call(ik, iq, iw, aq, ak, av, posk, posq, tq, topk):
    b, heads, s, _ = aq.shape
    nq = s // tq
    grid = (b, nq)
    qhead = lambda h, w: pl.BlockSpec((1, h, tq, w), lambda i, j: (i, 0, j, 0))
    khead = lambda h, w: pl.BlockSpec((1, h, s, w), lambda i, j: (i, 0, 0, 0))
    in_specs = [
        pl.BlockSpec((1, s, IDX_DIM), lambda i, j: (i, 0, 0)),
        qhead(IDX_HEADS, IDX_DIM),
        pl.BlockSpec((1, tq, LANES), lambda i, j: (i, j, 0)),
        qhead(A_HEADS, A_HEAD_DIM),
        khead(A_KV_HEADS, A_HEAD_DIM),
        khead(A_KV_HEADS, LANES),
        pl.BlockSpec((1, s, 1), lambda i, j: (i, 0, 0)),
        pl.BlockSpec((1, 1, tq), lambda i, j: (i, 0, j)),
    ]
    return pl.pallas_call(
        functools.partial(_dsa_kernel, topk=topk),
        grid=grid,
        in_specs=in_specs,
        out_specs=pl.BlockSpec((1, tq, heads * A_HEAD_DIM), lambda i, j: (i, j, 0)),
        out_shape=jax.ShapeDtypeStruct((b, s, heads * A_HEAD_DIM), BF16),
        scratch_shapes=[
            pltpu.VMEM((nq, tq, tq), F32),
            pltpu.VMEM((nq, tq, tq), F32),
            pltpu.VMEM((IDX_HEADS, tq), F32),
        ],
        compiler_params=pltpu.CompilerParams(
            dimension_semantics=("arbitrary", "arbitrary"),
            vmem_limit_bytes=VMEM_LIMIT_BYTES),
        name="dsa",
    )(ik, iq, iw, aq, ak, av, posk, posq)


def _mla_kernel(qb_ref, kb_ref, vb_ref, posq_ref, posk_ref, ob_ref):
    qi = pl.program_id(1)
    tq = qb_ref.shape[2]
    nk = qi + 1
    posq = posq_ref[0]
    lane = lax.broadcasted_iota(jnp.int32, (tq, LANES), 1)
    for pair in range(B_HEADS // 2):
        accs = []
        for h in (2 * pair, 2 * pair + 1):
            q = qb_ref[0, h]

            def att(c, carry, q=q, h=h):
                k0 = pl.multiple_of(c * tq, tq)
                s = _dot_nt(q, kb_ref[0, h, pl.ds(k0, tq), :])
                causal = posk_ref[0, :, pl.ds(k0, tq)] <= posq
                s = jnp.where(causal, s, NEG_INF)
                return _softmax_step(s, vb_ref[0, h, pl.ds(k0, tq), :], *carry)

            _, acc = lax.fori_loop(
                0, nk, att,
                (jnp.full((tq, 1), NEG_INF, F32), jnp.zeros((tq, LANES), F32)))
            accs.append(acc)
        ob_ref[0, :, pair * LANES:(pair + 1) * LANES] = _finish_heads(
            accs[0], accs[1], lane).astype(ob_ref.dtype)


def _mla_call(qb, kb, vb, posq, posk, tq):
    b, heads, s, _ = qb.shape
    grid = (b, s // tq)
    return pl.pallas_call(
        _mla_kernel,
        grid=grid,
        in_specs=[
            pl.BlockSpec((1, heads, tq, LANES), lambda i, j: (i, 0, j, 0)),
            pl.BlockSpec((1, heads, s, LANES), lambda i, j: (i, 0, 0, 0)),
            pl.BlockSpec((1, heads, s, LANES), lambda i, j: (i, 0, 0, 0)),
            pl.BlockSpec((1, tq, 1), lambda i, j: (i, j, 0)),
            pl.BlockSpec((1, 1, s), lambda i, j: (i, 0, 0)),
        ],
        out_specs=pl.BlockSpec((1, tq, heads * B_V_DIM), lambda i, j: (i, j, 0)),
        out_shape=jax.ShapeDtypeStruct((b, s, heads * B_V_DIM), BF16),
        compiler_params=pltpu.CompilerParams(
            dimension_semantics=("arbitrary", "arbitrary"),
            vmem_limit_bytes=VMEM_LIMIT_BYTES),
        name="mla",
    )(qb, kb, vb, posq, posk)


def _shift_rows(u, prev_tail, shift, row8):
    rolled = pltpu.roll(u, shift, axis=0)
    head = jnp.where(row8 < shift, pltpu.roll(prev_tail, shift, axis=0),
                     rolled[:SUBLANES])
    return jnp.concatenate([head, rolled[SUBLANES:]], axis=0)


def _ffn_kernel(x_ref, oa_ref, ob_ref, ga_ref, gb_ref, wba_ref, wbb_ref, wout_ref,
                gffn_ref, wup_ref, cw_ref, cb_ref, wdown_ref, gfin_ref,
                out_ref, tail_ref, *, n_chunks, final_norm):
    j = pl.program_id(1)
    x = x_ref[0]
    tm = x.shape[0]
    ya = _dot(oa_ref[0], wba_ref[...])
    yb = _dot(ob_ref[0], wbb_ref[...])
    merged = ga_ref[0].astype(F32) * ya + gb_ref[0].astype(F32) * yb
    h1 = x + _dot(merged.astype(BF16), wout_ref[...])
    ms = jnp.mean(h1 * h1, axis=-1, keepdims=True)
    hn = (h1 * lax.rsqrt(ms + EPS) * gffn_ref[...]).astype(BF16)

    fc = wdown_ref.shape[1]
    row8 = lax.broadcasted_iota(jnp.int32, (SUBLANES, fc), 0)

    @pl.when(j == 0)
    def _():
        tail_ref[...] = jnp.zeros_like(tail_ref)

    acc = h1
    for c in range(n_chunks):
        halves = []
        for part in range(2):
            idx = part * n_chunks + c
            u = _dot(hn, wup_ref[idx])
            prev_tail = tail_ref[idx]
            tail_ref[idx] = u[tm - SUBLANES:]
            y = (cb_ref[idx] + cw_ref[0, idx] * _shift_rows(u, prev_tail, 2, row8)
                 + cw_ref[1, idx] * _shift_rows(u, prev_tail, 1, row8)
                 + cw_ref[2, idx] * u)
            halves.append(y)
        gate, val = halves
        act = (gate * jax.nn.sigmoid(gate) * val).astype(BF16)
        acc = acc + _dot(act, wdown_ref[c])
    if final_norm:
        ms = jnp.mean(acc * acc, axis=-1, keepdims=True)
        acc = acc * lax.rsqrt(ms + EPS) * gfin_ref[...]
    out_ref[0] = acc


def _ffn_call(x, oa, ob, ga, gb, wba, wbb, wout, gffn, wup, cw, cb, wdown, gfin, tm,
              final_norm):
    b, s, d = x.shape
    n_parts, _, fc = wup.shape
    n_chunks = n_parts // 2
    tok = lambda w: pl.BlockSpec((1, tm, w), lambda i, j: (i, j, 0))
    weights = (wba, wbb, wout, gffn, wup, cw, cb, wdown, gfin)
    return pl.pallas_call(
        functools.partial(_ffn_kernel, n_chunks=n_chunks, final_norm=final_norm),
        grid=(b, s // tm),
        in_specs=[tok(d), tok(oa.shape[-1]), tok(ob.shape[-1]), tok(d), tok(d)]
        + [_const_spec(w.shape) for w in weights],
        out_specs=tok(d),
        out_shape=jax.ShapeDtypeStruct((b, s, d), x.dtype),
        scratch_shapes=[pltpu.VMEM((n_parts, SUBLANES, fc), F32)],
        compiler_params=pltpu.CompilerParams(
            dimension_semantics=("arbitrary", "arbitrary"),
            vmem_limit_bytes=VMEM_LIMIT_BYTES),
        name="ffn",
    )(x, oa, ob, ga, gb, *weights)


def _prepare_ffn_weights(w_branch_a, w_branch_b, w_out, norm_ffn_g, w_up, conv_w,
                         conv_b, w_down, norm_final_g, n_chunks):
    d, two_f = w_up.shape
    fc = two_f // (2 * n_chunks)
    wup = w_up.reshape(d, 2 * n_chunks, fc).transpose(1, 0, 2).astype(BF16)
    cw = conv_w.reshape(CONV_WIDTH, 2 * n_chunks, 1, fc)
    cb = conv_b.reshape(2 * n_chunks, 1, fc)
    wdown = w_down.reshape(n_chunks, fc, d).astype(BF16)
    return (w_branch_a.astype(BF16), w_branch_b.astype(BF16), w_out.astype(BF16),
            norm_ffn_g.reshape(1, -1), wup, cw, cb, wdown, norm_final_g.reshape(1, -1))


TM_PROJ = 512
TQ_ATTN = 256
TM_FFN = 512
FFN_CHUNKS = 2


def kernel(x, positions, norm_mix_g, w_in, idx_k_norm_g, q_a_norm_g, kv_a_norm_g, w_uq, w_uk, w_uv, w_branch_a, w_branch_b, w_out, norm_ffn_g, w_up, conv_w, conv_b, w_down, norm_final_g):
    b, s, _ = x.shape
    depth = w_in.shape[0]
    topk = min(INDEX_TOPK_MAX, s // 4)
    tables = _rope_tables(positions)
    pos_col = positions[:, :, None]
    pos_row = positions[:, None, :]
    h = x
    for l in range(depth):
        pw = _prepare_proj_weights(norm_mix_g[l], w_in[l], idx_k_norm_g[l], q_a_norm_g[l],
                                   kv_a_norm_g[l], w_uq[l], w_uk[l], w_uv[l])
        aq, ak, av, iq, ik, iw, qb, kb, vb, ga, gb = _proj_call(h, tables, *pw, tm=TM_PROJ)
        oa = _dsa_call(ik, iq, iw, aq, ak, av, pos_col, pos_row, TQ_ATTN, topk)
        ob = _mla_call(qb, kb, vb, pos_col, pos_row, TQ_ATTN)
        fw = _prepare_ffn_weights(w_branch_a[l], w_branch_b[l], w_out[l], norm_ffn_g[l],
                                  w_up[l], conv_w[l], conv_b[l], w_down[l], norm_final_g,
                                  FFN_CHUNKS)
        h = _ffn_call(h, oa, ob, ga, gb, *fw, tm=TM_FFN, final_norm=(l == depth - 1))
    return h
```

```python
import functools

import jax
import jax.numpy as jnp
import numpy as np
from jax import lax
from jax.experimental import pallas as pl
from jax.experimental.pallas import tpu as pltpu

ROPE_THETA = 500000.0
EPS = 1e-6
A_HEADS = 8
A_KV_HEADS = 2
A_GROUP = A_HEADS // A_KV_HEADS
A_HEAD_DIM = 64
A_ROT_DIM = 16
IDX_HEADS = 8
IDX_DIM = 64
INDEX_TOPK_MAX = 256
B_HEADS = 8
B_Q_RANK = 256
B_KV_RANK = 128
B_NOPE_DIM = 64
B_ROPE_DIM = 32
B_V_DIM = 64
CONV_WIDTH = 3

LANES = 128
SUBLANES = 8
VMEM_LIMIT_BYTES = 56 * 1024 * 1024

_O_AQ, _O_AK, _O_AV, _O_IQ, _O_CQ, _O_CKV, _O_GA, _O_GB, _O_TAIL = (
    0, 512, 640, 768, 1280, 1536, 1664, 2688, 3712)
_W_CAT = 3840
_T_KR, _T_IW = 64, 96

F32 = jnp.float32
BF16 = jnp.bfloat16
NEG_INF = float("-inf")
LOG2_E = 1.4426950408889634
F32_LOWEST = float(np.finfo(np.float32).min)


def _dot(a, b):
    return jnp.dot(a, b, preferred_element_type=F32)


def _dot_nt(a, b):
    return lax.dot_general(a, b, (((1,), (1,)), ((), ())), preferred_element_type=F32)


def _rope_lanes(x, cos_t, sin_t, take_up, half):
    up = pltpu.roll(x, LANES - half, axis=1)
    dn = pltpu.roll(x, half, axis=1)
    return x * cos_t + jnp.where(take_up, up, dn) * sin_t


def _proj_kernel(x_ref, ca_ref, sa_ref, cb_ref, sb_ref, gmix_ref, w_ref, gik_ref,
                 gq_ref, gkv_ref, wuq_ref, wuk_ref, wuv_ref,
                 aq_ref, ak_ref, av_ref, iq_ref, ik_ref, iw_ref, qb_ref, kb_ref,
                 vb_ref, ga_ref, gb_ref):
    x = x_ref[0]
    tm = x.shape[0]
    ms = jnp.mean(x * x, axis=-1, keepdims=True)
    hn = (x * lax.rsqrt(ms + EPS) * gmix_ref[...]).astype(BF16)

    lane = lax.broadcasted_iota(jnp.int32, (tm, LANES), 1)
    ca, sa, cb, sb = ca_ref[0], sa_ref[0], cb_ref[0], sb_ref[0]
    up_a = (lane % A_HEAD_DIM) < (A_ROT_DIM // 2)
    up_b = lane < (B_NOPE_DIM + B_ROPE_DIM // 2)
    v_one = jnp.where(lane == A_HEAD_DIM, 1.0, 0.0).astype(F32)
    low_half = lane < A_HEAD_DIM

    def proj(off, width):
        return _dot(hn, w_ref[:, off:off + width])

    def store_heads(ref, tile128, first_head, scale):
        t = tile128 * scale if scale != 1.0 else tile128
        ref[0, first_head] = t[:, :A_HEAD_DIM].astype(ref.dtype)
        ref[0, first_head + 1] = t[:, A_HEAD_DIM:].astype(ref.dtype)

    for off, ref, scale in ((_O_AQ, aq_ref, A_HEAD_DIM ** -0.5 * LOG2_E),
                            (_O_IQ, iq_ref, IDX_DIM ** -0.5)):
        full = proj(off, 4 * LANES)
        for j in range(4):
            t = _rope_lanes(full[:, j * LANES:(j + 1) * LANES], ca, sa, up_a,
                            A_ROT_DIM // 2)
            store_heads(ref, t, 2 * j, scale)

    akv = proj(_O_AK, 2 * LANES)
    store_heads(ak_ref, _rope_lanes(akv[:, :LANES], ca, sa, up_a, A_ROT_DIM // 2),
                0, 1.0)
    v2 = akv[:, LANES:]
    av_ref[0, 0] = (jnp.where(low_half, v2, 0.0) + v_one).astype(BF16)
    v2r = pltpu.roll(v2, A_HEAD_DIM, axis=1)
    av_ref[0, 1] = (jnp.where(low_half, v2r, 0.0) + v_one).astype(BF16)

    tail = proj(_O_TAIL, LANES)
    ik_sq = jnp.where(low_half, tail * tail, 0.0)
    ik_ms = jnp.sum(ik_sq, axis=-1, keepdims=True) * (1.0 / IDX_DIM)
    ik_n = tail * lax.rsqrt(ik_ms + EPS) * gik_ref[...]
    ik_r = _rope_lanes(ik_n, ca, sa, up_a, A_ROT_DIM // 2)
    ik_ref[0] = ik_r[:, :IDX_DIM].astype(BF16)
    iw_ref[0] = tail * (IDX_HEADS ** -0.5)
    kr = _rope_lanes(tail, cb, sb, up_b, B_ROPE_DIM // 2)
    kr = jnp.where((lane >= B_NOPE_DIM) & (lane < B_NOPE_DIM + B_ROPE_DIM), kr, 0.0)

    cq = proj(_O_CQ, B_Q_RANK)
    cq_n = (cq * lax.rsqrt(jnp.mean(cq * cq, axis=-1, keepdims=True) + EPS)
            * gq_ref[...]).astype(BF16)
    ckv = proj(_O_CKV, B_KV_RANK)
    ckv_n = (ckv * lax.rsqrt(jnp.mean(ckv * ckv, axis=-1, keepdims=True) + EPS)
             * gkv_ref[...]).astype(BF16)
    b_scale = (B_NOPE_DIM + B_ROPE_DIM) ** -0.5 * LOG2_E
    for h in range(B_HEADS):
        q = _dot(cq_n, wuq_ref[:, h * LANES:(h + 1) * LANES])
        q = _rope_lanes(q, cb, sb, up_b, B_ROPE_DIM // 2) * b_scale
        qb_ref[0, h] = q.astype(BF16)
        k = _dot(ckv_n, wuk_ref[:, h * LANES:(h + 1) * LANES]) + kr
        kb_ref[0, h] = k.astype(BF16)
        v = _dot(ckv_n, wuv_ref[:, h * LANES:(h + 1) * LANES]) + v_one
        vb_ref[0, h] = v.astype(BF16)

    ga_ref[0] = jax.nn.sigmoid(proj(_O_GA, 1024)).astype(BF16)
    gb_ref[0] = jax.nn.sigmoid(proj(_O_GB, 1024)).astype(BF16)


def _rope_tables(positions):
    pos = positions.astype(F32)[..., None]

    def cs(rot):
        inv = ROPE_THETA ** (-jnp.arange(0, rot, 2, dtype=F32) / rot)
        ang = pos * inv
        return jnp.cos(ang), jnp.sin(ang)

    b, s = positions.shape
    ca8, sa8 = cs(A_ROT_DIM)
    ones = jnp.ones((b, s, A_HEAD_DIM - A_ROT_DIM), F32)
    zeros = jnp.zeros((b, s, A_HEAD_DIM - A_ROT_DIM), F32)
    ca = jnp.concatenate([ca8, ca8, ones] * 2, axis=-1)
    sa = jnp.concatenate([-sa8, sa8, zeros] * 2, axis=-1)
    cb16, sb16 = cs(B_ROPE_DIM)
    one64 = jnp.ones((b, s, B_NOPE_DIM), F32)
    zero64 = jnp.zeros((b, s, B_NOPE_DIM), F32)
    pad = LANES - B_NOPE_DIM - B_ROPE_DIM
    cb = jnp.concatenate([one64, cb16, cb16, jnp.ones((b, s, pad), F32)], axis=-1)
    sb = jnp.concatenate([zero64, -sb16, sb16, jnp.zeros((b, s, pad), F32)], axis=-1)
    return ca, sa, cb, sb


def _regroup_w_in(w_in):
    splits = [512, 128, 128, 512, 64, 8, 256, 128, 32, 1024, 1024]
    offs = np.concatenate([[0], np.cumsum(splits)])
    seg = [w_in[:, offs[i]:offs[i + 1]] for i in range(len(splits))]
    a_q, a_k, a_v, i_q, i_k, i_w, b_cq, b_ckv, b_kr, g_a, g_b = seg
    pad = jnp.zeros((w_in.shape[0], LANES - 64 - 32 - 8), w_in.dtype)
    return jnp.concatenate(
        [a_q, a_k, a_v, i_q, b_cq, b_ckv, g_a, g_b, i_k, b_kr, i_w, pad], axis=1)


def _pad_heads(w, width):
    r, h, d = w.shape
    return jnp.pad(w, ((0, 0), (0, 0), (0, width - d))).reshape(r, h * width)


def _const_spec(shape):
    nd = len(shape)
    return pl.BlockSpec(shape, lambda *_: (0,) * nd)


def _proj_call(x, tables, gmix, wcat, gik, gq, gkv, wuq, wuk, wuv, tm):
    b, s, d = x.shape
    grid = (b, s // tm)
    tok = lambda w: pl.BlockSpec((1, tm, w), lambda i, j: (i, j, 0))
    head = lambda h, w: pl.BlockSpec((1, h, tm, w), lambda i, j: (i, 0, j, 0))
    in_specs = [tok(d)] + [tok(LANES)] * 4 + [
        _const_spec(gmix.shape), _const_spec(wcat.shape), _const_spec(gik.shape),
        _const_spec(gq.shape), _const_spec(gkv.shape), _const_spec(wuq.shape),
        _const_spec(wuk.shape), _const_spec(wuv.shape)]
    sds = jax.ShapeDtypeStruct
    out_shape = [
        sds((b, A_HEADS, s, A_HEAD_DIM), BF16),
        sds((b, A_KV_HEADS, s, A_HEAD_DIM), BF16),
        sds((b, A_KV_HEADS, s, LANES), BF16),
        sds((b, IDX_HEADS, s, IDX_DIM), BF16),
        sds((b, s, IDX_DIM), BF16),
        sds((b, s, LANES), F32),
        sds((b, B_HEADS, s, LANES), BF16),
        sds((b, B_HEADS, s, LANES), BF16),
        sds((b, B_HEADS, s, LANES), BF16),
        sds((b, s, d), BF16),
        sds((b, s, d), BF16),
    ]
    out_specs = [
        head(A_HEADS, A_HEAD_DIM), head(A_KV_HEADS, A_HEAD_DIM),
        head(A_KV_HEADS, LANES), head(IDX_HEADS, IDX_DIM), tok(IDX_DIM),
        tok(LANES), head(B_HEADS, LANES), head(B_HEADS, LANES),
        head(B_HEADS, LANES), tok(d), tok(d)]
    return pl.pallas_call(
        _proj_kernel,
        grid=grid,
        in_specs=in_specs,
        out_specs=out_specs,
        out_shape=out_shape,
        compiler_params=pltpu.CompilerParams(
            dimension_semantics=("arbitrary", "arbitrary"),
            vmem_limit_bytes=VMEM_LIMIT_BYTES),
        name="proj",
    )(x, *tables, gmix, wcat, gik, gq, gkv, wuq, wuk, wuv)


def _prepare_proj_weights(norm_mix_g, w_in, idx_k_norm_g, q_a_norm_g, kv_a_norm_g,
                          w_uq, w_uk, w_uv):
    gmix = norm_mix_g.reshape(1, -1)
    wcat = _regroup_w_in(w_in).astype(BF16)
    gik = jnp.pad(idx_k_norm_g, (0, LANES - IDX_DIM)).reshape(1, LANES)
    gq = q_a_norm_g.reshape(1, -1)
    gkv = kv_a_norm_g.reshape(1, -1)
    wuq = _pad_heads(w_uq, LANES).astype(BF16)
    wuk = _pad_heads(w_uk, LANES).astype(BF16)
    wuv = _pad_heads(w_uv, LANES).astype(BF16)
    return gmix, wcat, gik, gq, gkv, wuq, wuk, wuv


_INT_MIN = -2 ** 31
_ORD_LOWEST_FINITE = 0x00800000


def _ordinal_to_float(u):
    o = u ^ jnp.int32(_INT_MIN)
    bits = o ^ ((o >> 31) & jnp.int32(0x7FFFFFFF))
    return lax.bitcast_convert_type(bits, F32)


def _softmax_step(s, v, m_ref, acc_ref, h):
    m_old = m_ref[h]
    m_new = jnp.maximum(jnp.maximum(m_old, jnp.max(s, axis=-1, keepdims=True)), F32_LOWEST)
    p = jnp.concatenate(
        [jnp.exp2(s[:, j * LANES:(j + 1) * LANES] - m_new)
         for j in range(s.shape[1] // LANES)], axis=1)
    alpha = jnp.exp2(m_old - m_new)
    acc_ref[h] = alpha * acc_ref[h] + _dot(p.astype(BF16), v)
    m_ref[h] = m_new


def _init_softmax_state(m_ref, acc_ref):
    m_ref[...] = jnp.full(m_ref.shape, NEG_INF, F32)
    acc_ref[...] = jnp.zeros(acc_ref.shape, F32)


def _finish_heads(acc_even, acc_odd, lane):
    o0 = acc_even * (1.0 / acc_even[:, B_V_DIM:B_V_DIM + 1])
    o1 = acc_odd * (1.0 / acc_odd[:, B_V_DIM:B_V_DIM + 1])
    return jnp.where(lane < B_V_DIM, o0, pltpu.roll(o1, B_V_DIM, axis=1))


def _store_heads(o_ref, acc_ref, n_heads):
    lane = lax.broadcasted_iota(jnp.int32, acc_ref.shape[1:], 1)
    for pair in range(n_heads // 2):
        o_ref[0, :, pair * LANES:(pair + 1) * LANES] = _finish_heads(
            acc_ref[2 * pair], acc_ref[2 * pair + 1], lane).astype(o_ref.dtype)


def _dsa_kernel(ik_ref, iq_ref, iw_ref, aq_ref, ak_ref, av_ref, posk_ref, posq_ref,
                oa_ref, isc_ref, bias_ref, wt_ref, m_ref, acc_ref, *, topk):
    qi = pl.program_id(1)
    tq = aq_ref.shape[2]
    nk = qi + 1
    half = tq // 2
    wt_ref[...] = iw_ref[0].T[_T_IW:_T_IW + IDX_HEADS, :]
    posq = posq_ref[0]

    def idx_chunk(c, carry):
        for sub in range(2):
            k0 = pl.multiple_of(c * tq + sub * half, half)
            ikc = ik_ref[0, pl.ds(k0, half), :]
            acc = jnp.zeros((half, tq), F32)
            for h in range(IDX_HEADS):
                d = _dot_nt(ikc, iq_ref[0, h])
                acc = acc + wt_ref[h:h + 1, :] * jnp.maximum(d, 0.0)
            causal = posk_ref[0, pl.ds(k0, half), :] <= posq
            isc_ref[c, sub * half:(sub + 1) * half, :] = jnp.where(causal, acc, NEG_INF)
        return carry

    lax.fori_loop(0, nk, idx_chunk, 0)

    def count_ge(thr):
        def body(c, acc):
            hit = jnp.where(isc_ref[c] >= thr, 1.0, 0.0)
            return acc + jnp.sum(hit.reshape(tq // SUBLANES, SUBLANES, tq), axis=0)
        acc = lax.fori_loop(0, nk, body, jnp.zeros((SUBLANES, tq), F32))
        return jnp.sum(acc, axis=0, keepdims=True)

    def search_iter(it, carry):
        t_u, n_ge = carry
        cand = t_u | lax.shift_left(jnp.int32(1), 31 - it)
        cnt = count_ge(_ordinal_to_float(cand))
        ok = cnt >= topk
        return jnp.where(ok, cand, t_u), jnp.where(ok, cnt, n_ge)

    t_u, n_ge = lax.fori_loop(
        0, 32, search_iter,
        (jnp.zeros((1, tq), jnp.int32), jnp.zeros((1, tq), F32)))
    short = (t_u >= 0) & (t_u < _ORD_LOWEST_FINITE)
    thr = _ordinal_to_float(jnp.where(short, jnp.int32(_ORD_LOWEST_FINITE), t_u))
    tied = jnp.logical_and(n_ge > topk, jnp.logical_not(short))

    def write_bias(c, carry):
        sel = isc_ref[c] >= thr
        bias_ref[c] = jnp.where(sel, 0.0, NEG_INF).T
        return carry

    lax.fori_loop(0, nk, write_bias, 0)

    @pl.when(jnp.max(jnp.where(tied, 1.0, 0.0)) > 0.0)
    def _():
        def count_gt(c, acc):
            hit = jnp.where(isc_ref[c] > thr, 1.0, 0.0)
            return acc + jnp.sum(hit, axis=0, keepdims=True)
        n_gt = lax.fori_loop(0, nk, count_gt, jnp.zeros((1, tq), F32))
        need = topk - n_gt
        row = lax.broadcasted_iota(jnp.int32, (tq, tq), 0)
        col = lax.broadcasted_iota(jnp.int32, (tq, tq), 1)
        below = jnp.where(col < row, 1.0, 0.0).astype(BF16)

        def fix(c, seen):
            x = isc_ref[c]
            eq = jnp.where(x == thr, 1.0, 0.0)
            rank = _dot(below, eq.astype(BF16)) + seen
            sel = (x > thr) | ((x == thr) & (rank < need))
            bias_ref[c] = jnp.where(sel, 0.0, NEG_INF).T
            return seen + jnp.sum(eq, axis=0, keepdims=True)

        lax.fori_loop(0, nk, fix, jnp.zeros((1, tq), F32))

    _init_softmax_state(m_ref, acc_ref)

    def att(c, carry):
        k0 = pl.multiple_of(c * tq, tq)
        bias = bias_ref[c]
        for h in range(A_HEADS):
            g = h // A_GROUP
            s = _dot_nt(aq_ref[0, h], ak_ref[0, g, pl.ds(k0, tq), :]) + bias
            _softmax_step(s, av_ref[0, g, pl.ds(k0, tq), :], m_ref, acc_ref, h)
        return carry

    lax.fori_loop(0, nk, att, 0)
    _store_heads(oa_ref, acc_ref, A_HEADS)


def _dsa_call(ik, iq, iw, aq, ak, av, posk, posq, tq, topk):
    b, heads, s, _ = aq.shape
    nq = s // tq
    grid = (b, nq)
    qhead = lambda h, w: pl.BlockSpec((1, h, tq, w), lambda i, j: (i, 0, j, 0))
    khead = lambda h, w: pl.BlockSpec((1, h, s, w), lambda i, j: (i, 0, 0, 0))
    in_specs = [
        pl.BlockSpec((1, s, IDX_DIM), lambda i, j: (i, 0, 0)),
        qhead(IDX_HEADS, IDX_DIM),
        pl.BlockSpec((1, tq, LANES), lambda i, j: (i, j, 0)),
        qhead(A_HEADS, A_HEAD_DIM),
        khead(A_KV_HEADS, A_HEAD_DIM),
        khead(A_KV_HEADS, LANES),
        pl.BlockSpec((1, s, 1), lambda i, j: (i, 0, 0)),
        pl.BlockSpec((1, 1, tq), lambda i, j: (i, 0, j)),
    ]
    return pl.pallas_call(
        functools.partial(_dsa_kernel, topk=topk),
        grid=grid,
        in_specs=in_specs,
        out_specs=pl.BlockSpec((1, tq, heads * A_HEAD_DIM), lambda i, j: (i, j, 0)),
        out_shape=jax.ShapeDtypeStruct((b, s, heads * A_HEAD_DIM), BF16),
        scratch_shapes=[
            pltpu.VMEM((nq, tq, tq), F32),
            pltpu.VMEM((nq, tq, tq), F32),
            pltpu.VMEM((IDX_HEADS, tq), F32),
            pltpu.VMEM((heads, tq, LANES), F32),
            pltpu.VMEM((heads, tq, LANES), F32),
        ],
        compiler_params=pltpu.CompilerParams(
            dimension_semantics=("arbitrary", "arbitrary"),
            vmem_limit_bytes=VMEM_LIMIT_BYTES),
        name="dsa",
    )(ik, iq, iw, aq, ak, av, posk, posq)


def _mla_kernel(qb_ref, kb_ref, vb_ref, posq_ref, posk_ref, ob_ref, m_ref, acc_ref):
    qi = pl.program_id(1)
    tq = qb_ref.shape[2]
    _init_softmax_state(m_ref, acc_ref)

    def chunk(c, masked):
        k0 = pl.multiple_of(c * tq, tq)
        if masked:
            causal = posk_ref[0, :, pl.ds(k0, tq)] <= posq_ref[0]
        for h in range(B_HEADS):
            s = _dot_nt(qb_ref[0, h], kb_ref[0, h, pl.ds(k0, tq), :])
            if masked:
                s = jnp.where(causal, s, NEG_INF)
            _softmax_step(s, vb_ref[0, h, pl.ds(k0, tq), :], m_ref, acc_ref, h)

    def below_diagonal(c, carry):
        chunk(c, masked=False)
        return carry

    lax.fori_loop(0, qi, below_diagonal, 0)
    chunk(qi, masked=True)
    _store_heads(ob_ref, acc_ref, B_HEADS)


def _mla_call(qb, kb, vb, posq, posk, tq):
    b, heads, s, _ = qb.shape
    grid = (b, s // tq)
    return pl.pallas_call(
        _mla_kernel,
        grid=grid,
        in_specs=[
            pl.BlockSpec((1, heads, tq, LANES), lambda i, j: (i, 0, j, 0)),
            pl.BlockSpec((1, heads, s, LANES), lambda i, j: (i, 0, 0, 0)),
            pl.BlockSpec((1, heads, s, LANES), lambda i, j: (i, 0, 0, 0)),
            pl.BlockSpec((1, tq, 1), lambda i, j: (i, j, 0)),
            pl.BlockSpec((1, 1, s), lambda i, j: (i, 0, 0)),
        ],
        out_specs=pl.BlockSpec((1, tq, heads * B_V_DIM), lambda i, j: (i, j, 0)),
        out_shape=jax.ShapeDtypeStruct((b, s, heads * B_V_DIM), BF16),
        scratch_shapes=[
            pltpu.VMEM((heads, tq, LANES), F32),
            pltpu.VMEM((heads, tq, LANES), F32),
        ],
        compiler_params=pltpu.CompilerParams(
            dimension_semantics=("arbitrary", "arbitrary"),
            vmem_limit_bytes=VMEM_LIMIT_BYTES),
        name="mla",
    )(qb, kb, vb, posq, posk)


def _shift_rows(u, prev_tail, shift, row8):
    rolled = pltpu.roll(u, shift, axis=0)
    head = jnp.where(row8 < shift, pltpu.roll(prev_tail, shift, axis=0),
                     rolled[:SUBLANES])
    return jnp.concatenate([head, rolled[SUBLANES:]], axis=0)


def _ffn_kernel(x_ref, oa_ref, ob_ref, ga_ref, gb_ref, wba_ref, wbb_ref, wout_ref,
                gffn_ref, wup_ref, cw_ref, cb_ref, wdown_ref, gfin_ref,
                out_ref, tail_ref, *, n_chunks, final_norm):
    j = pl.program_id(1)
    x = x_ref[0]
    tm = x.shape[0]
    ya = _dot(oa_ref[0], wba_ref[...])
    yb = _dot(ob_ref[0], wbb_ref[...])
    merged = ga_ref[0].astype(F32) * ya + gb_ref[0].astype(F32) * yb
    h1 = x + _dot(merged.astype(BF16), wout_ref[...])
    ms = jnp.mean(h1 * h1, axis=-1, keepdims=True)
    hn = (h1 * lax.rsqrt(ms + EPS) * gffn_ref[...]).astype(BF16)

    fc = wdown_ref.shape[1]
    row8 = lax.broadcasted_iota(jnp.int32, (SUBLANES, fc), 0)

    @pl.when(j == 0)
    def _():
        tail_ref[...] = jnp.zeros_like(tail_ref)

    acc = h1
    for c in range(n_chunks):
        halves = []
        for part in range(2):
            idx = part * n_chunks + c
            u = _dot(hn, wup_ref[idx])
            prev_tail = tail_ref[idx]
            tail_ref[idx] = u[tm - SUBLANES:]
            y = (cb_ref[idx] + cw_ref[0, idx] * _shift_rows(u, prev_tail, 2, row8)
                 + cw_ref[1, idx] * _shift_rows(u, prev_tail, 1, row8)
                 + cw_ref[2, idx] * u)
            halves.append(y)
        gate, val = halves
        act = (gate * jax.nn.sigmoid(gate) * val).astype(BF16)
        acc = acc + _dot(act, wdown_ref[c])
    if final_norm:
        ms = jnp.mean(acc * acc, axis=-1, keepdims=True)
        acc = acc * lax.rsqrt(ms + EPS) * gfin_ref[...]
    out_ref[0] = acc


def _ffn_call(x, oa, ob, ga, gb, wba, wbb, wout, gffn, wup, cw, cb, wdown, gfin, tm,
              final_norm):
    b, s, d = x.shape
    n_parts, _, fc = wup.shape
    n_chunks = n_parts // 2
    tok = lambda w: pl.BlockSpec((1, tm, w), lambda i, j: (i, j, 0))
    weights = (wba, wbb, wout, gffn, wup, cw, cb, wdown, gfin)
    return pl.pallas_call(
        functools.partial(_ffn_kernel, n_chunks=n_chunks, final_norm=final_norm),
        grid=(b, s // tm),
        in_specs=[tok(d), tok(oa.shape[-1]), tok(ob.shape[-1]), tok(d), tok(d)]
        + [_const_spec(w.shape) for w in weights],
        out_specs=tok(d),
        out_shape=jax.ShapeDtypeStruct((b, s, d), x.dtype),
        scratch_shapes=[pltpu.VMEM((n_parts, SUBLANES, fc), F32)],
        compiler_params=pltpu.CompilerParams(
            dimension_semantics=("arbitrary", "arbitrary"),
            vmem_limit_bytes=VMEM_LIMIT_BYTES),
        name="ffn",
    )(x, oa, ob, ga, gb, *weights)


def _prepare_ffn_weights(w_branch_a, w_branch_b, w_out, norm_ffn_g, w_up, conv_w,
                         conv_b, w_down, norm_final_g, n_chunks):
    d, two_f = w_up.shape
    fc = two_f // (2 * n_chunks)
    wup = w_up.reshape(d, 2 * n_chunks, fc).transpose(1, 0, 2).astype(BF16)
    cw = conv_w.reshape(CONV_WIDTH, 2 * n_chunks, 1, fc)
    cb = conv_b.reshape(2 * n_chunks, 1, fc)
    wdown = w_down.reshape(n_chunks, fc, d).astype(BF16)
    return (w_branch_a.astype(BF16), w_branch_b.astype(BF16), w_out.astype(BF16),
            norm_ffn_g.reshape(1, -1), wup, cw, cb, wdown, norm_final_g.reshape(1, -1))


TM_PROJ = 512
TQ_ATTN = 256
TM_FFN = 512
FFN_CHUNKS = 2


def kernel(x, positions, norm_mix_g, w_in, idx_k_norm_g, q_a_norm_g, kv_a_norm_g, w_uq, w_uk, w_uv, w_branch_a, w_branch_b, w_out, norm_ffn_g, w_up, conv_w, conv_b, w_down, norm_final_g):
    b, s, _ = x.shape
    depth = w_in.shape[0]
    topk = min(INDEX_TOPK_MAX, s // 4)
    tables = _rope_tables(positions)
    pos_col = positions[:, :, None]
    pos_row = positions[:, None, :]
    h = x
    for l in range(depth):
        pw = _prepare_proj_weights(norm_mix_g[l], w_in[l], idx_k_norm_g[l], q_a_norm_g[l],
                                   kv_a_norm_g[l], w_uq[l], w_uk[l], w_uv[l])
        aq, ak, av, iq, ik, iw, qb, kb, vb, ga, gb = _proj_call(h, tables, *pw, tm=TM_PROJ)
        oa = _dsa_call(ik, iq, iw, aq, ak, av, pos_col, pos_row, TQ_ATTN, topk)
        ob = _mla_call(qb, kb, vb, pos_col, pos_row, TQ_ATTN)
        fw = _prepare_ffn_weights(w_branch_a[l], w_branch_b[l], w_out[l], norm_ffn_g[l],
                                  w_up[l], conv_w[l], conv_b[l], w_down[l], norm_final_g,
                                  FFN_CHUNKS)
        h = _ffn_call(h, oa, ob, ga, gb, *fw, tm=TM_FFN, final_norm=(l == depth - 1))
    return h
```

```python
import functools

import jax
import jax.numpy as jnp
import numpy as np
from jax import lax
from jax.experimental import pallas as pl
from jax.experimental.pallas import tpu as pltpu

ROPE_THETA = 500000.0
EPS = 1e-6
A_HEADS = 8
A_KV_HEADS = 2
A_GROUP = A_HEADS // A_KV_HEADS
A_HEAD_DIM = 64
A_ROT_DIM = 16
IDX_HEADS = 8
IDX_DIM = 64
INDEX_TOPK_MAX = 256
B_HEADS = 8
B_Q_RANK = 256
B_KV_RANK = 128
B_NOPE_DIM = 64
B_ROPE_DIM = 32
B_V_DIM = 64
CONV_WIDTH = 3

LANES = 128
SUBLANES = 8
VMEM_LIMIT_BYTES = 56 * 1024 * 1024

_O_AQ, _O_AK, _O_AV, _O_IQ, _O_CQ, _O_CKV, _O_GA, _O_GB, _O_TAIL = (
    0, 512, 640, 768, 1280, 1536, 1664, 2688, 3712)
_W_CAT = 3840
_T_KR, _T_IW = 64, 96

F32 = jnp.float32
BF16 = jnp.bfloat16
NEG_INF = float("-inf")
LOG2_E = 1.4426950408889634
F32_LOWEST = float(np.finfo(np.float32).min)


def _dot(a, b):
    return jnp.dot(a, b, preferred_element_type=F32)


def _dot_nt(a, b):
    return lax.dot_general(a, b, (((1,), (1,)), ((), ())), preferred_element_type=F32)


def _rope_lanes(x, cos_t, sin_t, take_up, half):
    up = pltpu.roll(x, LANES - half, axis=1)
    dn = pltpu.roll(x, half, axis=1)
    return x * cos_t + jnp.where(take_up, up, dn) * sin_t


def _expand_rope_tables(trig_t, expand_ref):
    t = trig_t.T
    a1 = t.astype(BF16)
    r1 = t - a1.astype(F32)
    a2 = r1.astype(BF16)
    a3 = (r1 - a2.astype(F32)).astype(BF16)
    tables = _dot(jnp.concatenate([a1, a2, a3], axis=1), expand_ref[...])
    return [tables[:, i * LANES:(i + 1) * LANES] for i in range(4)]


def _proj_kernel(x_ref, trig_ref, expand_ref, gmix_ref, w_ref, gik_ref,
                 gq_ref, gkv_ref, wuq_ref, wuk_ref, wuv_ref,
                 aq_ref, ak_ref, av_ref, iq_ref, ik_ref, iw_ref, qb_ref, kb_ref,
                 vb_ref, ga_ref, gb_ref):
    x = x_ref[0]
    tm = x.shape[0]
    ms = jnp.mean(x * x, axis=-1, keepdims=True)
    hn = (x * lax.rsqrt(ms + EPS) * gmix_ref[...]).astype(BF16)

    lane = lax.broadcasted_iota(jnp.int32, (tm, LANES), 1)
    ca, sa, cb, sb = _expand_rope_tables(trig_ref[0], expand_ref)
    up_a = (lane % A_HEAD_DIM) < (A_ROT_DIM // 2)
    up_b = lane < (B_NOPE_DIM + B_ROPE_DIM // 2)
    v_one = jnp.where(lane == A_HEAD_DIM, 1.0, 0.0).astype(F32)
    low_half = lane < A_HEAD_DIM

    def proj(off, width):
        return _dot(hn, w_ref[:, off:off + width])

    def store_heads(ref, tile128, first_head, scale):
        t = tile128 * scale if scale != 1.0 else tile128
        ref[0, first_head] = t[:, :A_HEAD_DIM].astype(ref.dtype)
        ref[0, first_head + 1] = t[:, A_HEAD_DIM:].astype(ref.dtype)

    for off, ref, scale in ((_O_AQ, aq_ref, A_HEAD_DIM ** -0.5 * LOG2_E),
                            (_O_IQ, iq_ref, IDX_DIM ** -0.5)):
        full = proj(off, 4 * LANES)
        for j in range(4):
            t = _rope_lanes(full[:, j * LANES:(j + 1) * LANES], ca, sa, up_a,
                            A_ROT_DIM // 2)
            store_heads(ref, t, 2 * j, scale)

    akv = proj(_O_AK, 2 * LANES)
    store_heads(ak_ref, _rope_lanes(akv[:, :LANES], ca, sa, up_a, A_ROT_DIM // 2),
                0, 1.0)
    v2 = akv[:, LANES:]
    av_ref[0, 0] = (jnp.where(low_half, v2, 0.0) + v_one).astype(BF16)
    v2r = pltpu.roll(v2, A_HEAD_DIM, axis=1)
    av_ref[0, 1] = (jnp.where(low_half, v2r, 0.0) + v_one).astype(BF16)

    tail = proj(_O_TAIL, LANES)
    ik_sq = jnp.where(low_half, tail * tail, 0.0)
    ik_ms = jnp.sum(ik_sq, axis=-1, keepdims=True) * (1.0 / IDX_DIM)
    ik_n = tail * lax.rsqrt(ik_ms + EPS) * gik_ref[...]
    ik_r = _rope_lanes(ik_n, ca, sa, up_a, A_ROT_DIM // 2)
    ik_ref[0] = ik_r[:, :IDX_DIM].astype(BF16)
    iw_ref[0] = tail * (IDX_HEADS ** -0.5)
    kr = _rope_lanes(tail, cb, sb, up_b, B_ROPE_DIM // 2)
    kr = jnp.where((lane >= B_NOPE_DIM) & (lane < B_NOPE_DIM + B_ROPE_DIM), kr, 0.0)

    cq = proj(_O_CQ, B_Q_RANK)
    cq_n = (cq * lax.rsqrt(jnp.mean(cq * cq, axis=-1, keepdims=True) + EPS)
            * gq_ref[...]).astype(BF16)
    ckv = proj(_O_CKV, B_KV_RANK)
    ckv_n = (ckv * lax.rsqrt(jnp.mean(ckv * ckv, axis=-1, keepdims=True) + EPS)
             * gkv_ref[...]).astype(BF16)
    b_scale = (B_NOPE_DIM + B_ROPE_DIM) ** -0.5 * LOG2_E
    for h in range(B_HEADS):
        q = _dot(cq_n, wuq_ref[:, h * LANES:(h + 1) * LANES])
        q = _rope_lanes(q, cb, sb, up_b, B_ROPE_DIM // 2) * b_scale
        qb_ref[0, h] = q.astype(BF16)
        k = _dot(ckv_n, wuk_ref[:, h * LANES:(h + 1) * LANES]) + kr
        kb_ref[0, h] = k.astype(BF16)
        v = _dot(ckv_n, wuv_ref[:, h * LANES:(h + 1) * LANES]) + v_one
        vb_ref[0, h] = v.astype(BF16)

    ga_ref[0] = jax.nn.sigmoid(proj(_O_GA, 1024)).astype(BF16)
    gb_ref[0] = jax.nn.sigmoid(proj(_O_GB, 1024)).astype(BF16)


_R_CA, _R_SA, _R_CB, _R_SB, _R_ONE = 0, 8, 16, 32, 48


def _rope_trig(positions):
    pos = positions.astype(F32)[:, None, :]

    def cs(rot):
        inv = ROPE_THETA ** (-jnp.arange(0, rot, 2, dtype=F32) / rot)
        ang = pos * inv[None, :, None]
        return jnp.cos(ang), jnp.sin(ang)

    b, s = positions.shape
    ca, sa = cs(A_ROT_DIM)
    cb, sb = cs(B_ROPE_DIM)
    pad = jnp.zeros((b, LANES - _R_ONE - 1, s), F32)
    return jnp.concatenate([ca, sa, cb, sb, jnp.ones((b, 1, s), F32), pad], axis=1)


def _rope_expansion():
    e = np.zeros((LANES, 4 * LANES), np.float32)
    half_a, half_b = A_ROT_DIM // 2, B_ROPE_DIM // 2
    for j in range(LANES):
        d = j % A_HEAD_DIM
        if d < A_ROT_DIM:
            e[_R_CA + d % half_a, j] = 1.0
            e[_R_SA + d % half_a, LANES + j] = -1.0 if d < half_a else 1.0
        else:
            e[_R_ONE, j] = 1.0
        d = j - B_NOPE_DIM
        if 0 <= d < B_ROPE_DIM:
            e[_R_CB + d % half_b, 2 * LANES + j] = 1.0
            e[_R_SB + d % half_b, 3 * LANES + j] = -1.0 if d < half_b else 1.0
        else:
            e[_R_ONE, 2 * LANES + j] = 1.0
    return jnp.asarray(np.concatenate([e, e, e], axis=0), BF16)


def _regroup_w_in(w_in):
    splits = [512, 128, 128, 512, 64, 8, 256, 128, 32, 1024, 1024]
    offs = np.concatenate([[0], np.cumsum(splits)])
    seg = [w_in[:, offs[i]:offs[i + 1]] for i in range(len(splits))]
    a_q, a_k, a_v, i_q, i_k, i_w, b_cq, b_ckv, b_kr, g_a, g_b = seg
    pad = jnp.zeros((w_in.shape[0], LANES - 64 - 32 - 8), w_in.dtype)
    return jnp.concatenate(
        [a_q, a_k, a_v, i_q, b_cq, b_ckv, g_a, g_b, i_k, b_kr, i_w, pad], axis=1)


def _pad_heads(w, width):
    r, h, d = w.shape
    return jnp.pad(w, ((0, 0), (0, 0), (0, width - d))).reshape(r, h * width)


def _const_spec(shape):
    nd = len(shape)
    return pl.BlockSpec(shape, lambda *_: (0,) * nd)


def _proj_call(x, trig, expand, gmix, wcat, gik, gq, gkv, wuq, wuk, wuv, tm):
    b, s, d = x.shape
    grid = (b, s // tm)
    tok = lambda w: pl.BlockSpec((1, tm, w), lambda i, j: (i, j, 0))
    head = lambda h, w: pl.BlockSpec((1, h, tm, w), lambda i, j: (i, 0, j, 0))
    in_specs = [tok(d), pl.BlockSpec((1, LANES, tm), lambda i, j: (i, 0, j)),
                _const_spec(expand.shape),
        _const_spec(gmix.shape), _const_spec(wcat.shape), _const_spec(gik.shape),
        _const_spec(gq.shape), _const_spec(gkv.shape), _const_spec(wuq.shape),
        _const_spec(wuk.shape), _const_spec(wuv.shape)]
    sds = jax.ShapeDtypeStruct
    out_shape = [
        sds((b, A_HEADS, s, A_HEAD_DIM), BF16),
        sds((b, A_KV_HEADS, s, A_HEAD_DIM), BF16),
        sds((b, A_KV_HEADS, s, LANES), BF16),
        sds((b, IDX_HEADS, s, IDX_DIM), BF16),
        sds((b, s, IDX_DIM), BF16),
        sds((b, s, LANES), F32),
        sds((b, B_HEADS, s, LANES), BF16),
        sds((b, B_HEADS, s, LANES), BF16),
        sds((b, B_HEADS, s, LANES), BF16),
        sds((b, s, d), BF16),
        sds((b, s, d), BF16),
    ]
    out_specs = [
        head(A_HEADS, A_HEAD_DIM), head(A_KV_HEADS, A_HEAD_DIM),
        head(A_KV_HEADS, LANES), head(IDX_HEADS, IDX_DIM), tok(IDX_DIM),
        tok(LANES), head(B_HEADS, LANES), head(B_HEADS, LANES),
        head(B_HEADS, LANES), tok(d), tok(d)]
    return pl.pallas_call(
        _proj_kernel,
        grid=grid,
        in_specs=in_specs,
        out_specs=out_specs,
        out_shape=out_shape,
        compiler_params=pltpu.CompilerParams(
            dimension_semantics=("arbitrary", "arbitrary"),
            vmem_limit_bytes=VMEM_LIMIT_BYTES),
        name="proj",
    )(x, trig, expand, gmix, wcat, gik, gq, gkv, wuq, wuk, wuv)


def _prepare_proj_weights(norm_mix_g, w_in, idx_k_norm_g, q_a_norm_g, kv_a_norm_g,
                          w_uq, w_uk, w_uv):
    gmix = norm_mix_g.reshape(1, -1)
    wcat = _regroup_w_in(w_in).astype(BF16)
    gik = jnp.pad(idx_k_norm_g, (0, LANES - IDX_DIM)).reshape(1, LANES)
    gq = q_a_norm_g.reshape(1, -1)
    gkv = kv_a_norm_g.reshape(1, -1)
    wuq = _pad_heads(w_uq, LANES).astype(BF16)
    wuk = _pad_heads(w_uk, LANES).astype(BF16)
    wuv = _pad_heads(w_uv, LANES).astype(BF16)
    return gmix, wcat, gik, gq, gkv, wuq, wuk, wuv


_INT_MIN = -2 ** 31
_ORD_LOWEST_FINITE = 0x00800000


def _ordinal_to_float(u):
    o = u ^ jnp.int32(_INT_MIN)
    bits = o ^ ((o >> 31) & jnp.int32(0x7FFFFFFF))
    return lax.bitcast_convert_type(bits, F32)


def _softmax_update(s, v, m_old, acc):
    m_new = jnp.maximum(jnp.maximum(m_old, jnp.max(s, axis=-1, keepdims=True)), F32_LOWEST)
    p = jnp.concatenate(
        [jnp.exp2(s[:, j * LANES:(j + 1) * LANES] - m_new)
         for j in range(s.shape[1] // LANES)], axis=1)
    alpha = jnp.exp2(m_old - m_new)
    return m_new, alpha * acc + _dot(p.astype(BF16), v)


def _attend_chunks(n_heads, m_ref, acc_ref, chunk_fns):
    for h in range(n_heads):
        m, acc = m_ref[h], acc_ref[h]
        for fn in chunk_fns:
            m, acc = _softmax_update(*fn(h), m, acc)
        m_ref[h] = m
        acc_ref[h] = acc


def _for_chunk_pairs(n, body):
    def pair(i, carry):
        body((2 * i, 2 * i + 1))
        return carry

    lax.fori_loop(0, lax.shift_right_logical(n, 1), pair, 0)

    @pl.when((n & 1) == 1)
    def _():
        body((n - 1,))


def _init_softmax_state(m_ref, acc_ref):
    m_ref[...] = jnp.full(m_ref.shape, NEG_INF, F32)
    acc_ref[...] = jnp.zeros(acc_ref.shape, F32)


def _finish_heads(acc_even, acc_odd, lane):
    o0 = acc_even * (1.0 / acc_even[:, B_V_DIM:B_V_DIM + 1])
    o1 = acc_odd * (1.0 / acc_odd[:, B_V_DIM:B_V_DIM + 1])
    return jnp.where(lane < B_V_DIM, o0, pltpu.roll(o1, B_V_DIM, axis=1))


def _store_heads(o_ref, acc_ref, n_heads):
    lane = lax.broadcasted_iota(jnp.int32, acc_ref.shape[1:], 1)
    for pair in range(n_heads // 2):
        o_ref[0, :, pair * LANES:(pair + 1) * LANES] = _finish_heads(
            acc_ref[2 * pair], acc_ref[2 * pair + 1], lane).astype(o_ref.dtype)


def _dsa_kernel(ik_ref, iq_ref, iw_ref, aq_ref, ak_ref, av_ref, posk_ref, posq_ref,
                oa_ref, isc_ref, bias_ref, wt_ref, m_ref, acc_ref, *, topk):
    qi = pl.program_id(1)
    tq = aq_ref.shape[2]
    nk = qi + 1
    half = tq // 2
    wt_ref[...] = iw_ref[0].T[_T_IW:_T_IW + IDX_HEADS, :]
    posq = posq_ref[0]

    def idx_chunk(c, carry):
        for sub in range(2):
            k0 = pl.multiple_of(c * tq + sub * half, half)
            ikc = ik_ref[0, pl.ds(k0, half), :]
            acc = jnp.zeros((half, tq), F32)
            for h in range(IDX_HEADS):
                d = _dot_nt(ikc, iq_ref[0, h])
                acc = acc + wt_ref[h:h + 1, :] * jnp.maximum(d, 0.0)
            causal = posk_ref[0, pl.ds(k0, half), :] <= posq
            isc_ref[c, sub * half:(sub + 1) * half, :] = jnp.where(causal, acc, NEG_INF)
        return carry

    lax.fori_loop(0, nk, idx_chunk, 0)

    n_pairs = lax.shift_right_logical(nk + 1, 1)

    @pl.when((nk & 1) == 1)
    def _():
        isc_ref[nk] = jnp.full((tq, tq), NEG_INF, F32)

    def count_ge(thr):
        def body(i, acc):
            for c in (2 * i, 2 * i + 1):
                hit = jnp.where(isc_ref[c] >= thr, 1.0, 0.0)
                acc = acc + jnp.sum(hit.reshape(tq // SUBLANES, SUBLANES, tq), axis=0)
            return acc
        acc = lax.fori_loop(0, n_pairs, body, jnp.zeros((SUBLANES, tq), F32))
        return jnp.sum(acc, axis=0, keepdims=True)

    def search_iter(it, carry):
        t_u, n_ge = carry
        cand = t_u | lax.shift_left(jnp.int32(1), 31 - it)
        cnt = count_ge(_ordinal_to_float(cand))
        ok = cnt >= topk
        return jnp.where(ok, cand, t_u), jnp.where(ok, cnt, n_ge)

    t_u, n_ge = lax.fori_loop(
        0, 32, search_iter,
        (jnp.zeros((1, tq), jnp.int32), jnp.zeros((1, tq), F32)))
    short = (t_u >= 0) & (t_u < _ORD_LOWEST_FINITE)
    thr = _ordinal_to_float(jnp.where(short, jnp.int32(_ORD_LOWEST_FINITE), t_u))
    tied = jnp.logical_and(n_ge > topk, jnp.logical_not(short))

    def write_bias(c, carry):
        sel = isc_ref[c] >= thr
        bias_ref[c] = jnp.where(sel, 0.0, NEG_INF).T
        return carry

    lax.fori_loop(0, nk, write_bias, 0)

    @pl.when(jnp.max(jnp.where(tied, 1.0, 0.0)) > 0.0)
    def _():
        def count_gt(c, acc):
            hit = jnp.where(isc_ref[c] > thr, 1.0, 0.0)
            return acc + jnp.sum(hit, axis=0, keepdims=True)
        n_gt = lax.fori_loop(0, nk, count_gt, jnp.zeros((1, tq), F32))
        need = topk - n_gt
        row = lax.broadcasted_iota(jnp.int32, (tq, tq), 0)
        col = lax.broadcasted_iota(jnp.int32, (tq, tq), 1)
        below = jnp.where(col < row, 1.0, 0.0).astype(BF16)

        def fix(c, seen):
            x = isc_ref[c]
            eq = jnp.where(x == thr, 1.0, 0.0)
            rank = _dot(below, eq.astype(BF16)) + seen
            sel = (x > thr) | ((x == thr) & (rank < need))
            bias_ref[c] = jnp.where(sel, 0.0, NEG_INF).T
            return seen + jnp.sum(eq, axis=0, keepdims=True)

        lax.fori_loop(0, nk, fix, jnp.zeros((1, tq), F32))

    _init_softmax_state(m_ref, acc_ref)

    def attend(chunks):
        def chunk_fn(c):
            k0 = pl.multiple_of(c * tq, tq)
            bias = bias_ref[c]

            def fn(h):
                g = h // A_GROUP
                s = _dot_nt(aq_ref[0, h], ak_ref[0, g, pl.ds(k0, tq), :]) + bias
                return s, av_ref[0, g, pl.ds(k0, tq), :]
            return fn

        _attend_chunks(A_HEADS, m_ref, acc_ref, [chunk_fn(c) for c in chunks])

    _for_chunk_pairs(nk, attend)
    _store_heads(oa_ref, acc_ref, A_HEADS)


def _dsa_call(ik, iq, iw, aq, ak, av, posk, posq, tq, topk):
    b, heads, s, _ = aq.shape
    nq = s // tq
    grid = (b, nq)
    qhead = lambda h, w: pl.BlockSpec((1, h, tq, w), lambda i, j: (i, 0, j, 0))
    khead = lambda h, w: pl.BlockSpec((1, h, s, w), lambda i, j: (i, 0, 0, 0))
    in_specs = [
        pl.BlockSpec((1, s, IDX_DIM), lambda i, j: (i, 0, 0)),
        qhead(IDX_HEADS, IDX_DIM),
        pl.BlockSpec((1, tq, LANES), lambda i, j: (i, j, 0)),
        qhead(A_HEADS, A_HEAD_DIM),
        khead(A_KV_HEADS, A_HEAD_DIM),
        khead(A_KV_HEADS, LANES),
        pl.BlockSpec((1, s, 1), lambda i, j: (i, 0, 0)),
        pl.BlockSpec((1, 1, tq), lambda i, j: (i, 0, j)),
    ]
    return pl.pallas_call(
        functools.partial(_dsa_kernel, topk=topk),
        grid=grid,
        in_specs=in_specs,
        out_specs=pl.BlockSpec((1, tq, heads * A_HEAD_DIM), lambda i, j: (i, j, 0)),
        out_shape=jax.ShapeDtypeStruct((b, s, heads * A_HEAD_DIM), BF16),
        scratch_shapes=[
            pltpu.VMEM((nq + 1, tq, tq), F32),
            pltpu.VMEM((nq, tq, tq), F32),
            pltpu.VMEM((IDX_HEADS, tq), F32),
            pltpu.VMEM((heads, tq, LANES), F32),
            pltpu.VMEM((heads, tq, LANES), F32),
        ],
        compiler_params=pltpu.CompilerParams(
            dimension_semantics=("arbitrary", "arbitrary"),
            vmem_limit_bytes=VMEM_LIMIT_BYTES),
        name="dsa",
    )(ik, iq, iw, aq, ak, av, posk, posq)


def _mla_kernel(qb_ref, kb_ref, vb_ref, posq_ref, posk_ref, ob_ref, m_ref, acc_ref):
    qi = pl.program_id(1)
    tq = qb_ref.shape[2]
    _init_softmax_state(m_ref, acc_ref)

    def chunk_fn(c, masked):
        k0 = pl.multiple_of(c * tq, tq)
        if masked:
            causal = posk_ref[0, :, pl.ds(k0, tq)] <= posq_ref[0]

        def fn(h):
            s = _dot_nt(qb_ref[0, h], kb_ref[0, h, pl.ds(k0, tq), :])
            if masked:
                s = jnp.where(causal, s, NEG_INF)
            return s, vb_ref[0, h, pl.ds(k0, tq), :]
        return fn

    def attend(chunks, masked_last):
        fns = [chunk_fn(c, masked_last and i == len(chunks) - 1)
               for i, c in enumerate(chunks)]
        _attend_chunks(B_HEADS, m_ref, acc_ref, fns)

    def pair(i, carry):
        attend((2 * i, 2 * i + 1), masked_last=False)
        return carry

    lax.fori_loop(0, lax.shift_right_logical(qi, 1), pair, 0)

    @pl.when((qi & 1) == 1)
    def _():
        attend((qi - 1, qi), masked_last=True)

    @pl.when((qi & 1) == 0)
    def _():
        attend((qi,), masked_last=True)

    _store_heads(ob_ref, acc_ref, B_HEADS)


def _mla_call(qb, kb, vb, posq, posk, tq):
    b, heads, s, _ = qb.shape
    grid = (b, s // tq)
    return pl.pallas_call(
        _mla_kernel,
        grid=grid,
        in_specs=[
            pl.BlockSpec((1, heads, tq, LANES), lambda i, j: (i, 0, j, 0)),
            pl.BlockSpec((1, heads, s, LANES), lambda i, j: (i, 0, 0, 0)),
            pl.BlockSpec((1, heads, s, LANES), lambda i, j: (i, 0, 0, 0)),
            pl.BlockSpec((1, tq, 1), lambda i, j: (i, j, 0)),
            pl.BlockSpec((1, 1, s), lambda i, j: (i, 0, 0)),
        ],
        out_specs=pl.BlockSpec((1, tq, heads * B_V_DIM), lambda i, j: (i, j, 0)),
        out_shape=jax.ShapeDtypeStruct((b, s, heads * B_V_DIM), BF16),
        scratch_shapes=[
            pltpu.VMEM((heads, tq, LANES), F32),
            pltpu.VMEM((heads, tq, LANES), F32),
        ],
        compiler_params=pltpu.CompilerParams(
            dimension_semantics=("arbitrary", "arbitrary"),
            vmem_limit_bytes=VMEM_LIMIT_BYTES),
        name="mla",
    )(qb, kb, vb, posq, posk)


def _shift_rows(u, prev_tail, shift, row8):
    rolled = pltpu.roll(u, shift, axis=0)
    head = jnp.where(row8 < shift, pltpu.roll(prev_tail, shift, axis=0),
                     rolled[:SUBLANES])
    return jnp.concatenate([head, rolled[SUBLANES:]], axis=0)


def _ffn_kernel(x_ref, oa_ref, ob_ref, ga_ref, gb_ref, wba_ref, wbb_ref, wout_ref,
                gffn_ref, wup_ref, cw_ref, cb_ref, wdown_ref, gfin_ref,
                out_ref, tail_ref, *, n_chunks, final_norm):
    j = pl.program_id(1)
    x = x_ref[0]
    tm = x.shape[0]
    ya = _dot(oa_ref[0], wba_ref[...])
    yb = _dot(ob_ref[0], wbb_ref[...])
    merged = ga_ref[0].astype(F32) * ya + gb_ref[0].astype(F32) * yb
    h1 = x + _dot(merged.astype(BF16), wout_ref[...])
    ms = jnp.mean(h1 * h1, axis=-1, keepdims=True)
    hn = (h1 * lax.rsqrt(ms + EPS) * gffn_ref[...]).astype(BF16)

    fc = wdown_ref.shape[1]
    row8 = lax.broadcasted_iota(jnp.int32, (SUBLANES, fc), 0)

    @pl.when(j == 0)
    def _():
        tail_ref[...] = jnp.zeros_like(tail_ref)

    acc = h1
    for c in range(n_chunks):
        halves = []
        for part in range(2):
            idx = part * n_chunks + c
            u = _dot(hn, wup_ref[idx])
            prev_tail = tail_ref[idx]
            tail_ref[idx] = u[tm - SUBLANES:]
            y = (cb_ref[idx] + cw_ref[0, idx] * _shift_rows(u, prev_tail, 2, row8)
                 + cw_ref[1, idx] * _shift_rows(u, prev_tail, 1, row8)
                 + cw_ref[2, idx] * u)
            halves.append(y)
        gate, val = halves
        act = (gate * jax.nn.sigmoid(gate) * val).astype(BF16)
        acc = acc + _dot(act, wdown_ref[c])
    if final_norm:
        ms = jnp.mean(acc * acc, axis=-1, keepdims=True)
        acc = acc * lax.rsqrt(ms + EPS) * gfin_ref[...]
    out_ref[0] = acc


def _ffn_call(x, oa, ob, ga, gb, wba, wbb, wout, gffn, wup, cw, cb, wdown, gfin, tm,
              final_norm):
    b, s, d = x.shape
    n_parts, _, fc = wup.shape
    n_chunks = n_parts // 2
    tok = lambda w: pl.BlockSpec((1, tm, w), lambda i, j: (i, j, 0))
    weights = (wba, wbb, wout, gffn, wup, cw, cb, wdown, gfin)
    return pl.pallas_call(
        functools.partial(_ffn_kernel, n_chunks=n_chunks, final_norm=final_norm),
        grid=(b, s // tm),
        in_specs=[tok(d), tok(oa.shape[-1]), tok(ob.shape[-1]), tok(d), tok(d)]
        + [_const_spec(w.shape) for w in weights],
        out_specs=tok(d),
        out_shape=jax.ShapeDtypeStruct((b, s, d), x.dtype),
        scratch_shapes=[pltpu.VMEM((n_parts, SUBLANES, fc), F32)],
        compiler_params=pltpu.CompilerParams(
            dimension_semantics=("arbitrary", "arbitrary"),
            vmem_limit_bytes=VMEM_LIMIT_BYTES),
        name="ffn",
    )(x, oa, ob, ga, gb, *weights)


def _prepare_ffn_weights(w_branch_a, w_branch_b, w_out, norm_ffn_g, w_up, conv_w,
                         conv_b, w_down, norm_final_g, n_chunks):
    d, two_f = w_up.shape
    fc = two_f // (2 * n_chunks)
    wup = w_up.reshape(d, 2 * n_chunks, fc).transpose(1, 0, 2).astype(BF16)
    cw = conv_w.reshape(CONV_WIDTH, 2 * n_chunks, 1, fc)
    cb = conv_b.reshape(2 * n_chunks, 1, fc)
    wdown = w_down.reshape(n_chunks, fc, d).astype(BF16)
    return (w_branch_a.astype(BF16), w_branch_b.astype(BF16), w_out.astype(BF16),
            norm_ffn_g.reshape(1, -1), wup, cw, cb, wdown, norm_final_g.reshape(1, -1))


TM_PROJ = 512
TQ_ATTN = 256
TM_FFN = 512
FFN_CHUNKS = 2


def kernel(x, positions, norm_mix_g, w_in, idx_k_norm_g, q_a_norm_g, kv_a_norm_g, w_uq, w_uk, w_uv, w_branch_a, w_branch_b, w_out, norm_ffn_g, w_up, conv_w, conv_b, w_down, norm_final_g):
    b, s, _ = x.shape
    depth = w_in.shape[0]
    topk = min(INDEX_TOPK_MAX, s // 4)
    trig = _rope_trig(positions)
    expand = _rope_expansion()
    pos_col = positions[:, :, None]
    pos_row = positions[:, None, :]
    h = x
    for l in range(depth):
        pw = _prepare_proj_weights(norm_mix_g[l], w_in[l], idx_k_norm_g[l], q_a_norm_g[l],
                                   kv_a_norm_g[l], w_uq[l], w_uk[l], w_uv[l])
        aq, ak, av, iq, ik, iw, qb, kb, vb, ga, gb = _proj_call(h, trig, expand, *pw, tm=TM_PROJ)
        oa = _dsa_call(ik, iq, iw, aq, ak, av, pos_col, pos_row, TQ_ATTN, topk)
        ob = _mla_call(qb, kb, vb, pos_col, pos_row, TQ_ATTN)
        fw = _prepare_ffn_weights(w_branch_a[l], w_branch_b[l], w_out[l], norm_ffn_g[l],
                                  w_up[l], conv_w[l], conv_b[l], w_down[l], norm_final_g,
                                  FFN_CHUNKS)
        h = _ffn_call(h, oa, ob, ga, gb, *fw, tm=TM_FFN, final_norm=(l == depth - 1))
    return h
```

```python
import functools

import jax
import jax.numpy as jnp
import numpy as np
from jax import lax
from jax.experimental import pallas as pl
from jax.experimental.pallas import tpu as pltpu

ROPE_THETA = 500000.0
EPS = 1e-6
A_HEADS = 8
A_KV_HEADS = 2
A_GROUP = A_HEADS // A_KV_HEADS
A_HEAD_DIM = 64
A_ROT_DIM = 16
IDX_HEADS = 8
IDX_DIM = 64
INDEX_TOPK_MAX = 256
B_HEADS = 8
B_Q_RANK = 256
B_KV_RANK = 128
B_NOPE_DIM = 64
B_ROPE_DIM = 32
B_V_DIM = 64
CONV_WIDTH = 3

LANES = 128
SUBLANES = 8
VMEM_LIMIT_BYTES = 56 * 1024 * 1024

_O_AQ, _O_AK, _O_AV, _O_IQ, _O_CQ, _O_CKV, _O_GA, _O_GB, _O_TAIL = (
    0, 512, 640, 768, 1280, 1536, 1664, 2688, 3712)
_W_CAT = 3840
_T_KR, _T_IW = 64, 96

F32 = jnp.float32
BF16 = jnp.bfloat16
NEG_INF = float("-inf")
LOG2_E = 1.4426950408889634
F32_LOWEST = float(np.finfo(np.float32).min)


def _dot(a, b):
    return jnp.dot(a, b, preferred_element_type=F32)


def _dot_nt(a, b):
    return lax.dot_general(a, b, (((1,), (1,)), ((), ())), preferred_element_type=F32)


def _rope_lanes(x, cos_t, sin_t, take_up, half):
    up = pltpu.roll(x, LANES - half, axis=1)
    dn = pltpu.roll(x, half, axis=1)
    return x * cos_t + jnp.where(take_up, up, dn) * sin_t


def _expand_rope_tables(trig_t, expand_ref):
    t = jnp.concatenate([trig_t, jnp.zeros_like(trig_t)], axis=0).T
    a1 = t.astype(BF16)
    r1 = t - a1.astype(F32)
    a2 = r1.astype(BF16)
    a3 = (r1 - a2.astype(F32)).astype(BF16)
    tables = _dot(jnp.concatenate([a1, a2, a3], axis=1), expand_ref[...])
    return [tables[:, i * LANES:(i + 1) * LANES] for i in range(4)]


def _proj_kernel(x_ref, trig_ref, expand_ref, gmix_ref, w_ref, gik_ref,
                 gq_ref, gkv_ref, wuq_ref, wuk_ref, wuv_ref,
                 aq_ref, ak_ref, av_ref, iq_ref, ik_ref, iw_ref, qb_ref, kb_ref,
                 vb_ref, ga_ref, gb_ref):
    x = x_ref[0]
    tm = x.shape[0]
    ms = jnp.mean(x * x, axis=-1, keepdims=True)
    hn = (x * lax.rsqrt(ms + EPS) * gmix_ref[...]).astype(BF16)

    lane = lax.broadcasted_iota(jnp.int32, (tm, LANES), 1)
    ca, sa, cb, sb = _expand_rope_tables(trig_ref[0], expand_ref)
    up_a = (lane % A_HEAD_DIM) < (A_ROT_DIM // 2)
    up_b = lane < (B_NOPE_DIM + B_ROPE_DIM // 2)
    v_one = jnp.where(lane == A_HEAD_DIM, 1.0, 0.0).astype(F32)
    low_half = lane < A_HEAD_DIM

    def proj(off, width):
        return _dot(hn, w_ref[:, off:off + width])

    def store_heads(ref, tile128, first_head, scale):
        t = tile128 * scale if scale != 1.0 else tile128
        ref[0, first_head] = t[:, :A_HEAD_DIM].astype(ref.dtype)
        ref[0, first_head + 1] = t[:, A_HEAD_DIM:].astype(ref.dtype)

    for off, ref, scale in ((_O_AQ, aq_ref, A_HEAD_DIM ** -0.5 * LOG2_E),
                            (_O_IQ, iq_ref, IDX_DIM ** -0.5)):
        full = proj(off, 4 * LANES)
        for j in range(4):
            t = _rope_lanes(full[:, j * LANES:(j + 1) * LANES], ca, sa, up_a,
                            A_ROT_DIM // 2)
            store_heads(ref, t, 2 * j, scale)

    akv = proj(_O_AK, 2 * LANES)
    store_heads(ak_ref, _rope_lanes(akv[:, :LANES], ca, sa, up_a, A_ROT_DIM // 2),
                0, 1.0)
    v2 = akv[:, LANES:]
    av_ref[0, 0] = (jnp.where(low_half, v2, 0.0) + v_one).astype(BF16)
    v2r = pltpu.roll(v2, A_HEAD_DIM, axis=1)
    av_ref[0, 1] = (jnp.where(low_half, v2r, 0.0) + v_one).astype(BF16)

    tail = proj(_O_TAIL, LANES)
    ik_sq = jnp.where(low_half, tail * tail, 0.0)
    ik_ms = jnp.sum(ik_sq, axis=-1, keepdims=True) * (1.0 / IDX_DIM)
    ik_n = tail * lax.rsqrt(ik_ms + EPS) * gik_ref[...]
    ik_r = _rope_lanes(ik_n, ca, sa, up_a, A_ROT_DIM // 2)
    ik_ref[0] = ik_r[:, :IDX_DIM].astype(BF16)
    iw_ref[0] = tail * (IDX_HEADS ** -0.5)
    kr = _rope_lanes(tail, cb, sb, up_b, B_ROPE_DIM // 2)
    kr = jnp.where((lane >= B_NOPE_DIM) & (lane < B_NOPE_DIM + B_ROPE_DIM), kr, 0.0)

    cq = proj(_O_CQ, B_Q_RANK)
    cq_n = (cq * lax.rsqrt(jnp.mean(cq * cq, axis=-1, keepdims=True) + EPS)
            * gq_ref[...]).astype(BF16)
    ckv = proj(_O_CKV, B_KV_RANK)
    ckv_n = (ckv * lax.rsqrt(jnp.mean(ckv * ckv, axis=-1, keepdims=True) + EPS)
             * gkv_ref[...]).astype(BF16)
    b_scale = (B_NOPE_DIM + B_ROPE_DIM) ** -0.5 * LOG2_E
    for h0 in range(0, B_HEADS, 2):
        cols = slice(h0 * LANES, (h0 + 2) * LANES)
        q2 = _dot(cq_n, wuq_ref[:, cols])
        k2 = _dot(ckv_n, wuk_ref[:, cols])
        v2 = _dot(ckv_n, wuv_ref[:, cols])
        for i in range(2):
            lanes = slice(i * LANES, (i + 1) * LANES)
            q = _rope_lanes(q2[:, lanes], cb, sb, up_b, B_ROPE_DIM // 2) * b_scale
            qb_ref[0, h0 + i] = q.astype(BF16)
            kb_ref[0, h0 + i] = (k2[:, lanes] + kr).astype(BF16)
            vb_ref[0, h0 + i] = (v2[:, lanes] + v_one).astype(BF16)

    ga_ref[0] = jax.nn.sigmoid(proj(_O_GA, 1024)).astype(BF16)
    gb_ref[0] = jax.nn.sigmoid(proj(_O_GB, 1024)).astype(BF16)


_R_CA, _R_SA, _R_CB, _R_SB, _R_ONE = 0, 8, 16, 32, 48
_TRIG_ROWS = 64


def _rope_trig(positions):
    pos = positions.astype(F32)[:, None, :]

    def cs(rot):
        inv = ROPE_THETA ** (-jnp.arange(0, rot, 2, dtype=F32) / rot)
        ang = pos * inv[None, :, None]
        return jnp.cos(ang), jnp.sin(ang)

    b, s = positions.shape
    ca, sa = cs(A_ROT_DIM)
    cb, sb = cs(B_ROPE_DIM)
    pad = jnp.zeros((b, _TRIG_ROWS - _R_ONE - 1, s), F32)
    return jnp.concatenate([ca, sa, cb, sb, jnp.ones((b, 1, s), F32), pad], axis=1)


def _rope_expansion():
    e = np.zeros((LANES, 4 * LANES), np.float32)
    half_a, half_b = A_ROT_DIM // 2, B_ROPE_DIM // 2
    for j in range(LANES):
        d = j % A_HEAD_DIM
        if d < A_ROT_DIM:
            e[_R_CA + d % half_a, j] = 1.0
            e[_R_SA + d % half_a, LANES + j] = -1.0 if d < half_a else 1.0
        else:
            e[_R_ONE, j] = 1.0
        d = j - B_NOPE_DIM
        if 0 <= d < B_ROPE_DIM:
            e[_R_CB + d % half_b, 2 * LANES + j] = 1.0
            e[_R_SB + d % half_b, 3 * LANES + j] = -1.0 if d < half_b else 1.0
        else:
            e[_R_ONE, 2 * LANES + j] = 1.0
    return jnp.asarray(np.concatenate([e, e, e], axis=0), BF16)


def _regroup_w_in(w_in):
    splits = [512, 128, 128, 512, 64, 8, 256, 128, 32, 1024, 1024]
    offs = np.concatenate([[0], np.cumsum(splits)])
    seg = [w_in[:, offs[i]:offs[i + 1]] for i in range(len(splits))]
    a_q, a_k, a_v, i_q, i_k, i_w, b_cq, b_ckv, b_kr, g_a, g_b = seg
    pad = jnp.zeros((w_in.shape[0], LANES - 64 - 32 - 8), w_in.dtype)
    return jnp.concatenate(
        [a_q, a_k, a_v, i_q, b_cq, b_ckv, g_a, g_b, i_k, b_kr, i_w, pad], axis=1)


def _pad_heads(w, width):
    r, h, d = w.shape
    return jnp.pad(w, ((0, 0), (0, 0), (0, width - d))).reshape(r, h * width)


def _const_spec(shape):
    nd = len(shape)
    return pl.BlockSpec(shape, lambda *_: (0,) * nd)


def _proj_call(x, trig, expand, gmix, wcat, gik, gq, gkv, wuq, wuk, wuv, tm):
    b, s, d = x.shape
    grid = (b, s // tm)
    tok = lambda w: pl.BlockSpec((1, tm, w), lambda i, j: (i, j, 0))
    head = lambda h, w: pl.BlockSpec((1, h, tm, w), lambda i, j: (i, 0, j, 0))
    in_specs = [tok(d), pl.BlockSpec((1, _TRIG_ROWS, tm), lambda i, j: (i, 0, j)),
                _const_spec(expand.shape),
        _const_spec(gmix.shape), _const_spec(wcat.shape), _const_spec(gik.shape),
        _const_spec(gq.shape), _const_spec(gkv.shape), _const_spec(wuq.shape),
        _const_spec(wuk.shape), _const_spec(wuv.shape)]
    sds = jax.ShapeDtypeStruct
    out_shape = [
        sds((b, A_HEADS, s, A_HEAD_DIM), BF16),
        sds((b, A_KV_HEADS, s, A_HEAD_DIM), BF16),
        sds((b, A_KV_HEADS, s, LANES), BF16),
        sds((b, IDX_HEADS, s, IDX_DIM), BF16),
        sds((b, s, IDX_DIM), BF16),
        sds((b, s, LANES), F32),
        sds((b, B_HEADS, s, LANES), BF16),
        sds((b, B_HEADS, s, LANES), BF16),
        sds((b, B_HEADS, s, LANES), BF16),
        sds((b, s, d), BF16),
        sds((b, s, d), BF16),
    ]
    out_specs = [
        head(A_HEADS, A_HEAD_DIM), head(A_KV_HEADS, A_HEAD_DIM),
        head(A_KV_HEADS, LANES), head(IDX_HEADS, IDX_DIM), tok(IDX_DIM),
        tok(LANES), head(B_HEADS, LANES), head(B_HEADS, LANES),
        head(B_HEADS, LANES), tok(d), tok(d)]
    return pl.pallas_call(
        _proj_kernel,
        grid=grid,
        in_specs=in_specs,
        out_specs=out_specs,
        out_shape=out_shape,
        compiler_params=pltpu.CompilerParams(
            dimension_semantics=("arbitrary", "arbitrary"),
            vmem_limit_bytes=VMEM_LIMIT_BYTES),
        name="proj",
    )(x, trig, expand, gmix, wcat, gik, gq, gkv, wuq, wuk, wuv)


def _prepare_proj_weights(norm_mix_g, w_in, idx_k_norm_g, q_a_norm_g, kv_a_norm_g,
                          w_uq, w_uk, w_uv):
    gmix = norm_mix_g.reshape(1, -1)
    wcat = _regroup_w_in(w_in).astype(BF16)
    gik = jnp.pad(idx_k_norm_g, (0, LANES - IDX_DIM)).reshape(1, LANES)
    gq = q_a_norm_g.reshape(1, -1)
    gkv = kv_a_norm_g.reshape(1, -1)
    wuq = _pad_heads(w_uq, LANES).astype(BF16)
    wuk = _pad_heads(w_uk, LANES).astype(BF16)
    wuv = _pad_heads(w_uv, LANES).astype(BF16)
    return gmix, wcat, gik, gq, gkv, wuq, wuk, wuv


_INT_MIN = -2 ** 31
_ORD_NEG_INF = 0x007FFFFF
_ORD_LOWEST_FINITE = 0x00800000


def _ordinal_to_float(u):
    o = u ^ jnp.int32(_INT_MIN)
    bits = o ^ ((o >> 31) & jnp.int32(0x7FFFFFFF))
    return lax.bitcast_convert_type(bits, F32)


def _softmax_update(s, v, m_old, acc):
    m_new = jnp.maximum(jnp.maximum(m_old, jnp.max(s, axis=-1, keepdims=True)), F32_LOWEST)
    p = jnp.concatenate(
        [jnp.exp2(s[:, j * LANES:(j + 1) * LANES] - m_new)
         for j in range(s.shape[1] // LANES)], axis=1)
    alpha = jnp.exp2(m_old - m_new)
    return m_new, alpha * acc + _dot(p.astype(BF16), v)


def _attend_chunks(n_heads, m_ref, acc_ref, chunk_fns):
    for h in range(n_heads):
        m, acc = m_ref[h], acc_ref[h]
        for fn in chunk_fns:
            m, acc = _softmax_update(*fn(h), m, acc)
        m_ref[h] = m
        acc_ref[h] = acc


def _for_chunk_groups(n, body):
    def group(i, carry):
        body(tuple(4 * i + r for r in range(4)))
        return carry

    lax.fori_loop(0, lax.shift_right_logical(n, 2), group, 0)
    done = n & ~3

    @pl.when((n & 2) != 0)
    def _():
        body((done, done + 1))

    @pl.when((n & 1) != 0)
    def _():
        body((n - 1,))


def _init_softmax_state(m_ref, acc_ref):
    m_ref[...] = jnp.full(m_ref.shape, NEG_INF, F32)
    acc_ref[...] = jnp.zeros(acc_ref.shape, F32)


def _finish_heads(acc_even, acc_odd, lane):
    o0 = acc_even * (1.0 / acc_even[:, B_V_DIM:B_V_DIM + 1])
    o1 = acc_odd * (1.0 / acc_odd[:, B_V_DIM:B_V_DIM + 1])
    return jnp.where(lane < B_V_DIM, o0, pltpu.roll(o1, B_V_DIM, axis=1))


def _store_heads(o_ref, acc_ref, n_heads):
    lane = lax.broadcasted_iota(jnp.int32, acc_ref.shape[1:], 1)
    for pair in range(n_heads // 2):
        o_ref[0, :, pair * LANES:(pair + 1) * LANES] = _finish_heads(
            acc_ref[2 * pair], acc_ref[2 * pair + 1], lane).astype(o_ref.dtype)


def _dsa_kernel(ik_ref, iq_ref, iw_ref, aq_ref, ak_ref, av_ref, posk_ref, posq_ref,
                oa_ref, isc_ref, xb_ref, bias_ref, wt_ref, m_ref, acc_ref, *, topk):
    qi = pl.program_id(1)
    tq = aq_ref.shape[2]
    nk = qi + 1
    half = tq // 2
    wt_ref[...] = iw_ref[0].T[_T_IW:_T_IW + IDX_HEADS, :]
    posq = posq_ref[0]

    def idx_chunk(c, carry):
        for sub in range(2):
            k0 = pl.multiple_of(c * tq + sub * half, half)
            ikc = ik_ref[0, pl.ds(k0, half), :]
            acc = jnp.zeros((half, tq), F32)
            for h in range(IDX_HEADS):
                d = _dot_nt(ikc, iq_ref[0, h])
                acc = acc + wt_ref[h:h + 1, :] * jnp.maximum(d, 0.0)
            causal = posk_ref[0, pl.ds(k0, half), :] <= posq
            score = jnp.where(causal, acc, NEG_INF)
            isc_ref[c, sub * half:(sub + 1) * half, :] = score
            xb_ref[c, sub * half:(sub + 1) * half, :] = score.astype(BF16)
        return carry

    lax.fori_loop(0, nk, idx_chunk, 0)

    n_pairs = lax.shift_right_logical(nk + 1, 1)

    @pl.when((nk & 1) == 1)
    def _():
        isc_ref[nk] = jnp.full((tq, tq), NEG_INF, F32)
        xb_ref[nk] = jnp.full((tq, tq), NEG_INF, BF16)

    packed = 2 * SUBLANES
    one_b, zero_b = jnp.asarray(1, BF16), jnp.asarray(0, BF16)

    def coarse_ordinal(j):
        low = jnp.where(j < 0x8000, jnp.int32(0xFFFF), jnp.int32(0))
        return lax.shift_left(j, 16) | low

    def coarse_iter(it, j):
        cand = j | lax.shift_left(jnp.int32(1), 15 - it)
        thr_b = _ordinal_to_float(coarse_ordinal(cand)).astype(BF16)

        def body(i, acc):
            for c in (2 * i, 2 * i + 1):
                hit = jnp.where(xb_ref[c] >= thr_b, one_b, zero_b)
                for r in range(tq // packed):
                    acc = acc + hit[r * packed:(r + 1) * packed]
            return acc

        acc = lax.fori_loop(0, n_pairs, body, jnp.zeros((packed, tq), BF16))
        cnt = jnp.sum(acc.astype(F32), axis=0, keepdims=True)
        return jnp.where(cnt >= topk, cand, j)

    j_hi = lax.fori_loop(0, 16, coarse_iter, jnp.zeros((1, tq), jnp.int32))
    j_hi = jnp.maximum(j_hi, _ORD_NEG_INF >> 16)

    base = coarse_ordinal(j_hi) - jnp.int32(1 << 16)

    def fine_iter(it, d):
        cand = d | lax.shift_left(jnp.int32(1), 16 - it)
        thr = _ordinal_to_float(base + cand)

        def body(i, acc):
            for c in (2 * i, 2 * i + 1):
                hit = jnp.where(isc_ref[c] >= thr, 1.0, 0.0)
                acc = acc + jnp.sum(hit.reshape(tq // SUBLANES, SUBLANES, tq), axis=0)
            return acc

        acc = lax.fori_loop(0, n_pairs, body, jnp.zeros((SUBLANES, tq), F32))
        cnt = jnp.sum(acc, axis=0, keepdims=True)
        return jnp.where(cnt >= topk, cand, d)

    t_u = base + lax.fori_loop(0, 17, fine_iter, jnp.zeros((1, tq), jnp.int32))
    short = (t_u >= 0) & (t_u < _ORD_LOWEST_FINITE)
    thr = _ordinal_to_float(jnp.where(short, jnp.int32(_ORD_LOWEST_FINITE), t_u))

    def write_bias(c, n_ge):
        sel = isc_ref[c] >= thr
        bias_ref[c] = jnp.where(sel, 0.0, NEG_INF).T
        return n_ge + jnp.sum(jnp.where(sel, 1.0, 0.0), axis=0, keepdims=True)

    n_ge = lax.fori_loop(0, nk, write_bias, jnp.zeros((1, tq), F32))
    tied = jnp.logical_and(n_ge > topk, jnp.logical_not(short))

    @pl.when(jnp.max(jnp.where(tied, 1.0, 0.0)) > 0.0)
    def _():
        def count_gt(c, acc):
            hit = jnp.where(isc_ref[c] > thr, 1.0, 0.0)
            return acc + jnp.sum(hit, axis=0, keepdims=True)
        n_gt = lax.fori_loop(0, nk, count_gt, jnp.zeros((1, tq), F32))
        need = topk - n_gt
        row = lax.broadcasted_iota(jnp.int32, (tq, tq), 0)
        col = lax.broadcasted_iota(jnp.int32, (tq, tq), 1)
        below = jnp.where(col < row, 1.0, 0.0).astype(BF16)

        def fix(c, seen):
            x = isc_ref[c]
            eq = jnp.where(x == thr, 1.0, 0.0)
            rank = _dot(below, eq.astype(BF16)) + seen
            sel = (x > thr) | ((x == thr) & (rank < need))
            bias_ref[c] = jnp.where(sel, 0.0, NEG_INF).T
            return seen + jnp.sum(eq, axis=0, keepdims=True)

        lax.fori_loop(0, nk, fix, jnp.zeros((1, tq), F32))

    _init_softmax_state(m_ref, acc_ref)

    def attend(chunks):
        def chunk_fn(c):
            k0 = pl.multiple_of(c * tq, tq)
            bias = bias_ref[c]

            def fn(h):
                g = h // A_GROUP
                s = _dot_nt(aq_ref[0, h], ak_ref[0, g, pl.ds(k0, tq), :]) + bias
                return s, av_ref[0, g, pl.ds(k0, tq), :]
            return fn

        _attend_chunks(A_HEADS, m_ref, acc_ref, [chunk_fn(c) for c in chunks])

    _for_chunk_groups(nk, attend)
    _store_heads(oa_ref, acc_ref, A_HEADS)


def _dsa_call(ik, iq, iw, aq, ak, av, positions, tq, topk):
    b, heads, s, _ = aq.shape
    nq = s // tq
    assert s % tq == 0
    assert (nq + 1) * tq // (2 * SUBLANES) <= 256
    grid = (b, nq)
    qhead = lambda h, w: pl.BlockSpec((1, h, tq, w), lambda i, j: (i, 0, j, 0))
    khead = lambda h, w: pl.BlockSpec((1, h, s, w), lambda i, j: (i, 0, 0, 0))
    in_specs = [
        pl.BlockSpec((1, s, IDX_DIM), lambda i, j: (i, 0, 0)),
        qhead(IDX_HEADS, IDX_DIM),
        pl.BlockSpec((1, tq, LANES), lambda i, j: (i, j, 0)),
        qhead(A_HEADS, A_HEAD_DIM),
        khead(A_KV_HEADS, A_HEAD_DIM),
        khead(A_KV_HEADS, LANES),
        pl.BlockSpec((1, s, 1), lambda i, j: (i, 0, 0)),
        pl.BlockSpec((1, 1, tq), lambda i, j: (i, 0, j)),
    ]
    state = pltpu.VMEM((heads, tq, LANES), F32)
    return pl.pallas_call(
        functools.partial(_dsa_kernel, topk=topk),
        grid=grid,
        in_specs=in_specs,
        out_specs=pl.BlockSpec((1, tq, heads * A_HEAD_DIM), lambda i, j: (i, j, 0)),
        out_shape=jax.ShapeDtypeStruct((b, s, heads * A_HEAD_DIM), BF16),
        scratch_shapes=[
            pltpu.VMEM((nq + 1, tq, tq), F32),
            pltpu.VMEM((nq + 1, tq, tq), BF16),
            pltpu.VMEM((nq, tq, tq), F32),
            pltpu.VMEM((IDX_HEADS, tq), F32),
            state, state,
        ],
        compiler_params=pltpu.CompilerParams(
            dimension_semantics=("arbitrary", "arbitrary"),
            vmem_limit_bytes=VMEM_LIMIT_BYTES),
        name="dsa",
    )(ik, iq, iw, aq, ak, av, positions[:, :, None], positions[:, None, :])


def _mla_kernel(qb_ref, kb_ref, vb_ref, posq_ref, posk_ref, ob_ref, m_ref, acc_ref):
    qi = pl.program_id(1)
    tq = qb_ref.shape[2]
    _init_softmax_state(m_ref, acc_ref)

    def chunk_fn(c, masked):
        k0 = pl.multiple_of(c * tq, tq)
        if masked:
            causal = posk_ref[0, :, pl.ds(k0, tq)] <= posq_ref[0]

        def fn(h):
            s = _dot_nt(qb_ref[0, h], kb_ref[0, h, pl.ds(k0, tq), :])
            if masked:
                s = jnp.where(causal, s, NEG_INF)
            return s, vb_ref[0, h, pl.ds(k0, tq), :]
        return fn

    def attend(chunks):
        fns = [chunk_fn(c, masked=(i == len(chunks) - 1)) for i, c in enumerate(chunks)]
        _attend_chunks(B_HEADS, m_ref, acc_ref, fns)

    _for_chunk_groups(qi + 1, attend)
    _store_heads(ob_ref, acc_ref, B_HEADS)


def _mla_call(qb, kb, vb, positions, tq):
    b, heads, s, _ = qb.shape
    state = pltpu.VMEM((heads, tq, LANES), F32)
    return pl.pallas_call(
        _mla_kernel,
        grid=(b, s // tq),
        in_specs=[
            pl.BlockSpec((1, heads, tq, LANES), lambda i, j: (i, 0, j, 0)),
            pl.BlockSpec((1, heads, s, LANES), lambda i, j: (i, 0, 0, 0)),
            pl.BlockSpec((1, heads, s, LANES), lambda i, j: (i, 0, 0, 0)),
            pl.BlockSpec((1, tq, 1), lambda i, j: (i, j, 0)),
            pl.BlockSpec((1, 1, s), lambda i, j: (i, 0, 0)),
        ],
        out_specs=pl.BlockSpec((1, tq, heads * B_V_DIM), lambda i, j: (i, j, 0)),
        out_shape=jax.ShapeDtypeStruct((b, s, heads * B_V_DIM), BF16),
        scratch_shapes=[state, state],
        compiler_params=pltpu.CompilerParams(
            dimension_semantics=("arbitrary", "arbitrary"),
            vmem_limit_bytes=VMEM_LIMIT_BYTES),
        name="mla",
    )(qb, kb, vb, positions[:, :, None], positions[:, None, :])


def _shift_rows(u, prev_tail, shift, row8):
    rolled = pltpu.roll(u, shift, axis=0)
    head = jnp.where(row8 < shift, pltpu.roll(prev_tail, shift, axis=0),
                     rolled[:SUBLANES])
    return jnp.concatenate([head, rolled[SUBLANES:]], axis=0)


def _ffn_kernel(x_ref, oa_ref, ob_ref, ga_ref, gb_ref, wba_ref, wbb_ref, wout_ref,
                gffn_ref, wup_ref, cw_ref, cb_ref, wdown_ref, gfin_ref,
                out_ref, tail_ref, *, n_chunks, final_norm):
    j = pl.program_id(1)
    x = x_ref[0]
    tm = x.shape[0]
    ya = _dot(oa_ref[0], wba_ref[...])
    yb = _dot(ob_ref[0], wbb_ref[...])
    merged = ga_ref[0].astype(F32) * ya + gb_ref[0].astype(F32) * yb
    h1 = x + _dot(merged.astype(BF16), wout_ref[...])
    ms = jnp.mean(h1 * h1, axis=-1, keepdims=True)
    hn = (h1 * lax.rsqrt(ms + EPS) * gffn_ref[...]).astype(BF16)

    fc = wdown_ref.shape[1]
    row8 = lax.broadcasted_iota(jnp.int32, (SUBLANES, fc), 0)

    @pl.when(j == 0)
    def _():
        tail_ref[...] = jnp.zeros_like(tail_ref)

    acc = h1
    for c in range(n_chunks):
        halves = []
        for part in range(2):
            idx = part * n_chunks + c
            u = _dot(hn, wup_ref[:, idx * fc:(idx + 1) * fc])
            prev_tail = tail_ref[idx]
            tail_ref[idx] = u[tm - SUBLANES:]
            y = (cb_ref[idx] + cw_ref[0, idx] * _shift_rows(u, prev_tail, 2, row8)
                 + cw_ref[1, idx] * _shift_rows(u, prev_tail, 1, row8)
                 + cw_ref[2, idx] * u)
            halves.append(y)
        gate, val = halves
        act = (gate * jax.nn.sigmoid(gate) * val).astype(BF16)
        acc = acc + _dot(act, wdown_ref[c])
    if final_norm:
        ms = jnp.mean(acc * acc, axis=-1, keepdims=True)
        acc = acc * lax.rsqrt(ms + EPS) * gfin_ref[...]
    out_ref[0] = acc


def _ffn_call(x, oa, ob, ga, gb, wba, wbb, wout, gffn, wup, cw, cb, wdown, gfin, tm,
              final_norm):
    b, s, d = x.shape
    n_chunks, fc, _ = wdown.shape
    n_parts = 2 * n_chunks
    assert wup.shape == (d, n_parts * fc) and fc % LANES == 0
    tok = lambda w: pl.BlockSpec((1, tm, w), lambda i, j: (i, j, 0))
    weights = (wba, wbb, wout, gffn, wup, cw, cb, wdown, gfin)
    return pl.pallas_call(
        functools.partial(_ffn_kernel, n_chunks=n_chunks, final_norm=final_norm),
        grid=(b, s // tm),
        in_specs=[tok(d), tok(oa.shape[-1]), tok(ob.shape[-1]), tok(d), tok(d)]
        + [_const_spec(w.shape) for w in weights],
        out_specs=tok(d),
        out_shape=jax.ShapeDtypeStruct((b, s, d), x.dtype),
        scratch_shapes=[pltpu.VMEM((n_parts, SUBLANES, fc), F32)],
        compiler_params=pltpu.CompilerParams(
            dimension_semantics=("arbitrary", "arbitrary"),
            vmem_limit_bytes=VMEM_LIMIT_BYTES),
        name="ffn",
    )(x, oa, ob, ga, gb, *weights)


def _prepare_ffn_weights(w_branch_a, w_branch_b, w_out, norm_ffn_g, w_up, conv_w,
                         conv_b, w_down, norm_final_g, n_chunks):
    d, two_f = w_up.shape
    fc = two_f // (2 * n_chunks)
    wup = w_up.astype(BF16)
    cw = conv_w.reshape(CONV_WIDTH, 2 * n_chunks, 1, fc)
    cb = conv_b.reshape(2 * n_chunks, 1, fc)
    wdown = w_down.reshape(n_chunks, fc, d).astype(BF16)
    return (w_branch_a.astype(BF16), w_branch_b.astype(BF16), w_out.astype(BF16),
            norm_ffn_g.reshape(1, -1), wup, cw, cb, wdown, norm_final_g.reshape(1, -1))


TM_PROJ = 512
TQ_ATTN = 256
TM_FFN = 512
FFN_CHUNKS = 2


def kernel(x, positions, norm_mix_g, w_in, idx_k_norm_g, q_a_norm_g, kv_a_norm_g, w_uq, w_uk, w_uv, w_branch_a, w_branch_b, w_out, norm_ffn_g, w_up, conv_w, conv_b, w_down, norm_final_g):
    b, s, _ = x.shape
    depth = w_in.shape[0]
    topk = min(INDEX_TOPK_MAX, s // 4)
    trig = _rope_trig(positions)
    expand = _rope_expansion()
    h = x
    for l in range(depth):
        pw = _prepare_proj_weights(norm_mix_g[l], w_in[l], idx_k_norm_g[l], q_a_norm_g[l],
                                   kv_a_norm_g[l], w_uq[l], w_uk[l], w_uv[l])
        aq, ak, av, iq, ik, iw, qb, kb, vb, ga, gb = _proj_call(h, trig, expand, *pw, tm=TM_PROJ)
        oa = _dsa_call(ik, iq, iw, aq, ak, av, positions, TQ_ATTN, topk)
        ob = _mla_call(qb, kb, vb, positions, TQ_ATTN)
        fw = _prepare_ffn_weights(w_branch_a[l], w_branch_b[l], w_out[l], norm_ffn_g[l],
                                  w_up[l], conv_w[l], conv_b[l], w_down[l], norm_final_g,
                                  FFN_CHUNKS)
        h = _ffn_call(h, oa, ob, ga, gb, *fw, tm=TM_FFN, final_norm=(l == depth - 1))
    return h
```

```python
import functools

import jax
import jax.numpy as jnp
import numpy as np
from jax import lax
from jax.experimental import pallas as pl
from jax.experimental.pallas import tpu as pltpu

ROPE_THETA = 500000.0
EPS = 1e-6
A_HEADS = 8
A_KV_HEADS = 2
A_GROUP = A_HEADS // A_KV_HEADS
A_HEAD_DIM = 64
A_ROT_DIM = 16
IDX_HEADS = 8
IDX_DIM = 64
INDEX_TOPK_MAX = 256
B_HEADS = 8
B_Q_RANK = 256
B_KV_RANK = 128
B_NOPE_DIM = 64
B_ROPE_DIM = 32
B_V_DIM = 64
CONV_WIDTH = 3

LANES = 128
SUBLANES = 8
VMEM_LIMIT_BYTES = 56 * 1024 * 1024

_O_AQ, _O_AK, _O_AV, _O_IQ, _O_CQ, _O_CKV, _O_GA, _O_GB, _O_TAIL = (
    0, 512, 640, 768, 1280, 1536, 1664, 2688, 3712)
_W_CAT = 3840
_T_KR, _T_IW = 64, 96

F32 = jnp.float32
BF16 = jnp.bfloat16
NEG_INF = float("-inf")
LOG2_E = 1.4426950408889634
F32_LOWEST = float(np.finfo(np.float32).min)


def _dot(a, b):
    return jnp.dot(a, b, preferred_element_type=F32)


def _dot_nt(a, b):
    return lax.dot_general(a, b, (((1,), (1,)), ((), ())), preferred_element_type=F32)


def _rope_lanes(x, cos_t, sin_t, take_up, half):
    up = pltpu.roll(x, LANES - half, axis=1)
    dn = pltpu.roll(x, half, axis=1)
    return x * cos_t + jnp.where(take_up, up, dn) * sin_t


def _expand_rope_tables(trig_t, expand_ref):
    t = jnp.concatenate([trig_t, jnp.zeros_like(trig_t)], axis=0).T
    a1 = t.astype(BF16)
    r1 = t - a1.astype(F32)
    a2 = r1.astype(BF16)
    a3 = (r1 - a2.astype(F32)).astype(BF16)
    tables = _dot(jnp.concatenate([a1, a2, a3], axis=1), expand_ref[...])
    return [tables[:, i * LANES:(i + 1) * LANES] for i in range(4)]


def _proj_kernel(x_ref, trig_ref, expand_ref, gmix_ref, w_ref, gik_ref,
                 gq_ref, gkv_ref, wuq_ref, wuk_ref, wuv_ref,
                 aq_ref, ak_ref, av_ref, iq_ref, ik_ref, iw_ref, qb_ref, kb_ref,
                 vb_ref, ga_ref, gb_ref):
    x = x_ref[0]
    tm = x.shape[0]
    ms = jnp.mean(x * x, axis=-1, keepdims=True)
    hn = (x * lax.rsqrt(ms + EPS) * gmix_ref[...]).astype(BF16)

    lane = lax.broadcasted_iota(jnp.int32, (tm, LANES), 1)
    ca, sa, cb, sb = _expand_rope_tables(trig_ref[0], expand_ref)
    up_a = (lane % A_HEAD_DIM) < (A_ROT_DIM // 2)
    up_b = lane < (B_NOPE_DIM + B_ROPE_DIM // 2)
    v_one = jnp.where(lane == A_HEAD_DIM, 1.0, 0.0).astype(F32)
    low_half = lane < A_HEAD_DIM

    def proj(off, width):
        return _dot(hn, w_ref[:, off:off + width])

    def store_heads(ref, tile128, first_head, scale):
        t = tile128 * scale if scale != 1.0 else tile128
        ref[0, first_head] = t[:, :A_HEAD_DIM].astype(ref.dtype)
        ref[0, first_head + 1] = t[:, A_HEAD_DIM:].astype(ref.dtype)

    for off, ref, scale in ((_O_AQ, aq_ref, A_HEAD_DIM ** -0.5 * LOG2_E),
                            (_O_IQ, iq_ref, IDX_DIM ** -0.5)):
        full = proj(off, 4 * LANES)
        for j in range(4):
            t = _rope_lanes(full[:, j * LANES:(j + 1) * LANES], ca, sa, up_a,
                            A_ROT_DIM // 2)
            store_heads(ref, t, 2 * j, scale)

    akv = proj(_O_AK, 2 * LANES)
    store_heads(ak_ref, _rope_lanes(akv[:, :LANES], ca, sa, up_a, A_ROT_DIM // 2),
                0, 1.0)
    v2 = akv[:, LANES:]
    av_ref[0, 0] = (jnp.where(low_half, v2, 0.0) + v_one).astype(BF16)
    v2r = pltpu.roll(v2, A_HEAD_DIM, axis=1)
    av_ref[0, 1] = (jnp.where(low_half, v2r, 0.0) + v_one).astype(BF16)

    tail = proj(_O_TAIL, LANES)
    ik_sq = jnp.where(low_half, tail * tail, 0.0)
    ik_ms = jnp.sum(ik_sq, axis=-1, keepdims=True) * (1.0 / IDX_DIM)
    ik_n = tail * lax.rsqrt(ik_ms + EPS) * gik_ref[...]
    ik_r = _rope_lanes(ik_n, ca, sa, up_a, A_ROT_DIM // 2)
    ik_ref[0] = ik_r[:, :IDX_DIM].astype(BF16)
    iw_ref[0] = tail * (IDX_HEADS ** -0.5)
    kr = _rope_lanes(tail, cb, sb, up_b, B_ROPE_DIM // 2)
    kr = jnp.where((lane >= B_NOPE_DIM) & (lane < B_NOPE_DIM + B_ROPE_DIM), kr, 0.0)

    cq = proj(_O_CQ, B_Q_RANK)
    cq_n = (cq * lax.rsqrt(jnp.mean(cq * cq, axis=-1, keepdims=True) + EPS)
            * gq_ref[...]).astype(BF16)
    ckv = proj(_O_CKV, B_KV_RANK)
    ckv_n = (ckv * lax.rsqrt(jnp.mean(ckv * ckv, axis=-1, keepdims=True) + EPS)
             * gkv_ref[...]).astype(BF16)
    b_scale = (B_NOPE_DIM + B_ROPE_DIM) ** -0.5 * LOG2_E
    for h0 in range(0, B_HEADS, 2):
        cols = slice(h0 * LANES, (h0 + 2) * LANES)
        q2 = _dot(cq_n, wuq_ref[:, cols])
        k2 = _dot(ckv_n, wuk_ref[:, cols])
        v2 = _dot(ckv_n, wuv_ref[:, cols])
        for i in range(2):
            lanes = slice(i * LANES, (i + 1) * LANES)
            q = _rope_lanes(q2[:, lanes], cb, sb, up_b, B_ROPE_DIM // 2) * b_scale
            qb_ref[0, h0 + i] = q.astype(BF16)
            kb_ref[0, h0 + i] = (k2[:, lanes] + kr).astype(BF16)
            vb_ref[0, h0 + i] = (v2[:, lanes] + v_one).astype(BF16)

    ga_ref[0] = jax.nn.sigmoid(proj(_O_GA, 1024)).astype(BF16)
    gb_ref[0] = jax.nn.sigmoid(proj(_O_GB, 1024)).astype(BF16)


_R_CA, _R_SA, _R_CB, _R_SB, _R_ONE = 0, 8, 16, 32, 48
_TRIG_ROWS = 64


def _rope_trig(positions):
    pos = positions.astype(F32)[:, None, :]

    def cs(rot):
        inv = ROPE_THETA ** (-jnp.arange(0, rot, 2, dtype=F32) / rot)
        ang = pos * inv[None, :, None]
        return jnp.cos(ang), jnp.sin(ang)

    b, s = positions.shape
    ca, sa = cs(A_ROT_DIM)
    cb, sb = cs(B_ROPE_DIM)
    pad = jnp.zeros((b, _TRIG_ROWS - _R_ONE - 1, s), F32)
    return jnp.concatenate([ca, sa, cb, sb, jnp.ones((b, 1, s), F32), pad], axis=1)


def _rope_expansion():
    e = np.zeros((LANES, 4 * LANES), np.float32)
    half_a, half_b = A_ROT_DIM // 2, B_ROPE_DIM // 2
    for j in range(LANES):
        d = j % A_HEAD_DIM
        if d < A_ROT_DIM:
            e[_R_CA + d % half_a, j] = 1.0
            e[_R_SA + d % half_a, LANES + j] = -1.0 if d < half_a else 1.0
        else:
            e[_R_ONE, j] = 1.0
        d = j - B_NOPE_DIM
        if 0 <= d < B_ROPE_DIM:
            e[_R_CB + d % half_b, 2 * LANES + j] = 1.0
            e[_R_SB + d % half_b, 3 * LANES + j] = -1.0 if d < half_b else 1.0
        else:
            e[_R_ONE, 2 * LANES + j] = 1.0
    return jnp.asarray(np.concatenate([e, e, e], axis=0), BF16)


def _regroup_w_in(w_in):
    splits = [512, 128, 128, 512, 64, 8, 256, 128, 32, 1024, 1024]
    offs = np.concatenate([[0], np.cumsum(splits)])
    seg = [w_in[:, offs[i]:offs[i + 1]] for i in range(len(splits))]
    a_q, a_k, a_v, i_q, i_k, i_w, b_cq, b_ckv, b_kr, g_a, g_b = seg
    pad = jnp.zeros((w_in.shape[0], LANES - 64 - 32 - 8), w_in.dtype)
    return jnp.concatenate(
        [a_q, a_k, a_v, i_q, b_cq, b_ckv, g_a, g_b, i_k, b_kr, i_w, pad], axis=1)


def _pad_heads(w, width):
    r, h, d = w.shape
    return jnp.pad(w, ((0, 0), (0, 0), (0, width - d))).reshape(r, h * width)


def _const_spec(shape):
    nd = len(shape)
    return pl.BlockSpec(shape, lambda *_: (0,) * nd)


def _proj_call(x, trig, expand, gmix, wcat, gik, gq, gkv, wuq, wuk, wuv, tm):
    b, s, d = x.shape
    grid = (b, s // tm)
    tok = lambda w: pl.BlockSpec((1, tm, w), lambda i, j: (i, j, 0))
    head = lambda h, w: pl.BlockSpec((1, h, tm, w), lambda i, j: (i, 0, j, 0))
    in_specs = [tok(d), pl.BlockSpec((1, _TRIG_ROWS, tm), lambda i, j: (i, 0, j)),
                _const_spec(expand.shape),
        _const_spec(gmix.shape), _const_spec(wcat.shape), _const_spec(gik.shape),
        _const_spec(gq.shape), _const_spec(gkv.shape), _const_spec(wuq.shape),
        _const_spec(wuk.shape), _const_spec(wuv.shape)]
    sds = jax.ShapeDtypeStruct
    out_shape = [
        sds((b, A_HEADS, s, A_HEAD_DIM), BF16),
        sds((b, A_KV_HEADS, s, A_HEAD_DIM), BF16),
        sds((b, A_KV_HEADS, s, LANES), BF16),
        sds((b, IDX_HEADS, s, IDX_DIM), BF16),
        sds((b, s, IDX_DIM), BF16),
        sds((b, s, LANES), F32),
        sds((b, B_HEADS, s, LANES), BF16),
        sds((b, B_HEADS, s, LANES), BF16),
        sds((b, B_HEADS, s, LANES), BF16),
        sds((b, s, d), BF16),
        sds((b, s, d), BF16),
    ]
    out_specs = [
        head(A_HEADS, A_HEAD_DIM), head(A_KV_HEADS, A_HEAD_DIM),
        head(A_KV_HEADS, LANES), head(IDX_HEADS, IDX_DIM), tok(IDX_DIM),
        tok(LANES), head(B_HEADS, LANES), head(B_HEADS, LANES),
        head(B_HEADS, LANES), tok(d), tok(d)]
    return pl.pallas_call(
        _proj_kernel,
        grid=grid,
        in_specs=in_specs,
        out_specs=out_specs,
        out_shape=out_shape,
        compiler_params=pltpu.CompilerParams(
            dimension_semantics=("arbitrary", "arbitrary"),
            vmem_limit_bytes=VMEM_LIMIT_BYTES),
        name="proj",
    )(x, trig, expand, gmix, wcat, gik, gq, gkv, wuq, wuk, wuv)


def _prepare_proj_weights(norm_mix_g, w_in, idx_k_norm_g, q_a_norm_g, kv_a_norm_g,
                          w_uq, w_uk, w_uv):
    gmix = norm_mix_g.reshape(1, -1)
    wcat = _regroup_w_in(w_in).astype(BF16)
    gik = jnp.pad(idx_k_norm_g, (0, LANES - IDX_DIM)).reshape(1, LANES)
    gq = q_a_norm_g.reshape(1, -1)
    gkv = kv_a_norm_g.reshape(1, -1)
    wuq = _pad_heads(w_uq, LANES).astype(BF16)
    wuk = _pad_heads(w_uk, LANES).astype(BF16)
    wuv = _pad_heads(w_uv, LANES).astype(BF16)
    return gmix, wcat, gik, gq, gkv, wuq, wuk, wuv


_INT_MIN = -2 ** 31
_ORD_NEG_INF = 0x007FFFFF
_ORD_LOWEST_FINITE = 0x00800000


def _ordinal_to_float(u):
    o = u ^ jnp.int32(_INT_MIN)
    bits = o ^ ((o >> 31) & jnp.int32(0x7FFFFFFF))
    return lax.bitcast_convert_type(bits, F32)


def _softmax_update(s, v, m_old, acc, rows_may_be_empty=True):
    m_new = jnp.maximum(m_old, jnp.max(s, axis=-1, keepdims=True))
    if rows_may_be_empty:
        m_new = jnp.maximum(m_new, F32_LOWEST)
    p = jnp.concatenate(
        [jnp.exp2(s[:, j * LANES:(j + 1) * LANES] - m_new)
         for j in range(s.shape[1] // LANES)], axis=1)
    alpha = jnp.exp2(m_old - m_new)
    return m_new, alpha * acc + _dot(p.astype(BF16), v)


def _attend_chunks(n_heads, m_ref, acc_ref, chunk_fns):
    for h in range(n_heads):
        m, acc = m_ref[h], acc_ref[h]
        for fn in chunk_fns:
            m, acc = _softmax_update(*fn(h), m, acc)
        m_ref[h] = m
        acc_ref[h] = acc


def _for_chunk_groups(n, body):
    def group(i, carry):
        body(tuple(4 * i + r for r in range(4)))
        return carry

    lax.fori_loop(0, lax.shift_right_logical(n, 2), group, 0)
    done = n & ~3

    @pl.when((n & 2) != 0)
    def _():
        body((done, done + 1))

    @pl.when((n & 1) != 0)
    def _():
        body((n - 1,))


def _init_softmax_state(m_ref, acc_ref):
    m_ref[...] = jnp.full(m_ref.shape, NEG_INF, F32)
    acc_ref[...] = jnp.zeros(acc_ref.shape, F32)


def _finish_heads(acc_even, acc_odd, lane):
    o0 = acc_even * (1.0 / acc_even[:, B_V_DIM:B_V_DIM + 1])
    o1 = acc_odd * (1.0 / acc_odd[:, B_V_DIM:B_V_DIM + 1])
    return jnp.where(lane < B_V_DIM, o0, pltpu.roll(o1, B_V_DIM, axis=1))


def _store_heads(o_ref, acc_ref, n_heads):
    lane = lax.broadcasted_iota(jnp.int32, acc_ref.shape[1:], 1)
    for pair in range(n_heads // 2):
        o_ref[0, :, pair * LANES:(pair + 1) * LANES] = _finish_heads(
            acc_ref[2 * pair], acc_ref[2 * pair + 1], lane).astype(o_ref.dtype)


_COARSE_PASSES = 16
_FINE_PASSES = 17
_PASSES_PER_CHUNK = 4


def _attn_kernel(ik_ref, ak_ref, av_ref, kb_ref, vb_ref, posk_col_ref, posk_row_ref,
                 iq0_ref, iq1_ref, iw0_ref, iw1_ref, aq0_ref, aq1_ref, qb0_ref, qb1_ref,
                 pqr0_ref, pqr1_ref, pqc0_ref, pqc1_ref,
                 oa0_ref, oa1_ref, ob0_ref, ob1_ref,
                 isc_ref, xb_ref, bias_ref, wt_ref, ma_ref, acca_ref, mb_ref, accb_ref,
                 *, topk, nq):
    pi = pl.program_id(1)
    tq = aq0_ref.shape[2]
    half = tq // 2
    nk0 = pi + 1
    n_virtual = nq + 1
    blocks = (
        dict(idx=0, nk=nk0, off=0, iq=iq0_ref, iw=iw0_ref, aq=aq0_ref, pqr=pqr0_ref,
             oa=oa0_ref, ob=ob0_ref),
        dict(idx=1, nk=nq - pi, off=nk0, iq=iq1_ref, iw=iw1_ref, aq=aq1_ref, pqr=pqr1_ref,
             oa=oa1_ref, ob=ob1_ref),
    )

    for blk in blocks:
        wt_ref[blk["idx"]] = blk["iw"][0].T[_T_IW:_T_IW + IDX_HEADS, :]
        posq = blk["pqr"][0]

        def idx_chunk(c, carry, blk=blk, posq=posq):
            v = blk["off"] + c
            for sub in range(2):
                k0 = pl.multiple_of(c * tq + sub * half, half)
                ikc = ik_ref[0, pl.ds(k0, half), :]
                acc = jnp.zeros((half, tq), F32)
                for h in range(IDX_HEADS):
                    d = _dot_nt(ikc, blk["iq"][0, h])
                    acc = acc + wt_ref[blk["idx"], h:h + 1, :] * jnp.maximum(d, 0.0)
                causal = posk_col_ref[0, pl.ds(k0, half), :] <= posq
                score = jnp.where(causal, acc, NEG_INF)
                isc_ref[v, sub * half:(sub + 1) * half, :] = score
                xb_ref[v, sub * half:(sub + 1) * half, :] = score.astype(BF16)
            return carry

        lax.fori_loop(0, blk["nk"], idx_chunk, 0)

    _init_softmax_state(mb_ref, accb_ref)

    def latent_chunk(v):
        if isinstance(v, int):
            assert v in (0, n_virtual - 1)
            in1 = v != 0
            c = (nq - 1 - pi) if in1 else (nk0 - 1)
            b = int(in1)
            q_ref, pqc_ref = (qb1_ref, pqc1_ref) if in1 else (qb0_ref, pqc0_ref)
        else:
            in1 = v >= nk0
            c = jnp.where(in1, v - nk0, v - 1)
            b = in1.astype(jnp.int32)
        k0 = pl.multiple_of(c * tq, tq)
        if isinstance(v, int):
            causal = posk_row_ref[0, :, pl.ds(k0, tq)] <= pqc_ref[0]
        for h in range(B_HEADS):
            if isinstance(v, int):
                q = q_ref[0, h]
            else:
                q = jnp.where(in1, qb1_ref[0, h], qb0_ref[0, h])
            s = _dot_nt(q, kb_ref[0, h, pl.ds(k0, tq), :])
            if isinstance(v, int):
                s = jnp.where(causal, s, NEG_INF)
            m, acc = _softmax_update(s, vb_ref[0, h, pl.ds(k0, tq), :],
                                     mb_ref[b, h], accb_ref[b, h], rows_may_be_empty=False)
            mb_ref[b, h] = m
            accb_ref[b, h] = acc

    packed = 2 * SUBLANES
    one_b, zero_b = jnp.asarray(1, BF16), jnp.asarray(0, BF16)

    def count_ge(thr0, thr1, coarse):
        rows = packed if coarse else SUBLANES
        dtype = BF16 if coarse else F32
        acc0 = jnp.zeros((rows, tq), dtype)
        acc1 = jnp.zeros((rows, tq), dtype)
        zero = jnp.zeros((rows, tq), dtype)
        for v in range(n_virtual):
            in1 = v >= nk0
            thr = jnp.where(in1, thr1, thr0)
            if coarse:
                hit = jnp.where(xb_ref[v] >= thr, one_b, zero_b)
            else:
                hit = jnp.where(isc_ref[v] >= thr, 1.0, 0.0)
            part = hit[:rows]
            for r in range(1, tq // rows):
                part = part + hit[r * rows:(r + 1) * rows]
            acc0 = acc0 + jnp.where(in1, zero, part)
            acc1 = acc1 + jnp.where(in1, part, zero)
        return (jnp.sum(acc0.astype(F32), axis=0, keepdims=True),
                jnp.sum(acc1.astype(F32), axis=0, keepdims=True))

    def coarse_ordinal(j):
        low = jnp.where(j < 0x8000, jnp.int32(0xFFFF), jnp.int32(0))
        return lax.shift_left(j, 16) | low

    def coarse_pass(it, js):
        cands = [j | lax.shift_left(jnp.int32(1), 15 - it) for j in js]
        thrs = [_ordinal_to_float(coarse_ordinal(c)).astype(BF16) for c in cands]
        cnts = count_ge(*thrs, coarse=True)
        return tuple(jnp.where(n >= topk, c, j) for n, c, j in zip(cnts, cands, js))

    def fine_pass(it, ds, bases):
        cands = [d | lax.shift_left(jnp.int32(1), 16 - it) for d in ds]
        thrs = [_ordinal_to_float(b + c) for b, c in zip(bases, cands)]
        cnts = count_ge(*thrs, coarse=False)
        return tuple(jnp.where(n >= topk, c, d) for n, c, d in zip(cnts, cands, ds))

    zeros = (jnp.zeros((1, tq), jnp.int32),) * 2
    first_fine = _COARSE_PASSES // _PASSES_PER_CHUNK

    def coarse_body(v, js):
        latent_chunk(v)
        for r in range(_PASSES_PER_CHUNK):
            js = coarse_pass(_PASSES_PER_CHUNK * v + r, js)
        return js

    js = lax.fori_loop(1, first_fine, coarse_body, coarse_body(0, zeros))
    js = [jnp.maximum(j, _ORD_NEG_INF >> 16) for j in js]
    bases = [coarse_ordinal(j) - jnp.int32(1 << 16) for j in js]

    def fine_body(v, ds):
        latent_chunk(v)
        for r in range(_PASSES_PER_CHUNK):
            ds = fine_pass(_PASSES_PER_CHUNK * (v - first_fine) + r, ds, bases)
        return ds

    ds = lax.fori_loop(first_fine, n_virtual - 1, fine_body, zeros)
    latent_chunk(n_virtual - 1)
    ds = fine_pass(_FINE_PASSES - 1, ds, bases)

    for blk, base, d in zip(blocks, bases, ds):
        t_u = base + d
        short = (t_u >= 0) & (t_u < _ORD_LOWEST_FINITE)
        thr = _ordinal_to_float(jnp.where(short, jnp.int32(_ORD_LOWEST_FINITE), t_u))
        off, nk = blk["off"], blk["nk"]

        def write_bias(c, n_ge, thr=thr, off=off):
            sel = isc_ref[off + c] >= thr
            bias_ref[off + c] = jnp.where(sel, 0.0, NEG_INF).T
            return n_ge + jnp.sum(jnp.where(sel, 1.0, 0.0), axis=0, keepdims=True)

        n_ge = lax.fori_loop(0, nk, write_bias, jnp.zeros((1, tq), F32))
        tied = jnp.logical_and(n_ge > topk, jnp.logical_not(short))

        @pl.when(jnp.max(jnp.where(tied, 1.0, 0.0)) > 0.0)
        def _(thr=thr, off=off, nk=nk):
            def count_gt(c, acc):
                hit = jnp.where(isc_ref[off + c] > thr, 1.0, 0.0)
                return acc + jnp.sum(hit, axis=0, keepdims=True)
            n_gt = lax.fori_loop(0, nk, count_gt, jnp.zeros((1, tq), F32))
            need = topk - n_gt
            row = lax.broadcasted_iota(jnp.int32, (tq, tq), 0)
            col = lax.broadcasted_iota(jnp.int32, (tq, tq), 1)
            below = jnp.where(col < row, 1.0, 0.0).astype(BF16)

            def fix(c, seen):
                x = isc_ref[off + c]
                eq = jnp.where(x == thr, 1.0, 0.0)
                rank = _dot(below, eq.astype(BF16)) + seen
                sel = (x > thr) | ((x == thr) & (rank < need))
                bias_ref[off + c] = jnp.where(sel, 0.0, NEG_INF).T
                return seen + jnp.sum(eq, axis=0, keepdims=True)

            lax.fori_loop(0, nk, fix, jnp.zeros((1, tq), F32))

    _init_softmax_state(ma_ref, acca_ref)
    for blk in blocks:
        def attend(chunks, blk=blk):
            def chunk_fn(c):
                k0 = pl.multiple_of(c * tq, tq)
                bias = bias_ref[blk["off"] + c]

                def fn(h):
                    g = h // A_GROUP
                    s = _dot_nt(blk["aq"][0, h], ak_ref[0, g, pl.ds(k0, tq), :]) + bias
                    return s, av_ref[0, g, pl.ds(k0, tq), :]
                return fn

            _attend_chunks(A_HEADS, ma_ref.at[blk["idx"]], acca_ref.at[blk["idx"]],
                           [chunk_fn(c) for c in chunks])

        _for_chunk_groups(blk["nk"], attend)
        _store_heads(blk["oa"], acca_ref.at[blk["idx"]], A_HEADS)
        _store_heads(blk["ob"], accb_ref.at[blk["idx"]], B_HEADS)


def _attn_call(ik, iq, iw, aq, ak, av, qb, kb, vb, positions, tq, topk):
    b, heads, s, _ = aq.shape
    nq = s // tq
    n_pairs = nq // 2
    assert s % tq == 0 and nq % 2 == 0
    assert nq + 1 == (_COARSE_PASSES + _FINE_PASSES - 1) // _PASSES_PER_CHUNK + 1
    assert (nq + 1) * tq // (2 * SUBLANES) <= 256
    pos_col = positions[:, :, None]
    pos_row = positions[:, None, :]

    def per_seq(*shape):
        nd = len(shape)
        return pl.BlockSpec((1,) + shape, lambda i, j: (i,) + (0,) * nd)

    blk_index = (lambda j: j, lambda j: nq - 1 - j)

    def qhead(h, w, which):
        return pl.BlockSpec((1, h, tq, w), lambda i, j: (i, 0, blk_index[which](j), 0))

    def qrows(w, which):
        return pl.BlockSpec((1, tq, w), lambda i, j: (i, blk_index[which](j), 0))

    def qlanes(which):
        return pl.BlockSpec((1, 1, tq), lambda i, j: (i, 0, blk_index[which](j)))

    in_specs = [
        per_seq(s, IDX_DIM), per_seq(A_KV_HEADS, s, A_HEAD_DIM), per_seq(A_KV_HEADS, s, LANES),
        per_seq(B_HEADS, s, LANES), per_seq(B_HEADS, s, LANES),
        per_seq(s, 1), per_seq(1, s),
        qhead(IDX_HEADS, IDX_DIM, 0), qhead(IDX_HEADS, IDX_DIM, 1),
        qrows(LANES, 0), qrows(LANES, 1),
        qhead(A_HEADS, A_HEAD_DIM, 0), qhead(A_HEADS, A_HEAD_DIM, 1),
        qhead(B_HEADS, LANES, 0), qhead(B_HEADS, LANES, 1),
        qlanes(0), qlanes(1), qrows(1, 0), qrows(1, 1),
    ]
    width = heads * A_HEAD_DIM
    out_sds = jax.ShapeDtypeStruct((b, s // 2, width), BF16)
    out_lo = pl.BlockSpec((1, tq, width), lambda i, j: (i, j, 0))
    out_hi = pl.BlockSpec((1, tq, width), lambda i, j: (i, n_pairs - 1 - j, 0))
    state = pltpu.VMEM((2, heads, tq, LANES), F32)
    return pl.pallas_call(
        functools.partial(_attn_kernel, topk=topk, nq=nq),
        grid=(b, n_pairs),
        in_specs=in_specs,
        out_specs=[out_lo, out_hi, out_lo, out_hi],
        out_shape=[out_sds] * 4,
        scratch_shapes=[
            pltpu.VMEM((nq + 1, tq, tq), F32),
            pltpu.VMEM((nq + 1, tq, tq), BF16),
            pltpu.VMEM((nq + 1, tq, tq), F32),
            pltpu.VMEM((2, IDX_HEADS, tq), F32),
            state, state, state, state,
        ],
        compiler_params=pltpu.CompilerParams(
            dimension_semantics=("arbitrary", "arbitrary"),
            vmem_limit_bytes=VMEM_LIMIT_BYTES),
        name="attn",
    )(ik, ak, av, kb, vb, pos_col, pos_row, iq, iq, iw, iw, aq, aq, qb, qb,
      pos_row, pos_row, pos_col, pos_col)


def _shift_rows(u, prev_tail, shift, row8):
    rolled = pltpu.roll(u, shift, axis=0)
    head = jnp.where(row8 < shift, pltpu.roll(prev_tail, shift, axis=0),
                     rolled[:SUBLANES])
    return jnp.concatenate([head, rolled[SUBLANES:]], axis=0)


def _ffn_kernel(x_ref, oa_lo_ref, oa_hi_ref, ob_lo_ref, ob_hi_ref, ga_ref, gb_ref,
                wba_ref, wbb_ref, wout_ref, gffn_ref, wup_ref, cw_ref, cb_ref, wdown_ref,
                gfin_ref, out_ref, tail_ref, *, n_chunks, final_norm):
    j = pl.program_id(1)
    x = x_ref[0]
    tm = x.shape[0]
    first_half = j < pl.num_programs(1) // 2
    oa = jnp.where(first_half, oa_lo_ref[0], oa_hi_ref[0])
    ob = jnp.where(first_half, ob_lo_ref[0], ob_hi_ref[0])
    ya = _dot(oa, wba_ref[...])
    yb = _dot(ob, wbb_ref[...])
    merged = ga_ref[0].astype(F32) * ya + gb_ref[0].astype(F32) * yb
    h1 = x + _dot(merged.astype(BF16), wout_ref[...])
    ms = jnp.mean(h1 * h1, axis=-1, keepdims=True)
    hn = (h1 * lax.rsqrt(ms + EPS) * gffn_ref[...]).astype(BF16)

    fc = wdown_ref.shape[1]
    row8 = lax.broadcasted_iota(jnp.int32, (SUBLANES, fc), 0)

    @pl.when(j == 0)
    def _():
        tail_ref[...] = jnp.zeros_like(tail_ref)

    acc = h1
    for c in range(n_chunks):
        halves = []
        for part in range(2):
            idx = part * n_chunks + c
            u = _dot(hn, wup_ref[:, idx * fc:(idx + 1) * fc])
            prev_tail = tail_ref[idx]
            tail_ref[idx] = u[tm - SUBLANES:]
            y = (cb_ref[idx] + cw_ref[0, idx] * _shift_rows(u, prev_tail, 2, row8)
                 + cw_ref[1, idx] * _shift_rows(u, prev_tail, 1, row8)
                 + cw_ref[2, idx] * u)
            halves.append(y)
        gate, val = halves
        act = (gate * jax.nn.sigmoid(gate) * val).astype(BF16)
        acc = acc + _dot(act, wdown_ref[c])
    if final_norm:
        ms = jnp.mean(acc * acc, axis=-1, keepdims=True)
        acc = acc * lax.rsqrt(ms + EPS) * gfin_ref[...]
    out_ref[0] = acc


def _ffn_call(x, oa_lo, oa_hi, ob_lo, ob_hi, ga, gb, wba, wbb, wout, gffn, wup, cw, cb,
              wdown, gfin, tm, final_norm):
    b, s, d = x.shape
    n_chunks, fc, _ = wdown.shape
    n_parts = 2 * n_chunks
    assert wup.shape == (d, n_parts * fc) and fc % LANES == 0
    n_tiles = s // tm
    assert s % tm == 0 and n_tiles % 2 == 0
    n_half = n_tiles // 2
    tok = lambda w: pl.BlockSpec((1, tm, w), lambda i, j: (i, j, 0))
    lo = lambda w: pl.BlockSpec((1, tm, w), lambda i, j: (i, jnp.minimum(j, n_half - 1), 0))
    hi = lambda w: pl.BlockSpec((1, tm, w), lambda i, j: (i, jnp.maximum(j - n_half, 0), 0))
    wa, wb = oa_lo.shape[-1], ob_lo.shape[-1]
    weights = (wba, wbb, wout, gffn, wup, cw, cb, wdown, gfin)
    return pl.pallas_call(
        functools.partial(_ffn_kernel, n_chunks=n_chunks, final_norm=final_norm),
        grid=(b, n_tiles),
        in_specs=[tok(d), lo(wa), hi(wa), lo(wb), hi(wb), tok(d), tok(d)]
        + [_const_spec(w.shape) for w in weights],
        out_specs=tok(d),
        out_shape=jax.ShapeDtypeStruct((b, s, d), x.dtype),
        scratch_shapes=[pltpu.VMEM((n_parts, SUBLANES, fc), F32)],
        compiler_params=pltpu.CompilerParams(
            dimension_semantics=("arbitrary", "arbitrary"),
            vmem_limit_bytes=VMEM_LIMIT_BYTES),
        name="ffn",
    )(x, oa_lo, oa_hi, ob_lo, ob_hi, ga, gb, *weights)


def _prepare_ffn_weights(w_branch_a, w_branch_b, w_out, norm_ffn_g, w_up, conv_w,
                         conv_b, w_down, norm_final_g, n_chunks):
    d, two_f = w_up.shape
    fc = two_f // (2 * n_chunks)
    wup = w_up.astype(BF16)
    cw = conv_w.reshape(CONV_WIDTH, 2 * n_chunks, 1, fc)
    cb = conv_b.reshape(2 * n_chunks, 1, fc)
    wdown = w_down.reshape(n_chunks, fc, d).astype(BF16)
    return (w_branch_a.astype(BF16), w_branch_b.astype(BF16), w_out.astype(BF16),
            norm_ffn_g.reshape(1, -1), wup, cw, cb, wdown, norm_final_g.reshape(1, -1))


TM_PROJ = 512
TQ_ATTN = 256
TM_FFN = 512
FFN_CHUNKS = 2


def kernel(x, positions, norm_mix_g, w_in, idx_k_norm_g, q_a_norm_g, kv_a_norm_g, w_uq, w_uk, w_uv, w_branch_a, w_branch_b, w_out, norm_ffn_g, w_up, conv_w, conv_b, w_down, norm_final_g):
    b, s, _ = x.shape
    depth = w_in.shape[0]
    topk = min(INDEX_TOPK_MAX, s // 4)
    trig = _rope_trig(positions)
    expand = _rope_expansion()
    h = x
    for l in range(depth):
        pw = _prepare_proj_weights(norm_mix_g[l], w_in[l], idx_k_norm_g[l], q_a_norm_g[l],
                                   kv_a_norm_g[l], w_uq[l], w_uk[l], w_uv[l])
        aq, ak, av, iq, ik, iw, qb, kb, vb, ga, gb = _proj_call(h, trig, expand, *pw, tm=TM_PROJ)
        mixed = _attn_call(ik, iq, iw, aq, ak, av, qb, kb, vb, positions, TQ_ATTN, topk)
        fw = _prepare_ffn_weights(w_branch_a[l], w_branch_b[l], w_out[l], norm_ffn_g[l],
                                  w_up[l], conv_w[l], conv_b[l], w_down[l], norm_final_g,
                                  FFN_CHUNKS)
        h = _ffn_call(h, *mixed, ga, gb, *fw, tm=TM_FFN, final_norm=(l == depth - 1))
    return h
```

```python
import functools

import jax
import jax.numpy as jnp
import numpy as np
from jax import lax
from jax.experimental import pallas as pl
from jax.experimental.pallas import tpu as pltpu

ROPE_THETA = 500000.0
EPS = 1e-6
A_HEADS = 8
A_KV_HEADS = 2
A_GROUP = A_HEADS // A_KV_HEADS
A_HEAD_DIM = 64
A_ROT_DIM = 16
IDX_HEADS = 8
IDX_DIM = 64
INDEX_TOPK_MAX = 256
B_HEADS = 8
B_Q_RANK = 256
B_KV_RANK = 128
B_NOPE_DIM = 64
B_ROPE_DIM = 32
B_V_DIM = 64
CONV_WIDTH = 3

LANES = 128
SUBLANES = 8
VMEM_LIMIT_BYTES = 56 * 1024 * 1024

_O_AQ, _O_AK, _O_AV, _O_IQ, _O_CQ, _O_CKV, _O_GA, _O_GB, _O_TAIL = (
    0, 512, 640, 768, 1280, 1536, 1664, 2688, 3712)
_W_CAT = 3840
_T_KR, _T_IW = 64, 96

F32 = jnp.float32
BF16 = jnp.bfloat16
NEG_INF = float("-inf")
LOG2_E = 1.4426950408889634
F32_LOWEST = float(np.finfo(np.float32).min)


def _dot(a, b):
    return jnp.dot(a, b, preferred_element_type=F32)


def _dot_nt(a, b):
    return lax.dot_general(a, b, (((1,), (1,)), ((), ())), preferred_element_type=F32)


def _rope_lanes(x, cos_t, sin_t, take_up, half):
    up = pltpu.roll(x, LANES - half, axis=1)
    dn = pltpu.roll(x, half, axis=1)
    return x * cos_t + jnp.where(take_up, up, dn) * sin_t


def _expand_rope_tables(trig_t, expand_ref):
    t = jnp.concatenate([trig_t, jnp.zeros_like(trig_t)], axis=0).T
    a1 = t.astype(BF16)
    r1 = t - a1.astype(F32)
    a2 = r1.astype(BF16)
    a3 = (r1 - a2.astype(F32)).astype(BF16)
    tables = _dot(jnp.concatenate([a1, a2, a3], axis=1), expand_ref[...])
    return [tables[:, i * LANES:(i + 1) * LANES] for i in range(4)]


def _proj_kernel(x_ref, trig_ref, expand_ref, gmix_ref, w_ref, gik_ref,
                 gq_ref, gkv_ref, wuq_ref, wuk_ref, wuv_ref,
                 aq_ref, ak_ref, av_ref, iq_ref, ik_ref, iw_ref, qb_ref, kb_ref,
                 vb_ref, ga_ref, gb_ref):
    x = x_ref[0]
    tm = x.shape[0]
    ms = jnp.mean(x * x, axis=-1, keepdims=True)
    hn = (x * lax.rsqrt(ms + EPS) * gmix_ref[...]).astype(BF16)

    lane = lax.broadcasted_iota(jnp.int32, (tm, LANES), 1)
    ca, sa, cb, sb = _expand_rope_tables(trig_ref[0], expand_ref)
    up_a = (lane % A_HEAD_DIM) < (A_ROT_DIM // 2)
    up_b = lane < (B_NOPE_DIM + B_ROPE_DIM // 2)
    v_one = jnp.where(lane == A_HEAD_DIM, 1.0, 0.0).astype(F32)
    low_half = lane < A_HEAD_DIM

    def proj(off, width):
        return _dot(hn, w_ref[:, off:off + width])

    def store_heads(ref, tile128, first_head, scale):
        t = tile128 * scale if scale != 1.0 else tile128
        ref[0, first_head] = t[:, :A_HEAD_DIM].astype(ref.dtype)
        ref[0, first_head + 1] = t[:, A_HEAD_DIM:].astype(ref.dtype)

    for off, ref, scale in ((_O_AQ, aq_ref, A_HEAD_DIM ** -0.5 * LOG2_E),
                            (_O_IQ, iq_ref, IDX_DIM ** -0.5)):
        full = proj(off, 4 * LANES)
        for j in range(4):
            t = _rope_lanes(full[:, j * LANES:(j + 1) * LANES], ca, sa, up_a,
                            A_ROT_DIM // 2)
            store_heads(ref, t, 2 * j, scale)

    akv = proj(_O_AK, 2 * LANES)
    store_heads(ak_ref, _rope_lanes(akv[:, :LANES], ca, sa, up_a, A_ROT_DIM // 2),
                0, 1.0)
    v2 = akv[:, LANES:]
    av_ref[0, 0] = (jnp.where(low_half, v2, 0.0) + v_one).astype(BF16)
    v2r = pltpu.roll(v2, A_HEAD_DIM, axis=1)
    av_ref[0, 1] = (jnp.where(low_half, v2r, 0.0) + v_one).astype(BF16)

    tail = proj(_O_TAIL, LANES)
    ik_sq = jnp.where(low_half, tail * tail, 0.0)
    ik_ms = jnp.sum(ik_sq, axis=-1, keepdims=True) * (1.0 / IDX_DIM)
    ik_n = tail * lax.rsqrt(ik_ms + EPS) * gik_ref[...]
    ik_r = _rope_lanes(ik_n, ca, sa, up_a, A_ROT_DIM // 2)
    ik_ref[0] = ik_r[:, :IDX_DIM].astype(BF16)
    iw_ref[0] = tail * (IDX_HEADS ** -0.5)
    kr = _rope_lanes(tail, cb, sb, up_b, B_ROPE_DIM // 2)
    kr = jnp.where((lane >= B_NOPE_DIM) & (lane < B_NOPE_DIM + B_ROPE_DIM), kr, 0.0)

    cq = proj(_O_CQ, B_Q_RANK)
    cq_n = (cq * lax.rsqrt(jnp.mean(cq * cq, axis=-1, keepdims=True) + EPS)
            * gq_ref[...]).astype(BF16)
    ckv = proj(_O_CKV, B_KV_RANK)
    ckv_n = (ckv * lax.rsqrt(jnp.mean(ckv * ckv, axis=-1, keepdims=True) + EPS)
             * gkv_ref[...]).astype(BF16)
    b_scale = (B_NOPE_DIM + B_ROPE_DIM) ** -0.5 * LOG2_E
    for h0 in range(0, B_HEADS, 2):
        cols = slice(h0 * LANES, (h0 + 2) * LANES)
        q2 = _dot(cq_n, wuq_ref[:, cols])
        k2 = _dot(ckv_n, wuk_ref[:, cols])
        v2 = _dot(ckv_n, wuv_ref[:, cols])
        for i in range(2):
            lanes = slice(i * LANES, (i + 1) * LANES)
            q = _rope_lanes(q2[:, lanes], cb, sb, up_b, B_ROPE_DIM // 2) * b_scale
            qb_ref[0, h0 + i] = q.astype(BF16)
            kb_ref[0, h0 + i] = (k2[:, lanes] + kr).astype(BF16)
            vb_ref[0, h0 + i] = (v2[:, lanes] + v_one).astype(BF16)

    ga_ref[0] = jax.nn.sigmoid(proj(_O_GA, 1024)).astype(BF16)
    gb_ref[0] = jax.nn.sigmoid(proj(_O_GB, 1024)).astype(BF16)


_R_CA, _R_SA, _R_CB, _R_SB, _R_ONE = 0, 8, 16, 32, 48
_TRIG_ROWS = 64


def _rope_trig(positions):
    pos = positions.astype(F32)[:, None, :]

    def cs(rot):
        inv = ROPE_THETA ** (-jnp.arange(0, rot, 2, dtype=F32) / rot)
        ang = pos * inv[None, :, None]
        return jnp.cos(ang), jnp.sin(ang)

    b, s = positions.shape
    ca, sa = cs(A_ROT_DIM)
    cb, sb = cs(B_ROPE_DIM)
    pad = jnp.zeros((b, _TRIG_ROWS - _R_ONE - 1, s), F32)
    return jnp.concatenate([ca, sa, cb, sb, jnp.ones((b, 1, s), F32), pad], axis=1)


def _rope_expansion():
    e = np.zeros((LANES, 4 * LANES), np.float32)
    half_a, half_b = A_ROT_DIM // 2, B_ROPE_DIM // 2
    for j in range(LANES):
        d = j % A_HEAD_DIM
        if d < A_ROT_DIM:
            e[_R_CA + d % half_a, j] = 1.0
            e[_R_SA + d % half_a, LANES + j] = -1.0 if d < half_a else 1.0
        else:
            e[_R_ONE, j] = 1.0
        d = j - B_NOPE_DIM
        if 0 <= d < B_ROPE_DIM:
            e[_R_CB + d % half_b, 2 * LANES + j] = 1.0
            e[_R_SB + d % half_b, 3 * LANES + j] = -1.0 if d < half_b else 1.0
        else:
            e[_R_ONE, 2 * LANES + j] = 1.0
    return jnp.asarray(np.concatenate([e, e, e], axis=0), BF16)


def _regroup_w_in(w_in):
    splits = [512, 128, 128, 512, 64, 8, 256, 128, 32, 1024, 1024]
    offs = np.concatenate([[0], np.cumsum(splits)])
    seg = [w_in[:, offs[i]:offs[i + 1]] for i in range(len(splits))]
    a_q, a_k, a_v, i_q, i_k, i_w, b_cq, b_ckv, b_kr, g_a, g_b = seg
    pad = jnp.zeros((w_in.shape[0], LANES - 64 - 32 - 8), w_in.dtype)
    return jnp.concatenate(
        [a_q, a_k, a_v, i_q, b_cq, b_ckv, g_a, g_b, i_k, b_kr, i_w, pad], axis=1)


def _pad_heads(w, width):
    r, h, d = w.shape
    return jnp.pad(w, ((0, 0), (0, 0), (0, width - d))).reshape(r, h * width)


def _const_spec(shape):
    nd = len(shape)
    return pl.BlockSpec(shape, lambda *_: (0,) * nd)


def _proj_call(x, trig, expand, gmix, wcat, gik, gq, gkv, wuq, wuk, wuv, tm):
    b, s, d = x.shape
    grid = (b, s // tm)
    tok = lambda w: pl.BlockSpec((1, tm, w), lambda i, j: (i, j, 0))
    head = lambda h, w: pl.BlockSpec((1, h, tm, w), lambda i, j: (i, 0, j, 0))
    in_specs = [tok(d), pl.BlockSpec((1, _TRIG_ROWS, tm), lambda i, j: (i, 0, j)),
                _const_spec(expand.shape),
        _const_spec(gmix.shape), _const_spec(wcat.shape), _const_spec(gik.shape),
        _const_spec(gq.shape), _const_spec(gkv.shape), _const_spec(wuq.shape),
        _const_spec(wuk.shape), _const_spec(wuv.shape)]
    sds = jax.ShapeDtypeStruct
    out_shape = [
        sds((b, A_HEADS, s, A_HEAD_DIM), BF16),
        sds((b, A_KV_HEADS, s, A_HEAD_DIM), BF16),
        sds((b, A_KV_HEADS, s, LANES), BF16),
        sds((b, IDX_HEADS, s, IDX_DIM), BF16),
        sds((b, s, IDX_DIM), BF16),
        sds((b, s, LANES), F32),
        sds((b, B_HEADS, s, LANES), BF16),
        sds((b, B_HEADS, s, LANES), BF16),
        sds((b, B_HEADS, s, LANES), BF16),
        sds((b, s, d), BF16),
        sds((b, s, d), BF16),
    ]
    out_specs = [
        head(A_HEADS, A_HEAD_DIM), head(A_KV_HEADS, A_HEAD_DIM),
        head(A_KV_HEADS, LANES), head(IDX_HEADS, IDX_DIM), tok(IDX_DIM),
        tok(LANES), head(B_HEADS, LANES), head(B_HEADS, LANES),
        head(B_HEADS, LANES), tok(d), tok(d)]
    return pl.pallas_call(
        _proj_kernel,
        grid=grid,
        in_specs=in_specs,
        out_specs=out_specs,
        out_shape=out_shape,
        compiler_params=pltpu.CompilerParams(
            dimension_semantics=("arbitrary", "arbitrary"),
            vmem_limit_bytes=VMEM_LIMIT_BYTES),
        name="proj",
    )(x, trig, expand, gmix, wcat, gik, gq, gkv, wuq, wuk, wuv)


def _prepare_proj_weights(norm_mix_g, w_in, idx_k_norm_g, q_a_norm_g, kv_a_norm_g,
                          w_uq, w_uk, w_uv):
    gmix = norm_mix_g.reshape(1, -1)
    wcat = _regroup_w_in(w_in).astype(BF16)
    gik = jnp.pad(idx_k_norm_g, (0, LANES - IDX_DIM)).reshape(1, LANES)
    gq = q_a_norm_g.reshape(1, -1)
    gkv = kv_a_norm_g.reshape(1, -1)
    wuq = _pad_heads(w_uq, LANES).astype(BF16)
    wuk = _pad_heads(w_uk, LANES).astype(BF16)
    wuv = _pad_heads(w_uv, LANES).astype(BF16)
    return gmix, wcat, gik, gq, gkv, wuq, wuk, wuv


_INT_MIN = -2 ** 31
_ORD_NEG_INF = 0x007FFFFF
_ORD_LOWEST_FINITE = 0x00800000


def _ordinal_to_float(u):
    o = u ^ jnp.int32(_INT_MIN)
    bits = o ^ ((o >> 31) & jnp.int32(0x7FFFFFFF))
    return lax.bitcast_convert_type(bits, F32)


def _softmax_update(s, v, m_old, acc, rows_may_be_empty=True):
    m_new = jnp.maximum(m_old, jnp.max(s, axis=-1, keepdims=True))
    if rows_may_be_empty:
        m_new = jnp.maximum(m_new, F32_LOWEST)
    p = jnp.concatenate(
        [jnp.exp2(s[:, j * LANES:(j + 1) * LANES] - m_new)
         for j in range(s.shape[1] // LANES)], axis=1)
    alpha = jnp.exp2(m_old - m_new)
    return m_new, alpha * acc + _dot(p.astype(BF16), v)


def _attend_chunks(n_heads, m_ref, acc_ref, chunk_fns):
    for h in range(n_heads):
        m, acc = m_ref[h], acc_ref[h]
        for fn in chunk_fns:
            m, acc = _softmax_update(*fn(h), m, acc)
        m_ref[h] = m
        acc_ref[h] = acc


def _for_chunk_groups(n, body):
    def group(i, carry):
        body(tuple(4 * i + r for r in range(4)))
        return carry

    lax.fori_loop(0, lax.shift_right_logical(n, 2), group, 0)
    done = n & ~3

    @pl.when((n & 2) != 0)
    def _():
        body((done, done + 1))

    @pl.when((n & 1) != 0)
    def _():
        body((n - 1,))


def _init_softmax_state(m_ref, acc_ref):
    m_ref[...] = jnp.full(m_ref.shape, NEG_INF, F32)
    acc_ref[...] = jnp.zeros(acc_ref.shape, F32)


def _finish_heads(acc_even, acc_odd, lane):
    o0 = acc_even * (1.0 / acc_even[:, B_V_DIM:B_V_DIM + 1])
    o1 = acc_odd * (1.0 / acc_odd[:, B_V_DIM:B_V_DIM + 1])
    return jnp.where(lane < B_V_DIM, o0, pltpu.roll(o1, B_V_DIM, axis=1))


def _store_heads(o_ref, acc_ref, n_heads):
    lane = lax.broadcasted_iota(jnp.int32, acc_ref.shape[1:], 1)
    for pair in range(n_heads // 2):
        o_ref[0, :, pair * LANES:(pair + 1) * LANES] = _finish_heads(
            acc_ref[2 * pair], acc_ref[2 * pair + 1], lane).astype(o_ref.dtype)


_COARSE_PASSES = 16
_FINE_PASSES = 17
_PASSES_PER_CHUNK = 4


def _attn_kernel(ik_ref, ak_ref, av_ref, kb_ref, vb_ref, posk_col_ref, posk_row_ref,
                 iq0_ref, iq1_ref, iw0_ref, iw1_ref, aq0_ref, aq1_ref, qb0_ref, qb1_ref,
                 pqr0_ref, pqr1_ref, pqc0_ref, pqc1_ref,
                 oa0_ref, oa1_ref, ob0_ref, ob1_ref,
                 isc_ref, xb_ref, bias_ref, wt_ref, ma_ref, acca_ref, mb_ref, accb_ref,
                 *, topk, nq):
    pi = pl.program_id(1)
    tq = aq0_ref.shape[2]
    half = tq // 2
    nk0 = pi + 1
    n_virtual = nq + 1
    blocks = (
        dict(idx=0, nk=nk0, off=0, iq=iq0_ref, iw=iw0_ref, aq=aq0_ref, pqr=pqr0_ref,
             oa=oa0_ref, ob=ob0_ref),
        dict(idx=1, nk=nq - pi, off=nk0, iq=iq1_ref, iw=iw1_ref, aq=aq1_ref, pqr=pqr1_ref,
             oa=oa1_ref, ob=ob1_ref),
    )

    for blk in blocks:
        wt_ref[blk["idx"]] = blk["iw"][0].T[_T_IW:_T_IW + IDX_HEADS, :]
        posq = blk["pqr"][0]

        def idx_chunk(c, carry, blk=blk, posq=posq):
            v = blk["off"] + c
            for sub in range(2):
                k0 = pl.multiple_of(c * tq + sub * half, half)
                ikc = ik_ref[0, pl.ds(k0, half), :]
                acc = jnp.zeros((half, tq), F32)
                for h in range(IDX_HEADS):
                    d = _dot_nt(ikc, blk["iq"][0, h])
                    acc = acc + wt_ref[blk["idx"], h:h + 1, :] * jnp.maximum(d, 0.0)
                causal = posk_col_ref[0, pl.ds(k0, half), :] <= posq
                score = jnp.where(causal, acc, NEG_INF)
                isc_ref[v, sub * half:(sub + 1) * half, :] = score
                xb_ref[v, sub * half:(sub + 1) * half, :] = score.astype(BF16)
            return carry

        lax.fori_loop(0, blk["nk"], idx_chunk, 0)

    _init_softmax_state(mb_ref, accb_ref)

    def latent_chunk(v, diagonal_of=None):
        if diagonal_of is not None:
            in1 = diagonal_of == 1
            c = (nq - 1 - pi) if in1 else (nk0 - 1)
            b = diagonal_of
            q_ref, pqc_ref = (qb1_ref, pqc1_ref) if in1 else (qb0_ref, pqc0_ref)
        else:
            in1 = v >= nk0
            c = jnp.where(in1, v - nk0, v - 1)
            b = in1.astype(jnp.int32)
        k0 = pl.multiple_of(c * tq, tq)
        if diagonal_of is not None:
            causal = posk_row_ref[0, :, pl.ds(k0, tq)] <= pqc_ref[0]
        for h in range(B_HEADS):
            if diagonal_of is not None:
                q = q_ref[0, h]
            else:
                q = jnp.where(in1, qb1_ref[0, h], qb0_ref[0, h])
            s = _dot_nt(q, kb_ref[0, h, pl.ds(k0, tq), :])
            if diagonal_of is not None:
                s = jnp.where(causal, s, NEG_INF)
            m, acc = _softmax_update(s, vb_ref[0, h, pl.ds(k0, tq), :],
                                     mb_ref[b, h], accb_ref[b, h], rows_may_be_empty=False)
            mb_ref[b, h] = m
            accb_ref[b, h] = acc

    packed = 2 * SUBLANES
    one_b, zero_b = jnp.asarray(1, BF16), jnp.asarray(0, BF16)

    def count_ge(thr0, thr1, coarse):
        rows = packed if coarse else SUBLANES
        dtype = BF16 if coarse else F32
        acc0 = jnp.zeros((rows, tq), dtype)
        acc1 = jnp.zeros((rows, tq), dtype)
        zero = jnp.zeros((rows, tq), dtype)
        for v in range(n_virtual):
            in1 = v >= nk0
            thr = jnp.where(in1, thr1, thr0)
            if coarse:
                hit = jnp.where(xb_ref[v] >= thr, one_b, zero_b)
            else:
                hit = jnp.where(isc_ref[v] >= thr, 1.0, 0.0)
            part = hit[:rows]
            for r in range(1, tq // rows):
                part = part + hit[r * rows:(r + 1) * rows]
            acc0 = acc0 + jnp.where(in1, zero, part)
            acc1 = acc1 + jnp.where(in1, part, zero)
        return (jnp.sum(acc0.astype(F32), axis=0, keepdims=True),
                jnp.sum(acc1.astype(F32), axis=0, keepdims=True))

    def coarse_ordinal(j):
        low = jnp.where(j < 0x8000, jnp.int32(0xFFFF), jnp.int32(0))
        return lax.shift_left(j, 16) | low

    def coarse_pass(it, js):
        cands = [j | lax.shift_left(jnp.int32(1), 15 - it) for j in js]
        thrs = [_ordinal_to_float(coarse_ordinal(c)).astype(BF16) for c in cands]
        cnts = count_ge(*thrs, coarse=True)
        return tuple(jnp.where(n >= topk, c, j) for n, c, j in zip(cnts, cands, js))

    def fine_pass(it, ds, bases):
        cands = [d | lax.shift_left(jnp.int32(1), 16 - it) for d in ds]
        thrs = [_ordinal_to_float(b + c) for b, c in zip(bases, cands)]
        cnts = count_ge(*thrs, coarse=False)
        return tuple(jnp.where(n >= topk, c, d) for n, c, d in zip(cnts, cands, ds))

    zeros = (jnp.zeros((1, tq), jnp.int32),) * 2
    first_fine = _COARSE_PASSES // _PASSES_PER_CHUNK

    def coarse_body(v, js, diagonal_of=None):
        latent_chunk(v, diagonal_of)
        for r in range(_PASSES_PER_CHUNK):
            js = coarse_pass(_PASSES_PER_CHUNK * v + r, js)
        return js

    js = lax.fori_loop(1, first_fine, coarse_body, coarse_body(0, zeros, diagonal_of=0))
    js = [jnp.maximum(j, _ORD_NEG_INF >> 16) for j in js]
    bases = [coarse_ordinal(j) - jnp.int32(1 << 16) for j in js]

    def fine_body(v, ds):
        latent_chunk(v)
        for r in range(_PASSES_PER_CHUNK):
            ds = fine_pass(_PASSES_PER_CHUNK * (v - first_fine) + r, ds, bases)
        return ds

    ds = lax.fori_loop(first_fine, n_virtual - 1, fine_body, zeros)
    latent_chunk(n_virtual - 1, diagonal_of=1)
    ds = fine_pass(_FINE_PASSES - 1, ds, bases)

    for blk, base, d in zip(blocks, bases, ds):
        t_u = base + d
        short = (t_u >= 0) & (t_u < _ORD_LOWEST_FINITE)
        thr = _ordinal_to_float(jnp.where(short, jnp.int32(_ORD_LOWEST_FINITE), t_u))
        off, nk = blk["off"], blk["nk"]

        def write_bias(c, n_ge, thr=thr, off=off):
            sel = isc_ref[off + c] >= thr
            bias_ref[off + c] = jnp.where(sel, 0.0, NEG_INF).T
            return n_ge + jnp.sum(jnp.where(sel, 1.0, 0.0), axis=0, keepdims=True)

        n_ge = lax.fori_loop(0, nk, write_bias, jnp.zeros((1, tq), F32))
        tied = jnp.logical_and(n_ge > topk, jnp.logical_not(short))

        @pl.when(jnp.max(jnp.where(tied, 1.0, 0.0)) > 0.0)
        def _(thr=thr, off=off, nk=nk):
            def count_gt(c, acc):
                hit = jnp.where(isc_ref[off + c] > thr, 1.0, 0.0)
                return acc + jnp.sum(hit, axis=0, keepdims=True)
            n_gt = lax.fori_loop(0, nk, count_gt, jnp.zeros((1, tq), F32))
            need = topk - n_gt
            row = lax.broadcasted_iota(jnp.int32, (tq, tq), 0)
            col = lax.broadcasted_iota(jnp.int32, (tq, tq), 1)
            below = jnp.where(col < row, 1.0, 0.0).astype(BF16)

            def fix(c, seen):
                x = isc_ref[off + c]
                eq = jnp.where(x == thr, 1.0, 0.0)
                rank = _dot(below, eq.astype(BF16)) + seen
                sel = (x > thr) | ((x == thr) & (rank < need))
                bias_ref[off + c] = jnp.where(sel, 0.0, NEG_INF).T
                return seen + jnp.sum(eq, axis=0, keepdims=True)

            lax.fori_loop(0, nk, fix, jnp.zeros((1, tq), F32))

    _init_softmax_state(ma_ref, acca_ref)
    for blk in blocks:
        def attend(chunks, blk=blk):
            def chunk_fn(c):
                k0 = pl.multiple_of(c * tq, tq)
                bias = bias_ref[blk["off"] + c]

                def fn(h):
                    g = h // A_GROUP
                    s = _dot_nt(blk["aq"][0, h], ak_ref[0, g, pl.ds(k0, tq), :]) + bias
                    return s, av_ref[0, g, pl.ds(k0, tq), :]
                return fn

            _attend_chunks(A_HEADS, ma_ref.at[blk["idx"]], acca_ref.at[blk["idx"]],
                           [chunk_fn(c) for c in chunks])

        _for_chunk_groups(blk["nk"], attend)
        _store_heads(blk["oa"], acca_ref.at[blk["idx"]], A_HEADS)
        _store_heads(blk["ob"], accb_ref.at[blk["idx"]], B_HEADS)


def _attn_call(ik, iq, iw, aq, ak, av, qb, kb, vb, positions, tq, topk):
    b, heads, s, _ = aq.shape
    nq = s // tq
    n_pairs = nq // 2
    assert s % tq == 0 and nq % 2 == 0
    assert nq + 1 == (_COARSE_PASSES + _FINE_PASSES - 1) // _PASSES_PER_CHUNK + 1
    assert (nq + 1) * tq // (2 * SUBLANES) <= 256
    pos_col = positions[:, :, None]
    pos_row = positions[:, None, :]

    def per_seq(*shape):
        nd = len(shape)
        return pl.BlockSpec((1,) + shape, lambda i, j: (i,) + (0,) * nd)

    blk_index = (lambda j: j, lambda j: nq - 1 - j)

    def qhead(h, w, which):
        return pl.BlockSpec((1, h, tq, w), lambda i, j: (i, 0, blk_index[which](j), 0))

    def qrows(w, which):
        return pl.BlockSpec((1, tq, w), lambda i, j: (i, blk_index[which](j), 0))

    def qlanes(which):
        return pl.BlockSpec((1, 1, tq), lambda i, j: (i, 0, blk_index[which](j)))

    in_specs = [
        per_seq(s, IDX_DIM), per_seq(A_KV_HEADS, s, A_HEAD_DIM), per_seq(A_KV_HEADS, s, LANES),
        per_seq(B_HEADS, s, LANES), per_seq(B_HEADS, s, LANES),
        per_seq(s, 1), per_seq(1, s),
        qhead(IDX_HEADS, IDX_DIM, 0), qhead(IDX_HEADS, IDX_DIM, 1),
        qrows(LANES, 0), qrows(LANES, 1),
        qhead(A_HEADS, A_HEAD_DIM, 0), qhead(A_HEADS, A_HEAD_DIM, 1),
        qhead(B_HEADS, LANES, 0), qhead(B_HEADS, LANES, 1),
        qlanes(0), qlanes(1), qrows(1, 0), qrows(1, 1),
    ]
    width = heads * A_HEAD_DIM
    out_sds = jax.ShapeDtypeStruct((b, s // 2, width), BF16)
    out_lo = pl.BlockSpec((1, tq, width), lambda i, j: (i, j, 0))
    out_hi = pl.BlockSpec((1, tq, width), lambda i, j: (i, n_pairs - 1 - j, 0))
    state = pltpu.VMEM((2, heads, tq, LANES), F32)
    return pl.pallas_call(
        functools.partial(_attn_kernel, topk=topk, nq=nq),
        grid=(b, n_pairs),
        in_specs=in_specs,
        out_specs=[out_lo, out_hi, out_lo, out_hi],
        out_shape=[out_sds] * 4,
        scratch_shapes=[
            pltpu.VMEM((nq + 1, tq, tq), F32),
            pltpu.VMEM((nq + 1, tq, tq), BF16),
            pltpu.VMEM((nq + 1, tq, tq), F32),
            pltpu.VMEM((2, IDX_HEADS, tq), F32),
            state, state, state, state,
        ],
        compiler_params=pltpu.CompilerParams(
            dimension_semantics=("arbitrary", "arbitrary"),
            vmem_limit_bytes=VMEM_LIMIT_BYTES),
        name="attn",
    )(ik, ak, av, kb, vb, pos_col, pos_row, iq, iq, iw, iw, aq, aq, qb, qb,
      pos_row, pos_row, pos_col, pos_col)


def _shift_rows(u, prev_tail, shift, row8):
    rolled = pltpu.roll(u, shift, axis=0)
    head = jnp.where(row8 < shift, pltpu.roll(prev_tail, shift, axis=0),
                     rolled[:SUBLANES])
    return jnp.concatenate([head, rolled[SUBLANES:]], axis=0)


def _ffn_kernel(x_ref, oa_lo_ref, oa_hi_ref, ob_lo_ref, ob_hi_ref, ga_ref, gb_ref,
                wba_ref, wbb_ref, wout_ref, gffn_ref, wup_ref, cw_ref, cb_ref, wdown_ref,
                gfin_ref, out_ref, tail_ref, *, n_chunks, final_norm):
    j = pl.program_id(1)
    x = x_ref[0]
    tm = x.shape[0]
    first_half = j < pl.num_programs(1) // 2
    oa = jnp.where(first_half, oa_lo_ref[0], oa_hi_ref[0])
    ob = jnp.where(first_half, ob_lo_ref[0], ob_hi_ref[0])
    ya = _dot(oa, wba_ref[...])
    yb = _dot(ob, wbb_ref[...])
    merged = ga_ref[0].astype(F32) * ya + gb_ref[0].astype(F32) * yb
    h1 = x + _dot(merged.astype(BF16), wout_ref[...])
    ms = jnp.mean(h1 * h1, axis=-1, keepdims=True)
    hn = (h1 * lax.rsqrt(ms + EPS) * gffn_ref[...]).astype(BF16)

    fc = wdown_ref.shape[1]
    row8 = lax.broadcasted_iota(jnp.int32, (SUBLANES, fc), 0)

    @pl.when(j == 0)
    def _():
        tail_ref[...] = jnp.zeros_like(tail_ref)

    acc = h1
    for c in range(n_chunks):
        halves = []
        for part in range(2):
            idx = part * n_chunks + c
            u = _dot(hn, wup_ref[:, idx * fc:(idx + 1) * fc])
            prev_tail = tail_ref[idx]
            tail_ref[idx] = u[tm - SUBLANES:]
            y = (cb_ref[idx] + cw_ref[0, idx] * _shift_rows(u, prev_tail, 2, row8)
                 + cw_ref[1, idx] * _shift_rows(u, prev_tail, 1, row8)
                 + cw_ref[2, idx] * u)
            halves.append(y)
        gate, val = halves
        act = (gate * jax.nn.sigmoid(gate) * val).astype(BF16)
        acc = acc + _dot(act, wdown_ref[c])
    if final_norm:
        ms = jnp.mean(acc * acc, axis=-1, keepdims=True)
        acc = acc * lax.rsqrt(ms + EPS) * gfin_ref[...]
    out_ref[0] = acc


def _ffn_call(x, oa_lo, oa_hi, ob_lo, ob_hi, ga, gb, wba, wbb, wout, gffn, wup, cw, cb,
              wdown, gfin, tm, final_norm):
    b, s, d = x.shape
    n_chunks, fc, _ = wdown.shape
    n_parts = 2 * n_chunks
    assert wup.shape == (d, n_parts * fc) and fc % LANES == 0
    n_tiles = s // tm
    assert s % tm == 0 and n_tiles % 2 == 0
    n_half = n_tiles // 2
    tok = lambda w: pl.BlockSpec((1, tm, w), lambda i, j: (i, j, 0))
    lo = lambda w: pl.BlockSpec((1, tm, w), lambda i, j: (i, jnp.minimum(j, n_half - 1), 0))
    hi = lambda w: pl.BlockSpec((1, tm, w), lambda i, j: (i, jnp.maximum(j - n_half, 0), 0))
    wa, wb = oa_lo.shape[-1], ob_lo.shape[-1]
    weights = (wba, wbb, wout, gffn, wup, cw, cb, wdown, gfin)
    return pl.pallas_call(
        functools.partial(_ffn_kernel, n_chunks=n_chunks, final_norm=final_norm),
        grid=(b, n_tiles),
        in_specs=[tok(d), lo(wa), hi(wa), lo(wb), hi(wb), tok(d), tok(d)]
        + [_const_spec(w.shape) for w in weights],
        out_specs=tok(d),
        out_shape=jax.ShapeDtypeStruct((b, s, d), x.dtype),
        scratch_shapes=[pltpu.VMEM((n_parts, SUBLANES, fc), F32)],
        compiler_params=pltpu.CompilerParams(
            dimension_semantics=("arbitrary", "arbitrary"),
            vmem_limit_bytes=VMEM_LIMIT_BYTES),
        name="ffn",
    )(x, oa_lo, oa_hi, ob_lo, ob_hi, ga, gb, *weights)


def _prepare_ffn_weights(w_branch_a, w_branch_b, w_out, norm_ffn_g, w_up, conv_w,
                         conv_b, w_down, norm_final_g, n_chunks):
    d, two_f = w_up.shape
    fc = two_f // (2 * n_chunks)
    wup = w_up.astype(BF16)
    cw = conv_w.reshape(CONV_WIDTH, 2 * n_chunks, 1, fc)
    cb = conv_b.reshape(2 * n_chunks, 1, fc)
    wdown = w_down.reshape(n_chunks, fc, d).astype(BF16)
    return (w_branch_a.astype(BF16), w_branch_b.astype(BF16), w_out.astype(BF16),
            norm_ffn_g.reshape(1, -1), wup, cw, cb, wdown, norm_final_g.reshape(1, -1))


TM_PROJ = 512
TQ_ATTN = 256
TM_FFN = 512
FFN_CHUNKS = 1


def kernel(x, positions, norm_mix_g, w_in, idx_k_norm_g, q_a_norm_g, kv_a_norm_g, w_uq, w_uk, w_uv, w_branch_a, w_branch_b, w_out, norm_ffn_g, w_up, conv_w, conv_b, w_down, norm_final_g):
    b, s, _ = x.shape
    depth = w_in.shape[0]
    topk = min(INDEX_TOPK_MAX, s // 4)
    trig = _rope_trig(positions)
    expand = _rope_expansion()
    h = x
    for l in range(depth):
        pw = _prepare_proj_weights(norm_mix_g[l], w_in[l], idx_k_norm_g[l], q_a_norm_g[l],
                                   kv_a_norm_g[l], w_uq[l], w_uk[l], w_uv[l])
        aq, ak, av, iq, ik, iw, qb, kb, vb, ga, gb = _proj_call(h, trig, expand, *pw, tm=TM_PROJ)
        mixed = _attn_call(ik, iq, iw, aq, ak, av, qb, kb, vb, positions, TQ_ATTN, topk)
        fw = _prepare_ffn_weights(w_branch_a[l], w_branch_b[l], w_out[l], norm_ffn_g[l],
                                  w_up[l], conv_w[l], conv_b[l], w_down[l], norm_final_g,
                                  FFN_CHUNKS)
        h = _ffn_call(h, *mixed, ga, gb, *fw, tm=TM_FFN, final_norm=(l == depth - 1))
    return h
```

```python
import functools

import jax
import jax.numpy as jnp
import numpy as np
from jax import lax
from jax.experimental import pallas as pl
from jax.experimental.pallas import tpu as pltpu

ROPE_THETA = 500000.0
EPS = 1e-6
A_HEADS = 8
A_KV_HEADS = 2
A_GROUP = A_HEADS // A_KV_HEADS
A_HEAD_DIM = 64
A_ROT_DIM = 16
IDX_HEADS = 8
IDX_DIM = 64
INDEX_TOPK_MAX = 256
B_HEADS = 8
B_Q_RANK = 256
B_KV_RANK = 128
B_NOPE_DIM = 64
B_ROPE_DIM = 32
B_V_DIM = 64
CONV_WIDTH = 3

LANES = 128
SUBLANES = 8
VMEM_LIMIT_BYTES = 56 * 1024 * 1024

_T_IW = IDX_DIM + B_ROPE_DIM
assert IDX_DIM == B_NOPE_DIM and _T_IW + IDX_HEADS <= LANES


def _proj_layout(d_model):
    widths = dict(aq=A_HEADS * A_HEAD_DIM, ak=A_KV_HEADS * A_HEAD_DIM,
                  av=A_KV_HEADS * A_HEAD_DIM, iq=IDX_HEADS * IDX_DIM, cq=B_Q_RANK,
                  ckv=B_KV_RANK, ga=d_model, gb=d_model, tail=LANES)
    offsets, total = {}, 0
    for name, width in widths.items():
        assert width % LANES == 0
        offsets[name] = total
        total += width
    return offsets, total

F32 = jnp.float32
BF16 = jnp.bfloat16
NEG_INF = float("-inf")
LOG2_E = 1.4426950408889634
F32_LOWEST = float(np.finfo(np.float32).min)


def _dot(a, b):
    return jnp.dot(a, b, preferred_element_type=F32)


def _dot_nt(a, b):
    return lax.dot_general(a, b, (((1,), (1,)), ((), ())), preferred_element_type=F32)


def _rope_lanes(x, cos_t, sin_t, take_up, half):
    up = pltpu.roll(x, LANES - half, axis=1)
    dn = pltpu.roll(x, half, axis=1)
    return x * cos_t + jnp.where(take_up, up, dn) * sin_t


def _expand_rope_tables(trig_t, expand_ref):
    tm = trig_t[0].shape[1]
    used = sum(piece.shape[0] for piece in trig_t)
    rest = lax.broadcasted_iota(jnp.int32, (LANES - used, tm), 0)
    ones_then_zeros = jnp.where(rest == 0, 1.0, 0.0)
    t = jnp.concatenate(list(trig_t) + [ones_then_zeros], axis=0).T
    a1 = t.astype(BF16)
    r1 = t - a1.astype(F32)
    a2 = r1.astype(BF16)
    a3 = (r1 - a2.astype(F32)).astype(BF16)
    tables = _dot(jnp.concatenate([a1, a2, a3], axis=1), expand_ref[...])
    return [tables[:, i * LANES:(i + 1) * LANES] for i in range(4)]


def _proj_kernel(x_ref, ca_ref, sa_ref, cb_ref, sb_ref, expand_ref, gmix_ref, w_ref,
                 gik_ref, gq_ref, gkv_ref, wuq_ref, wuk_ref, wuv_ref,
                 aq_ref, ak_ref, av_ref, iq_ref, ik_ref, iw_ref, qb_ref, kb_ref,
                 vb_ref, ga_ref, gb_ref, *, offs):
    x = x_ref[0]
    tm, d_model = x.shape
    ms = jnp.mean(x * x, axis=-1, keepdims=True)
    hn = (x * lax.rsqrt(ms + EPS) * gmix_ref[...]).astype(BF16)

    lane = lax.broadcasted_iota(jnp.int32, (tm, LANES), 1)
    ca, sa, cb, sb = _expand_rope_tables(
        (ca_ref[0], sa_ref[0], cb_ref[0], sb_ref[0]), expand_ref)
    up_a = (lane % A_HEAD_DIM) < (A_ROT_DIM // 2)
    up_b = lane < (B_NOPE_DIM + B_ROPE_DIM // 2)
    v_one = jnp.where(lane == A_HEAD_DIM, 1.0, 0.0).astype(F32)
    low_half = lane < A_HEAD_DIM

    def proj(off, width):
        return _dot(hn, w_ref[:, off:off + width])

    def store_heads(ref, tile128, first_head, scale):
        t = tile128 * scale if scale != 1.0 else tile128
        ref[0, first_head] = t[:, :A_HEAD_DIM].astype(ref.dtype)
        ref[0, first_head + 1] = t[:, A_HEAD_DIM:].astype(ref.dtype)

    assert A_HEADS * A_HEAD_DIM == IDX_HEADS * IDX_DIM and A_HEAD_DIM == IDX_DIM
    n_head_pairs = A_HEADS // 2
    for off, ref, scale in ((offs["aq"], aq_ref, A_HEAD_DIM ** -0.5 * LOG2_E),
                            (offs["iq"], iq_ref, IDX_DIM ** -0.5)):
        full = proj(off, n_head_pairs * LANES)
        for j in range(n_head_pairs):
            t = _rope_lanes(full[:, j * LANES:(j + 1) * LANES], ca, sa, up_a,
                            A_ROT_DIM // 2)
            store_heads(ref, t, 2 * j, scale)

    assert A_KV_HEADS * A_HEAD_DIM == LANES and offs["av"] == offs["ak"] + LANES
    akv = proj(offs["ak"], 2 * LANES)
    store_heads(ak_ref, _rope_lanes(akv[:, :LANES], ca, sa, up_a, A_ROT_DIM // 2),
                0, 1.0)
    v2 = akv[:, LANES:]
    av_ref[0, 0] = (jnp.where(low_half, v2, 0.0) + v_one).astype(BF16)
    v2r = pltpu.roll(v2, A_HEAD_DIM, axis=1)
    av_ref[0, 1] = (jnp.where(low_half, v2r, 0.0) + v_one).astype(BF16)

    tail = proj(offs["tail"], LANES)
    ik_sq = jnp.where(low_half, tail * tail, 0.0)
    ik_ms = jnp.sum(ik_sq, axis=-1, keepdims=True) * (1.0 / IDX_DIM)
    ik_n = tail * lax.rsqrt(ik_ms + EPS) * gik_ref[...]
    ik_r = _rope_lanes(ik_n, ca, sa, up_a, A_ROT_DIM // 2)
    ik_ref[0] = ik_r[:, :IDX_DIM].astype(BF16)
    iw_ref[0] = tail * (IDX_HEADS ** -0.5)
    kr = _rope_lanes(tail, cb, sb, up_b, B_ROPE_DIM // 2)
    kr = jnp.where((lane >= B_NOPE_DIM) & (lane < B_NOPE_DIM + B_ROPE_DIM), kr, 0.0)

    cq = proj(offs["cq"], B_Q_RANK)
    cq_n = (cq * lax.rsqrt(jnp.mean(cq * cq, axis=-1, keepdims=True) + EPS)
            * gq_ref[...]).astype(BF16)
    ckv = proj(offs["ckv"], B_KV_RANK)
    ckv_n = (ckv * lax.rsqrt(jnp.mean(ckv * ckv, axis=-1, keepdims=True) + EPS)
             * gkv_ref[...]).astype(BF16)
    b_scale = (B_NOPE_DIM + B_ROPE_DIM) ** -0.5 * LOG2_E
    for h0 in range(0, B_HEADS, 2):
        cols = slice(h0 * LANES, (h0 + 2) * LANES)
        q2 = _dot(cq_n, wuq_ref[:, cols])
        k2 = _dot(ckv_n, wuk_ref[:, cols])
        v2 = _dot(ckv_n, wuv_ref[:, cols])
        for i in range(2):
            lanes = slice(i * LANES, (i + 1) * LANES)
            q = _rope_lanes(q2[:, lanes], cb, sb, up_b, B_ROPE_DIM // 2) * b_scale
            qb_ref[0, h0 + i] = q.astype(BF16)
            kb_ref[0, h0 + i] = (k2[:, lanes] + kr).astype(BF16)
            vb_ref[0, h0 + i] = (v2[:, lanes] + v_one).astype(BF16)

    ga_ref[0] = jax.nn.sigmoid(proj(offs["ga"], d_model)).astype(BF16)
    gb_ref[0] = jax.nn.sigmoid(proj(offs["gb"], d_model)).astype(BF16)


_R_CA = 0
_R_SA = _R_CA + A_ROT_DIM // 2
_R_CB = _R_SA + A_ROT_DIM // 2
_R_SB = _R_CB + B_ROPE_DIM // 2
_R_ONE = _R_SB + B_ROPE_DIM // 2


def _rope_trig(positions):
    pos = positions.astype(F32)[:, None, :]

    def cs(rot):
        inv = ROPE_THETA ** (-jnp.arange(0, rot, 2, dtype=F32) / rot)
        ang = pos * inv[None, :, None]
        return jnp.cos(ang), jnp.sin(ang)

    return cs(A_ROT_DIM) + cs(B_ROPE_DIM)


def _rope_expansion():
    e = np.zeros((LANES, 4 * LANES), np.float32)
    half_a, half_b = A_ROT_DIM // 2, B_ROPE_DIM // 2
    for j in range(LANES):
        d = j % A_HEAD_DIM
        if d < A_ROT_DIM:
            e[_R_CA + d % half_a, j] = 1.0
            e[_R_SA + d % half_a, LANES + j] = -1.0 if d < half_a else 1.0
        else:
            e[_R_ONE, j] = 1.0
        d = j - B_NOPE_DIM
        if 0 <= d < B_ROPE_DIM:
            e[_R_CB + d % half_b, 2 * LANES + j] = 1.0
            e[_R_SB + d % half_b, 3 * LANES + j] = -1.0 if d < half_b else 1.0
        else:
            e[_R_ONE, 2 * LANES + j] = 1.0
    return jnp.asarray(np.concatenate([e, e, e], axis=0), BF16)


def _regroup_w_in(w_in):
    d_model = w_in.shape[0]
    a_kv = A_KV_HEADS * A_HEAD_DIM
    splits = [A_HEADS * A_HEAD_DIM, a_kv, a_kv, IDX_HEADS * IDX_DIM, IDX_DIM, IDX_HEADS,
              B_Q_RANK, B_KV_RANK, B_ROPE_DIM, d_model, d_model]
    assert sum(splits) == w_in.shape[1]
    offs = np.concatenate([[0], np.cumsum(splits)])
    seg = [w_in[:, offs[i]:offs[i + 1]] for i in range(len(splits))]
    a_q, a_k, a_v, i_q, i_k, i_w, b_cq, b_ckv, b_kr, g_a, g_b = seg
    pad = jnp.zeros((d_model, LANES - _T_IW - IDX_HEADS), w_in.dtype)
    return jnp.concatenate(
        [a_q, a_k, a_v, i_q, b_cq, b_ckv, g_a, g_b, i_k, b_kr, i_w, pad], axis=1)


def _pad_heads(w, width):
    r, h, d = w.shape
    return jnp.pad(w, ((0, 0), (0, 0), (0, width - d))).reshape(r, h * width)


def _const_spec(shape):
    nd = len(shape)
    return pl.BlockSpec(shape, lambda *_: (0,) * nd)


def _proj_call(x, trig, expand, gmix, wcat, gik, gq, gkv, wuq, wuk, wuv, tm):
    b, s, d = x.shape
    grid = (b, s // tm)
    offs, total = _proj_layout(d)
    assert wcat.shape == (d, total)
    tok = lambda w: pl.BlockSpec((1, tm, w), lambda i, j: (i, j, 0))
    head = lambda h, w: pl.BlockSpec((1, h, tm, w), lambda i, j: (i, 0, j, 0))
    seq_on_lanes = lambda t: pl.BlockSpec((1, t.shape[1], tm), lambda i, j: (i, 0, j))
    consts = (expand, gmix, wcat, gik, gq, gkv, wuq, wuk, wuv)
    in_specs = ([tok(d)] + [seq_on_lanes(t) for t in trig]
                + [_const_spec(c.shape) for c in consts])
    sds = jax.ShapeDtypeStruct
    out_shape = [
        sds((b, A_HEADS, s, A_HEAD_DIM), BF16),
        sds((b, A_KV_HEADS, s, A_HEAD_DIM), BF16),
        sds((b, A_KV_HEADS, s, LANES), BF16),
        sds((b, IDX_HEADS, s, IDX_DIM), BF16),
        sds((b, s, IDX_DIM), BF16),
        sds((b, s, LANES), F32),
        sds((b, B_HEADS, s, LANES), BF16),
        sds((b, B_HEADS, s, LANES), BF16),
        sds((b, B_HEADS, s, LANES), BF16),
        sds((b, s, d), BF16),
        sds((b, s, d), BF16),
    ]
    out_specs = [
        head(A_HEADS, A_HEAD_DIM), head(A_KV_HEADS, A_HEAD_DIM),
        head(A_KV_HEADS, LANES), head(IDX_HEADS, IDX_DIM), tok(IDX_DIM),
        tok(LANES), head(B_HEADS, LANES), head(B_HEADS, LANES),
        head(B_HEADS, LANES), tok(d), tok(d)]
    return pl.pallas_call(
        functools.partial(_proj_kernel, offs=offs),
        grid=grid,
        in_specs=in_specs,
        out_specs=out_specs,
        out_shape=out_shape,
        compiler_params=pltpu.CompilerParams(
            dimension_semantics=("arbitrary", "arbitrary"),
            vmem_limit_bytes=VMEM_LIMIT_BYTES),
        name="proj",
    )(x, *trig, *consts)

def _prepare_proj_weights(norm_mix_g, w_in, idx_k_norm_g, q_a_norm_g, kv_a_norm_g,
                          w_uq, w_uk, w_uv):
    gmix = norm_mix_g.reshape(1, -1)
    wcat = _regroup_w_in(w_in).astype(BF16)
    gik = jnp.pad(idx_k_norm_g, (0, LANES - IDX_DIM)).reshape(1, LANES)
    gq = q_a_norm_g.reshape(1, -1)
    gkv = kv_a_norm_g.reshape(1, -1)
    wuq = _pad_heads(w_uq, LANES).astype(BF16)
    wuk = _pad_heads(w_uk, LANES).astype(BF16)
    wuv = _pad_heads(w_uv, LANES).astype(BF16)
    return gmix, wcat, gik, gq, gkv, wuq, wuk, wuv


_INT_MIN = -2 ** 31
_ORD_NEG_INF = 0x007FFFFF
_ORD_LOWEST_FINITE = 0x00800000


def _ordinal_to_float(u):
    o = u ^ jnp.int32(_INT_MIN)
    bits = o ^ ((o >> 31) & jnp.int32(0x7FFFFFFF))
    return lax.bitcast_convert_type(bits, F32)


def _softmax_update(s, v, m_old, acc, rows_may_be_empty=True):
    m_new = jnp.maximum(m_old, jnp.max(s, axis=-1, keepdims=True))
    if rows_may_be_empty:
        m_new = jnp.maximum(m_new, F32_LOWEST)
    p = jnp.concatenate(
        [jnp.exp2(s[:, j * LANES:(j + 1) * LANES] - m_new)
         for j in range(s.shape[1] // LANES)], axis=1)
    alpha = jnp.exp2(m_old - m_new)
    return m_new, alpha * acc + _dot(p.astype(BF16), v)


def _attend_chunks(n_heads, m_ref, acc_ref, chunk_fns):
    for h in range(n_heads):
        m, acc = m_ref[h], acc_ref[h]
        for fn in chunk_fns:
            m, acc = _softmax_update(*fn(h), m, acc)
        m_ref[h] = m
        acc_ref[h] = acc


def _for_chunk_groups(n, body):
    def group(i, carry):
        body(tuple(4 * i + r for r in range(4)))
        return carry

    lax.fori_loop(0, lax.shift_right_logical(n, 2), group, 0)
    done = n & ~3

    @pl.when((n & 2) != 0)
    def _():
        body((done, done + 1))

    @pl.when((n & 1) != 0)
    def _():
        body((n - 1,))


def _init_softmax_state(m_ref, acc_ref):
    m_ref[...] = jnp.full(m_ref.shape, NEG_INF, F32)
    acc_ref[...] = jnp.zeros(acc_ref.shape, F32)


def _finish_heads(acc_even, acc_odd, lane):
    o0 = acc_even * (1.0 / acc_even[:, B_V_DIM:B_V_DIM + 1])
    o1 = acc_odd * (1.0 / acc_odd[:, B_V_DIM:B_V_DIM + 1])
    return jnp.where(lane < B_V_DIM, o0, pltpu.roll(o1, B_V_DIM, axis=1))


def _store_heads(o_ref, acc_ref, n_heads):
    lane = lax.broadcasted_iota(jnp.int32, acc_ref.shape[1:], 1)
    for pair in range(n_heads // 2):
        o_ref[0, :, pair * LANES:(pair + 1) * LANES] = _finish_heads(
            acc_ref[2 * pair], acc_ref[2 * pair + 1], lane).astype(o_ref.dtype)


_COARSE_PASSES = 16
_FINE_PASSES = 17
_PASSES_PER_CHUNK = 4


def _attn_kernel(ik_ref, ak_ref, av_ref, kb_ref, vb_ref, posk_col_ref, posk_row_ref,
                 iq0_ref, iq1_ref, iw0_ref, iw1_ref, aq0_ref, aq1_ref, qb0_ref, qb1_ref,
                 pqr0_ref, pqr1_ref, pqc0_ref, pqc1_ref,
                 oa0_ref, oa1_ref, ob0_ref, ob1_ref,
                 isc_ref, xb_ref, bias_ref, wt_ref, ma_ref, acca_ref, mb_ref, accb_ref,
                 *, topk, nq):
    pi = pl.program_id(1)
    tq = aq0_ref.shape[2]
    half = tq // 2
    nk0 = pi + 1
    n_virtual = nq + 1
    blocks = (
        dict(idx=0, nk=nk0, off=0, iq=iq0_ref, iw=iw0_ref, aq=aq0_ref, pqr=pqr0_ref,
             oa=oa0_ref, ob=ob0_ref),
        dict(idx=1, nk=nq - pi, off=nk0, iq=iq1_ref, iw=iw1_ref, aq=aq1_ref, pqr=pqr1_ref,
             oa=oa1_ref, ob=ob1_ref),
    )

    for blk in blocks:
        wt_ref[blk["idx"]] = blk["iw"][0].T[_T_IW:_T_IW + IDX_HEADS, :]
        posq = blk["pqr"][0]

        def idx_chunk(c, carry, blk=blk, posq=posq):
            v = blk["off"] + c
            for sub in range(2):
                k0 = pl.multiple_of(c * tq + sub * half, half)
                ikc = ik_ref[0, pl.ds(k0, half), :]
                acc = jnp.zeros((half, tq), F32)
                for h in range(IDX_HEADS):
                    d = _dot_nt(ikc, blk["iq"][0, h])
                    acc = acc + wt_ref[blk["idx"], h:h + 1, :] * jnp.maximum(d, 0.0)
                causal = posk_col_ref[0, pl.ds(k0, half), :] <= posq
                score = jnp.where(causal, acc, NEG_INF)
                isc_ref[v, sub * half:(sub + 1) * half, :] = score
                xb_ref[v, sub * half:(sub + 1) * half, :] = score.astype(BF16)
            return carry

        lax.fori_loop(0, blk["nk"], idx_chunk, 0)

    _init_softmax_state(mb_ref, accb_ref)

    def latent_chunk(v, diagonal_of=None):
        if diagonal_of is not None:
            in1 = diagonal_of == 1
            c = (nq - 1 - pi) if in1 else (nk0 - 1)
            b = diagonal_of
            q_ref, pqc_ref = (qb1_ref, pqc1_ref) if in1 else (qb0_ref, pqc0_ref)
        else:
            in1 = v >= nk0
            c = jnp.where(in1, v - nk0, v - 1)
            b = in1.astype(jnp.int32)
        k0 = pl.multiple_of(c * tq, tq)
        if diagonal_of is not None:
            causal = posk_row_ref[0, :, pl.ds(k0, tq)] <= pqc_ref[0]
        for h in range(B_HEADS):
            if diagonal_of is not None:
                q = q_ref[0, h]
            else:
                q = jnp.where(in1, qb1_ref[0, h], qb0_ref[0, h])
            s = _dot_nt(q, kb_ref[0, h, pl.ds(k0, tq), :])
            if diagonal_of is not None:
                s = jnp.where(causal, s, NEG_INF)
            m, acc = _softmax_update(s, vb_ref[0, h, pl.ds(k0, tq), :],
                                     mb_ref[b, h], accb_ref[b, h], rows_may_be_empty=False)
            mb_ref[b, h] = m
            accb_ref[b, h] = acc

    packed = 2 * SUBLANES
    one_b, zero_b = jnp.asarray(1, BF16), jnp.asarray(0, BF16)

    def count_ge(thr0, thr1, coarse):
        rows = packed if coarse else SUBLANES
        dtype = BF16 if coarse else F32
        acc0 = jnp.zeros((rows, tq), dtype)
        acc1 = jnp.zeros((rows, tq), dtype)
        zero = jnp.zeros((rows, tq), dtype)
        for v in range(n_virtual):
            in1 = v >= nk0
            thr = jnp.where(in1, thr1, thr0)
            if coarse:
                hit = jnp.where(xb_ref[v] >= thr, one_b, zero_b)
            else:
                hit = jnp.where(isc_ref[v] >= thr, 1.0, 0.0)
            part = hit[:rows]
            for r in range(1, tq // rows):
                part = part + hit[r * rows:(r + 1) * rows]
            acc0 = acc0 + jnp.where(in1, zero, part)
            acc1 = acc1 + jnp.where(in1, part, zero)
        return (jnp.sum(acc0.astype(F32), axis=0, keepdims=True),
                jnp.sum(acc1.astype(F32), axis=0, keepdims=True))

    def coarse_ordinal(j):
        low = jnp.where(j < 0x8000, jnp.int32(0xFFFF), jnp.int32(0))
        return lax.shift_left(j, 16) | low

    def coarse_pass(it, js):
        cands = [j | lax.shift_left(jnp.int32(1), 15 - it) for j in js]
        thrs = [_ordinal_to_float(coarse_ordinal(c)).astype(BF16) for c in cands]
        cnts = count_ge(*thrs, coarse=True)
        return tuple(jnp.where(n >= topk, c, j) for n, c, j in zip(cnts, cands, js))

    def fine_pass(it, ds, bases):
        cands = [d | lax.shift_left(jnp.int32(1), 16 - it) for d in ds]
        thrs = [_ordinal_to_float(b + c) for b, c in zip(bases, cands)]
        cnts = count_ge(*thrs, coarse=False)
        return tuple(jnp.where(n >= topk, c, d) for n, c, d in zip(cnts, cands, ds))

    zeros = (jnp.zeros((1, tq), jnp.int32),) * 2
    first_fine = _COARSE_PASSES // _PASSES_PER_CHUNK

    def coarse_body(v, js, diagonal_of=None):
        latent_chunk(v, diagonal_of)
        for r in range(_PASSES_PER_CHUNK):
            js = coarse_pass(_PASSES_PER_CHUNK * v + r, js)
        return js

    js = lax.fori_loop(1, first_fine, coarse_body, coarse_body(0, zeros, diagonal_of=0))
    js = [jnp.maximum(j, _ORD_NEG_INF >> 16) for j in js]
    bases = [coarse_ordinal(j) - jnp.int32(1 << 16) for j in js]

    def fine_body(v, ds):
        latent_chunk(v)
        for r in range(_PASSES_PER_CHUNK):
            ds = fine_pass(_PASSES_PER_CHUNK * (v - first_fine) + r, ds, bases)
        return ds

    ds = lax.fori_loop(first_fine, n_virtual - 1, fine_body, zeros)
    latent_chunk(n_virtual - 1, diagonal_of=1)
    ds = fine_pass(_FINE_PASSES - 1, ds, bases)

    for blk, base, d in zip(blocks, bases, ds):
        t_u = base + d
        short = (t_u >= 0) & (t_u < _ORD_LOWEST_FINITE)
        thr = _ordinal_to_float(jnp.where(short, jnp.int32(_ORD_LOWEST_FINITE), t_u))
        off, nk = blk["off"], blk["nk"]

        def write_bias(c, n_ge, thr=thr, off=off):
            sel = isc_ref[off + c] >= thr
            bias_ref[off + c] = jnp.where(sel, 0.0, NEG_INF).T
            return n_ge + jnp.sum(jnp.where(sel, 1.0, 0.0), axis=0, keepdims=True)

        n_ge = lax.fori_loop(0, nk, write_bias, jnp.zeros((1, tq), F32))
        tied = jnp.logical_and(n_ge > topk, jnp.logical_not(short))

        @pl.when(jnp.max(jnp.where(tied, 1.0, 0.0)) > 0.0)
        def _(thr=thr, off=off, nk=nk):
            def count_gt(c, acc):
                hit = jnp.where(isc_ref[off + c] > thr, 1.0, 0.0)
                return acc + jnp.sum(hit, axis=0, keepdims=True)
            n_gt = lax.fori_loop(0, nk, count_gt, jnp.zeros((1, tq), F32))
            need = topk - n_gt
            row = lax.broadcasted_iota(jnp.int32, (tq, tq), 0)
            col = lax.broadcasted_iota(jnp.int32, (tq, tq), 1)
            below = jnp.where(col < row, 1.0, 0.0).astype(BF16)

            def fix(c, seen):
                x = isc_ref[off + c]
                eq = jnp.where(x == thr, 1.0, 0.0)
                rank = _dot(below, eq.astype(BF16)) + seen
                sel = (x > thr) | ((x == thr) & (rank < need))
                bias_ref[off + c] = jnp.where(sel, 0.0, NEG_INF).T
                return seen + jnp.sum(eq, axis=0, keepdims=True)

            lax.fori_loop(0, nk, fix, jnp.zeros((1, tq), F32))

    _init_softmax_state(ma_ref, acca_ref)
    for blk in blocks:
        def attend(chunks, blk=blk):
            def chunk_fn(c):
                k0 = pl.multiple_of(c * tq, tq)
                bias = bias_ref[blk["off"] + c]

                def fn(h):
                    g = h // A_GROUP
                    s = _dot_nt(blk["aq"][0, h], ak_ref[0, g, pl.ds(k0, tq), :]) + bias
                    return s, av_ref[0, g, pl.ds(k0, tq), :]
                return fn

            _attend_chunks(A_HEADS, ma_ref.at[blk["idx"]], acca_ref.at[blk["idx"]],
                           [chunk_fn(c) for c in chunks])

        _for_chunk_groups(blk["nk"], attend)
        _store_heads(blk["oa"], acca_ref.at[blk["idx"]], A_HEADS)
        _store_heads(blk["ob"], accb_ref.at[blk["idx"]], B_HEADS)


def _attn_call(ik, iq, iw, aq, ak, av, qb, kb, vb, positions, tq, topk):
    b, heads, s, _ = aq.shape
    nq = s // tq
    n_pairs = nq // 2
    assert s % tq == 0 and nq % 2 == 0
    assert nq + 1 == (_COARSE_PASSES + _FINE_PASSES - 1) // _PASSES_PER_CHUNK + 1
    assert (nq + 1) * tq // (2 * SUBLANES) <= 256
    pos_col = positions[:, :, None]
    pos_row = positions[:, None, :]

    def per_seq(*shape):
        nd = len(shape)
        return pl.BlockSpec((1,) + shape, lambda i, j: (i,) + (0,) * nd)

    blk_index = (lambda j: j, lambda j: nq - 1 - j)

    def qhead(h, w, which):
        return pl.BlockSpec((1, h, tq, w), lambda i, j: (i, 0, blk_index[which](j), 0))

    def qrows(w, which):
        return pl.BlockSpec((1, tq, w), lambda i, j: (i, blk_index[which](j), 0))

    def qlanes(which):
        return pl.BlockSpec((1, 1, tq), lambda i, j: (i, 0, blk_index[which](j)))

    in_specs = [
        per_seq(s, IDX_DIM), per_seq(A_KV_HEADS, s, A_HEAD_DIM), per_seq(A_KV_HEADS, s, LANES),
        per_seq(B_HEADS, s, LANES), per_seq(B_HEADS, s, LANES),
        per_seq(s, 1), per_seq(1, s),
        qhead(IDX_HEADS, IDX_DIM, 0), qhead(IDX_HEADS, IDX_DIM, 1),
        qrows(LANES, 0), qrows(LANES, 1),
        qhead(A_HEADS, A_HEAD_DIM, 0), qhead(A_HEADS, A_HEAD_DIM, 1),
        qhead(B_HEADS, LANES, 0), qhead(B_HEADS, LANES, 1),
        qlanes(0), qlanes(1), qrows(1, 0), qrows(1, 1),
    ]
    width = heads * A_HEAD_DIM
    out_sds = jax.ShapeDtypeStruct((b, s // 2, width), BF16)
    out_lo = pl.BlockSpec((1, tq, width), lambda i, j: (i, j, 0))
    out_hi = pl.BlockSpec((1, tq, width), lambda i, j: (i, n_pairs - 1 - j, 0))
    state = pltpu.VMEM((2, heads, tq, LANES), F32)
    return pl.pallas_call(
        functools.partial(_attn_kernel, topk=topk, nq=nq),
        grid=(b, n_pairs),
        in_specs=in_specs,
        out_specs=[out_lo, out_hi, out_lo, out_hi],
        out_shape=[out_sds] * 4,
        scratch_shapes=[
            pltpu.VMEM((nq + 1, tq, tq), F32),
            pltpu.VMEM((nq + 1, tq, tq), BF16),
            pltpu.VMEM((nq + 1, tq, tq), F32),
            pltpu.VMEM((2, IDX_HEADS, tq), F32),
            state, state, state, state,
        ],
        compiler_params=pltpu.CompilerParams(
            dimension_semantics=("arbitrary", "arbitrary"),
            vmem_limit_bytes=VMEM_LIMIT_BYTES),
        name="attn",
    )(ik, ak, av, kb, vb, pos_col, pos_row, iq, iq, iw, iw, aq, aq, qb, qb,
      pos_row, pos_row, pos_col, pos_col)


def _shift_rows(u, prev_tail, shift, row8):
    rolled = pltpu.roll(u, shift, axis=0)
    head = jnp.where(row8 < shift, pltpu.roll(prev_tail, shift, axis=0),
                     rolled[:SUBLANES])
    return jnp.concatenate([head, rolled[SUBLANES:]], axis=0)


def _ffn_kernel(x_ref, oa_lo_ref, oa_hi_ref, ob_lo_ref, ob_hi_ref, ga_ref, gb_ref,
                wba_ref, wbb_ref, wout_ref, gffn_ref, wup_ref, cw_ref, cb_ref, wdown_ref,
                gfin_ref, out_ref, tail_ref, *, n_chunks, final_norm):
    j = pl.program_id(1)
    x = x_ref[0]
    tm = x.shape[0]
    first_half = j < pl.num_programs(1) // 2
    oa = jnp.where(first_half, oa_lo_ref[0], oa_hi_ref[0])
    ob = jnp.where(first_half, ob_lo_ref[0], ob_hi_ref[0])
    ya = _dot(oa, wba_ref[...])
    yb = _dot(ob, wbb_ref[...])
    merged = ga_ref[0].astype(F32) * ya + gb_ref[0].astype(F32) * yb
    h1 = x + _dot(merged.astype(BF16), wout_ref[...])
    ms = jnp.mean(h1 * h1, axis=-1, keepdims=True)
    hn = (h1 * lax.rsqrt(ms + EPS) * gffn_ref[...]).astype(BF16)

    fc = wdown_ref.shape[1]
    row8 = lax.broadcasted_iota(jnp.int32, (SUBLANES, fc), 0)

    @pl.when(j == 0)
    def _():
        tail_ref[...] = jnp.zeros_like(tail_ref)

    acc = h1
    for c in range(n_chunks):
        halves = []
        for part in range(2):
            idx = part * n_chunks + c
            u = _dot(hn, wup_ref[:, idx * fc:(idx + 1) * fc])
            prev_tail = tail_ref[idx]
            tail_ref[idx] = u[tm - SUBLANES:]
            y = (cb_ref[idx] + cw_ref[0, idx] * _shift_rows(u, prev_tail, 2, row8)
                 + cw_ref[1, idx] * _shift_rows(u, prev_tail, 1, row8)
                 + cw_ref[2, idx] * u)
            halves.append(y)
        gate, val = halves
        act = (gate * jax.nn.sigmoid(gate) * val).astype(BF16)
        acc = acc + _dot(act, wdown_ref[c])
    if final_norm:
        ms = jnp.mean(acc * acc, axis=-1, keepdims=True)
        acc = acc * lax.rsqrt(ms + EPS) * gfin_ref[...]
    out_ref[0] = acc


def _ffn_call(x, oa_lo, oa_hi, ob_lo, ob_hi, ga, gb, wba, wbb, wout, gffn, wup, cw, cb,
              wdown, gfin, tm, final_norm):
    b, s, d = x.shape
    n_chunks, fc, _ = wdown.shape
    n_parts = 2 * n_chunks
    assert wup.shape == (d, n_parts * fc) and fc % LANES == 0
    n_tiles = s // tm
    assert s % tm == 0 and n_tiles % 2 == 0
    n_half = n_tiles // 2
    tok = lambda w: pl.BlockSpec((1, tm, w), lambda i, j: (i, j, 0))
    lo = lambda w: pl.BlockSpec((1, tm, w), lambda i, j: (i, jnp.minimum(j, n_half - 1), 0))
    hi = lambda w: pl.BlockSpec((1, tm, w), lambda i, j: (i, jnp.maximum(j - n_half, 0), 0))
    wa, wb = oa_lo.shape[-1], ob_lo.shape[-1]
    weights = (wba, wbb, wout, gffn, wup, cw, cb, wdown, gfin)
    return pl.pallas_call(
        functools.partial(_ffn_kernel, n_chunks=n_chunks, final_norm=final_norm),
        grid=(b, n_tiles),
        in_specs=[tok(d), lo(wa), hi(wa), lo(wb), hi(wb), tok(d), tok(d)]
        + [_const_spec(w.shape) for w in weights],
        out_specs=tok(d),
        out_shape=jax.ShapeDtypeStruct((b, s, d), x.dtype),
        scratch_shapes=[pltpu.VMEM((n_parts, SUBLANES, fc), F32)],
        compiler_params=pltpu.CompilerParams(
            dimension_semantics=("arbitrary", "arbitrary"),
            vmem_limit_bytes=VMEM_LIMIT_BYTES),
        name="ffn",
    )(x, oa_lo, oa_hi, ob_lo, ob_hi, ga, gb, *weights)


def _prepare_ffn_weights(w_branch_a, w_branch_b, w_out, norm_ffn_g, w_up, conv_w,
                         conv_b, w_down, norm_final_g, n_chunks):
    d, two_f = w_up.shape
    fc = two_f // (2 * n_chunks)
    wup = w_up.astype(BF16)
    cw = conv_w.reshape(CONV_WIDTH, 2 * n_chunks, 1, fc)
    cb = conv_b.reshape(2 * n_chunks, 1, fc)
    wdown = w_down.reshape(n_chunks, fc, d).astype(BF16)
    return (w_branch_a.astype(BF16), w_branch_b.astype(BF16), w_out.astype(BF16),
            norm_ffn_g.reshape(1, -1), wup, cw, cb, wdown, norm_final_g.reshape(1, -1))


TM_PROJ = 512
TQ_ATTN = 256
TM_FFN = 512
FFN_CHUNKS = 1


def kernel(x, positions, norm_mix_g, w_in, idx_k_norm_g, q_a_norm_g, kv_a_norm_g, w_uq, w_uk, w_uv, w_branch_a, w_branch_b, w_out, norm_ffn_g, w_up, conv_w, conv_b, w_down, norm_final_g):
    b, s, _ = x.shape
    depth = w_in.shape[0]
    topk = min(INDEX_TOPK_MAX, s // 4)
    trig = _rope_trig(positions)
    expand = _rope_expansion()
    h = x
    for l in range(depth):
        pw = _prepare_proj_weights(norm_mix_g[l], w_in[l], idx_k_norm_g[l], q_a_norm_g[l],
                                   kv_a_norm_g[l], w_uq[l], w_uk[l], w_uv[l])
        aq, ak, av, iq, ik, iw, qb, kb, vb, ga, gb = _proj_call(h, trig, expand, *pw, tm=TM_PROJ)
        mixed = _attn_call(ik, iq, iw, aq, ak, av, qb, kb, vb, positions, TQ_ATTN, topk)
        fw = _prepare_ffn_weights(w_branch_a[l], w_branch_b[l], w_out[l], norm_ffn_g[l],
                                  w_up[l], conv_w[l], conv_b[l], w_down[l], norm_final_g,
                                  FFN_CHUNKS)
        h = _ffn_call(h, *mixed, ga, gb, *fw, tm=TM_FFN, final_norm=(l == depth - 1))
    return h
```

```python
import functools

import jax
import jax.numpy as jnp
import numpy as np
from jax import lax
from jax.experimental import pallas as pl
from jax.experimental.pallas import tpu as pltpu

ROPE_THETA = 500000.0
EPS = 1e-6
A_HEADS = 8
A_KV_HEADS = 2
A_GROUP = A_HEADS // A_KV_HEADS
A_HEAD_DIM = 64
A_ROT_DIM = 16
IDX_HEADS = 8
IDX_DIM = 64
INDEX_TOPK_MAX = 256
B_HEADS = 8
B_Q_RANK = 256
B_KV_RANK = 128
B_NOPE_DIM = 64
B_ROPE_DIM = 32
B_V_DIM = 64
CONV_WIDTH = 3

LANES = 128
SUBLANES = 8
VMEM_LIMIT_BYTES = 56 * 1024 * 1024

_T_IW = IDX_DIM + B_ROPE_DIM
assert IDX_DIM == B_NOPE_DIM and _T_IW + IDX_HEADS <= LANES


def _proj_layout(d_model):
    widths = dict(aq=A_HEADS * A_HEAD_DIM, ak=A_KV_HEADS * A_HEAD_DIM,
                  av=A_KV_HEADS * A_HEAD_DIM, iq=IDX_HEADS * IDX_DIM, cq=B_Q_RANK,
                  ckv=B_KV_RANK, ga=d_model, gb=d_model, tail=LANES)
    offsets, total = {}, 0
    for name, width in widths.items():
        assert width % LANES == 0
        offsets[name] = total
        total += width
    return offsets, total

F32 = jnp.float32
BF16 = jnp.bfloat16
NEG_INF = float("-inf")
LOG2_E = 1.4426950408889634
F32_LOWEST = float(np.finfo(np.float32).min)


def _dot(a, b):
    return jnp.dot(a, b, preferred_element_type=F32)


def _dot_nt(a, b):
    return lax.dot_general(a, b, (((1,), (1,)), ((), ())), preferred_element_type=F32)


def _rope_lanes(x, cos_t, sin_t, take_up, half):
    up = pltpu.roll(x, LANES - half, axis=1)
    dn = pltpu.roll(x, half, axis=1)
    return x * cos_t + jnp.where(take_up, up, dn) * sin_t


def _expand_rope_tables(trig_t, expand_ref):
    tm = trig_t[0].shape[1]
    used = sum(piece.shape[0] for piece in trig_t)
    rest = lax.broadcasted_iota(jnp.int32, (LANES - used, tm), 0)
    ones_then_zeros = jnp.where(rest == 0, 1.0, 0.0)
    t = jnp.concatenate(list(trig_t) + [ones_then_zeros], axis=0).T
    a1 = t.astype(BF16)
    r1 = t - a1.astype(F32)
    a2 = r1.astype(BF16)
    a3 = (r1 - a2.astype(F32)).astype(BF16)
    tables = _dot(jnp.concatenate([a1, a2, a3], axis=1), expand_ref[...])
    return [tables[:, i * LANES:(i + 1) * LANES] for i in range(4)]


def _proj_kernel(x_ref, ca_ref, sa_ref, cb_ref, sb_ref, expand_ref, gmix_ref, w_ref,
                 gik_ref, gq_ref, gkv_ref, wuq_ref, wuk_ref, wuv_ref,
                 aq_ref, ak_ref, av_ref, iq_ref, ik_ref, iw_ref, qb_ref, kb_ref,
                 vb_ref, ga_ref, gb_ref, *, offs):
    x = x_ref[0]
    tm, d_model = x.shape
    ms = jnp.mean(x * x, axis=-1, keepdims=True)
    hn = (x * lax.rsqrt(ms + EPS) * gmix_ref[...]).astype(BF16)

    lane = lax.broadcasted_iota(jnp.int32, (tm, LANES), 1)
    ca, sa, cb, sb = _expand_rope_tables(
        (ca_ref[0], sa_ref[0], cb_ref[0], sb_ref[0]), expand_ref)
    up_a = (lane % A_HEAD_DIM) < (A_ROT_DIM // 2)
    up_b = lane < (B_NOPE_DIM + B_ROPE_DIM // 2)
    v_one = jnp.where(lane == A_HEAD_DIM, 1.0, 0.0).astype(F32)
    low_half = lane < A_HEAD_DIM

    def proj(off, width):
        return _dot(hn, w_ref[:, off:off + width])

    def store_heads(ref, tile128, first_head, scale):
        t = tile128 * scale if scale != 1.0 else tile128
        ref[0, first_head] = t[:, :A_HEAD_DIM].astype(ref.dtype)
        ref[0, first_head + 1] = t[:, A_HEAD_DIM:].astype(ref.dtype)

    assert A_HEADS * A_HEAD_DIM == IDX_HEADS * IDX_DIM and A_HEAD_DIM == IDX_DIM
    n_head_pairs = A_HEADS // 2
    for off, ref, scale in ((offs["aq"], aq_ref, A_HEAD_DIM ** -0.5 * LOG2_E),
                            (offs["iq"], iq_ref, IDX_DIM ** -0.5)):
        full = proj(off, n_head_pairs * LANES)
        for j in range(n_head_pairs):
            t = _rope_lanes(full[:, j * LANES:(j + 1) * LANES], ca, sa, up_a,
                            A_ROT_DIM // 2)
            store_heads(ref, t, 2 * j, scale)

    assert A_KV_HEADS * A_HEAD_DIM == LANES and offs["av"] == offs["ak"] + LANES
    akv = proj(offs["ak"], 2 * LANES)
    store_heads(ak_ref, _rope_lanes(akv[:, :LANES], ca, sa, up_a, A_ROT_DIM // 2),
                0, 1.0)
    v2 = akv[:, LANES:]
    av_ref[0, 0] = (jnp.where(low_half, v2, 0.0) + v_one).astype(BF16)
    v2r = pltpu.roll(v2, A_HEAD_DIM, axis=1)
    av_ref[0, 1] = (jnp.where(low_half, v2r, 0.0) + v_one).astype(BF16)

    tail = proj(offs["tail"], LANES)
    ik_sq = jnp.where(low_half, tail * tail, 0.0)
    ik_ms = jnp.sum(ik_sq, axis=-1, keepdims=True) * (1.0 / IDX_DIM)
    ik_n = tail * lax.rsqrt(ik_ms + EPS) * gik_ref[...]
    ik_r = _rope_lanes(ik_n, ca, sa, up_a, A_ROT_DIM // 2)
    ik_ref[0] = ik_r[:, :IDX_DIM].astype(BF16)
    iw_ref[0] = tail * (IDX_HEADS ** -0.5)
    kr = _rope_lanes(tail, cb, sb, up_b, B_ROPE_DIM // 2)
    kr = jnp.where((lane >= B_NOPE_DIM) & (lane < B_NOPE_DIM + B_ROPE_DIM), kr, 0.0)

    cq = proj(offs["cq"], B_Q_RANK)
    cq_n = (cq * lax.rsqrt(jnp.mean(cq * cq, axis=-1, keepdims=True) + EPS)
            * gq_ref[...]).astype(BF16)
    ckv = proj(offs["ckv"], B_KV_RANK)
    ckv_n = (ckv * lax.rsqrt(jnp.mean(ckv * ckv, axis=-1, keepdims=True) + EPS)
             * gkv_ref[...]).astype(BF16)
    b_scale = (B_NOPE_DIM + B_ROPE_DIM) ** -0.5 * LOG2_E
    for h0 in range(0, B_HEADS, 2):
        cols = slice(h0 * LANES, (h0 + 2) * LANES)
        q2 = _dot(cq_n, wuq_ref[:, cols])
        k2 = _dot(ckv_n, wuk_ref[:, cols])
        v2 = _dot(ckv_n, wuv_ref[:, cols])
        for i in range(2):
            lanes = slice(i * LANES, (i + 1) * LANES)
            q = _rope_lanes(q2[:, lanes], cb, sb, up_b, B_ROPE_DIM // 2) * b_scale
            qb_ref[0, h0 + i] = q.astype(BF16)
            kb_ref[0, h0 + i] = (k2[:, lanes] + kr).astype(BF16)
            vb_ref[0, h0 + i] = (v2[:, lanes] + v_one).astype(BF16)

    ga_ref[0] = jax.nn.sigmoid(proj(offs["ga"], d_model)).astype(BF16)
    gb_ref[0] = jax.nn.sigmoid(proj(offs["gb"], d_model)).astype(BF16)


_R_CA = 0
_R_SA = _R_CA + A_ROT_DIM // 2
_R_CB = _R_SA + A_ROT_DIM // 2
_R_SB = _R_CB + B_ROPE_DIM // 2
_R_ONE = _R_SB + B_ROPE_DIM // 2


def _rope_trig(positions):
    pos = positions.astype(F32)[:, None, :]

    def cs(rot):
        inv = ROPE_THETA ** (-jnp.arange(0, rot, 2, dtype=F32) / rot)
        ang = pos * inv[None, :, None]
        return jnp.cos(ang), jnp.sin(ang)

    return cs(A_ROT_DIM) + cs(B_ROPE_DIM)


def _rope_expansion():
    e = np.zeros((LANES, 4 * LANES), np.float32)
    half_a, half_b = A_ROT_DIM // 2, B_ROPE_DIM // 2
    for j in range(LANES):
        d = j % A_HEAD_DIM
        if d < A_ROT_DIM:
            e[_R_CA + d % half_a, j] = 1.0
            e[_R_SA + d % half_a, LANES + j] = -1.0 if d < half_a else 1.0
        else:
            e[_R_ONE, j] = 1.0
        d = j - B_NOPE_DIM
        if 0 <= d < B_ROPE_DIM:
            e[_R_CB + d % half_b, 2 * LANES + j] = 1.0
            e[_R_SB + d % half_b, 3 * LANES + j] = -1.0 if d < half_b else 1.0
        else:
            e[_R_ONE, 2 * LANES + j] = 1.0
    return jnp.asarray(np.concatenate([e, e, e], axis=0), BF16)


def _regroup_w_in(w_in):
    d_model = w_in.shape[0]
    a_kv = A_KV_HEADS * A_HEAD_DIM
    splits = [A_HEADS * A_HEAD_DIM, a_kv, a_kv, IDX_HEADS * IDX_DIM, IDX_DIM, IDX_HEADS,
              B_Q_RANK, B_KV_RANK, B_ROPE_DIM, d_model, d_model]
    assert sum(splits) == w_in.shape[1]
    offs = np.concatenate([[0], np.cumsum(splits)])
    seg = [w_in[:, offs[i]:offs[i + 1]] for i in range(len(splits))]
    a_q, a_k, a_v, i_q, i_k, i_w, b_cq, b_ckv, b_kr, g_a, g_b = seg
    pad = jnp.zeros((d_model, LANES - _T_IW - IDX_HEADS), w_in.dtype)
    return jnp.concatenate(
        [a_q, a_k, a_v, i_q, b_cq, b_ckv, g_a, g_b, i_k, b_kr, i_w, pad], axis=1)


def _pad_heads(w, width):
    r, h, d = w.shape
    return jnp.pad(w, ((0, 0), (0, 0), (0, width - d))).reshape(r, h * width)


def _const_spec(shape):
    nd = len(shape)
    return pl.BlockSpec(shape, lambda *_: (0,) * nd)


def _proj_call(x, trig, expand, gmix, wcat, gik, gq, gkv, wuq, wuk, wuv, tm):
    b, s, d = x.shape
    grid = (b, s // tm)
    offs, total = _proj_layout(d)
    assert wcat.shape == (d, total)
    tok = lambda w: pl.BlockSpec((1, tm, w), lambda i, j: (i, j, 0))
    head = lambda h, w: pl.BlockSpec((1, h, tm, w), lambda i, j: (i, 0, j, 0))
    seq_on_lanes = lambda t: pl.BlockSpec((1, t.shape[1], tm), lambda i, j: (i, 0, j))
    consts = (expand, gmix, wcat, gik, gq, gkv, wuq, wuk, wuv)
    in_specs = ([tok(d)] + [seq_on_lanes(t) for t in trig]
                + [_const_spec(c.shape) for c in consts])
    sds = jax.ShapeDtypeStruct
    out_shape = [
        sds((b, A_HEADS, s, A_HEAD_DIM), BF16),
        sds((b, A_KV_HEADS, s, A_HEAD_DIM), BF16),
        sds((b, A_KV_HEADS, s, LANES), BF16),
        sds((b, IDX_HEADS, s, IDX_DIM), BF16),
        sds((b, s, IDX_DIM), BF16),
        sds((b, s, LANES), F32),
        sds((b, B_HEADS, s, LANES), BF16),
        sds((b, B_HEADS, s, LANES), BF16),
        sds((b, B_HEADS, s, LANES), BF16),
        sds((b, s, d), BF16),
        sds((b, s, d), BF16),
    ]
    out_specs = [
        head(A_HEADS, A_HEAD_DIM), head(A_KV_HEADS, A_HEAD_DIM),
        head(A_KV_HEADS, LANES), head(IDX_HEADS, IDX_DIM), tok(IDX_DIM),
        tok(LANES), head(B_HEADS, LANES), head(B_HEADS, LANES),
        head(B_HEADS, LANES), tok(d), tok(d)]
    return pl.pallas_call(
        functools.partial(_proj_kernel, offs=offs),
        grid=grid,
        in_specs=in_specs,
        out_specs=out_specs,
        out_shape=out_shape,
        compiler_params=pltpu.CompilerParams(
            dimension_semantics=("arbitrary", "arbitrary"),
            vmem_limit_bytes=VMEM_LIMIT_BYTES),
        name="proj",
    )(x, *trig, *consts)

def _prepare_proj_weights(norm_mix_g, w_in, idx_k_norm_g, q_a_norm_g, kv_a_norm_g,
                          w_uq, w_uk, w_uv):
    gmix = norm_mix_g.reshape(1, -1)
    wcat = _regroup_w_in(w_in).astype(BF16)
    gik = jnp.pad(idx_k_norm_g, (0, LANES - IDX_DIM)).reshape(1, LANES)
    gq = q_a_norm_g.reshape(1, -1)
    gkv = kv_a_norm_g.reshape(1, -1)
    wuq = _pad_heads(w_uq, LANES).astype(BF16)
    wuk = _pad_heads(w_uk, LANES).astype(BF16)
    wuv = _pad_heads(w_uv, LANES).astype(BF16)
    return gmix, wcat, gik, gq, gkv, wuq, wuk, wuv


_INT_MIN = -2 ** 31
_ORD_NEG_INF = 0x007FFFFF
_ORD_LOWEST_FINITE = 0x00800000


def _ordinal_to_float(u):
    o = u ^ jnp.int32(_INT_MIN)
    bits = o ^ ((o >> 31) & jnp.int32(0x7FFFFFFF))
    return lax.bitcast_convert_type(bits, F32)


def _softmax_update(s, v, m_old, acc, rows_may_be_empty=True):
    m_new = jnp.maximum(m_old, jnp.max(s, axis=-1, keepdims=True))
    if rows_may_be_empty:
        m_new = jnp.maximum(m_new, F32_LOWEST)
    p = jnp.concatenate(
        [jnp.exp2(s[:, j * LANES:(j + 1) * LANES] - m_new)
         for j in range(s.shape[1] // LANES)], axis=1)
    alpha = jnp.exp2(m_old - m_new)
    return m_new, alpha * acc + _dot(p.astype(BF16), v)


def _attend_chunks(n_heads, m_ref, acc_ref, chunk_fns):
    for h in range(n_heads):
        m, acc = m_ref[h], acc_ref[h]
        for fn in chunk_fns:
            m, acc = _softmax_update(*fn(h), m, acc)
        m_ref[h] = m
        acc_ref[h] = acc


def _for_chunk_groups(n, body):
    def group(i, carry):
        body(tuple(4 * i + r for r in range(4)))
        return carry

    lax.fori_loop(0, lax.shift_right_logical(n, 2), group, 0)
    done = n & ~3

    @pl.when((n & 2) != 0)
    def _():
        body((done, done + 1))

    @pl.when((n & 1) != 0)
    def _():
        body((n - 1,))


def _init_softmax_state(m_ref, acc_ref):
    m_ref[...] = jnp.full(m_ref.shape, NEG_INF, F32)
    acc_ref[...] = jnp.zeros(acc_ref.shape, F32)


def _finish_heads(acc_even, acc_odd, lane):
    o0 = acc_even * (1.0 / acc_even[:, B_V_DIM:B_V_DIM + 1])
    o1 = acc_odd * (1.0 / acc_odd[:, B_V_DIM:B_V_DIM + 1])
    return jnp.where(lane < B_V_DIM, o0, pltpu.roll(o1, B_V_DIM, axis=1))


def _store_heads(o_ref, acc_ref, n_heads):
    lane = lax.broadcasted_iota(jnp.int32, acc_ref.shape[1:], 1)
    for pair in range(n_heads // 2):
        o_ref[0, :, pair * LANES:(pair + 1) * LANES] = _finish_heads(
            acc_ref[2 * pair], acc_ref[2 * pair + 1], lane).astype(o_ref.dtype)


_COARSE_PASSES = 16
_FINE_PASSES = 17
_PASSES_PER_CHUNK = 4


def _attn_kernel(ik_ref, ak_ref, av_ref, kb_ref, vb_ref, posk_col_ref, posk_row_ref,
                 iq0_ref, iq1_ref, iw0_ref, iw1_ref, aq0_ref, aq1_ref, qb0_ref, qb1_ref,
                 pqr0_ref, pqr1_ref, pqc0_ref, pqc1_ref,
                 oa0_ref, oa1_ref, ob0_ref, ob1_ref,
                 isc_ref, xb_ref, bias_ref, wt_ref, ma_ref, acca_ref, mb_ref, accb_ref,
                 *, topk, nq):
    pi = pl.program_id(1)
    tq = aq0_ref.shape[2]
    half = tq // 2
    nk0 = pi + 1
    n_virtual = nq + 1
    blocks = (
        dict(idx=0, nk=nk0, off=0, iq=iq0_ref, iw=iw0_ref, aq=aq0_ref, pqr=pqr0_ref,
             oa=oa0_ref, ob=ob0_ref),
        dict(idx=1, nk=nq - pi, off=nk0, iq=iq1_ref, iw=iw1_ref, aq=aq1_ref, pqr=pqr1_ref,
             oa=oa1_ref, ob=ob1_ref),
    )

    for blk in blocks:
        wt_ref[blk["idx"]] = blk["iw"][0].T[_T_IW:_T_IW + IDX_HEADS, :]

    def idx_chunk(v):
        in1 = v >= nk0
        c = v - jnp.where(in1, nk0, 0)
        posq = jnp.where(in1, pqr1_ref[0], pqr0_ref[0])
        wt = jnp.where(in1, wt_ref[1], wt_ref[0])
        iqs = [jnp.where(in1, iq1_ref[0, h], iq0_ref[0, h]) for h in range(IDX_HEADS)]
        for sub in range(2):
            k0 = pl.multiple_of(c * tq + sub * half, half)
            ikc = ik_ref[0, pl.ds(k0, half), :]
            acc = jnp.zeros((half, tq), F32)
            for h in range(IDX_HEADS):
                d = _dot_nt(ikc, iqs[h])
                acc = acc + wt[h:h + 1, :] * jnp.maximum(d, 0.0)
            causal = posk_col_ref[0, pl.ds(k0, half), :] <= posq
            score = jnp.where(causal, acc, NEG_INF)
            isc_ref[v, sub * half:(sub + 1) * half, :] = score
            xb_ref[v, sub * half:(sub + 1) * half, :] = score.astype(BF16)

    for v in range(n_virtual):
        idx_chunk(v)

    _init_softmax_state(mb_ref, accb_ref)

    def latent_chunk(v, diagonal_of=None):
        if diagonal_of is not None:
            in1 = diagonal_of == 1
            c = (nq - 1 - pi) if in1 else (nk0 - 1)
            b = diagonal_of
            q_ref, pqc_ref = (qb1_ref, pqc1_ref) if in1 else (qb0_ref, pqc0_ref)
        else:
            in1 = v >= nk0
            c = jnp.where(in1, v - nk0, v - 1)
            b = in1.astype(jnp.int32)
        k0 = pl.multiple_of(c * tq, tq)
        if diagonal_of is not None:
            causal = posk_row_ref[0, :, pl.ds(k0, tq)] <= pqc_ref[0]
        for h in range(B_HEADS):
            if diagonal_of is not None:
                q = q_ref[0, h]
            else:
                q = jnp.where(in1, qb1_ref[0, h], qb0_ref[0, h])
            s = _dot_nt(q, kb_ref[0, h, pl.ds(k0, tq), :])
            if diagonal_of is not None:
                s = jnp.where(causal, s, NEG_INF)
            m, acc = _softmax_update(s, vb_ref[0, h, pl.ds(k0, tq), :],
                                     mb_ref[b, h], accb_ref[b, h], rows_may_be_empty=False)
            mb_ref[b, h] = m
            accb_ref[b, h] = acc

    packed = 2 * SUBLANES
    one_b, zero_b = jnp.asarray(1, BF16), jnp.asarray(0, BF16)

    def count_ge(thr0, thr1, coarse):
        rows = packed if coarse else SUBLANES
        dtype = BF16 if coarse else F32
        acc0 = jnp.zeros((rows, tq), dtype)
        acc1 = jnp.zeros((rows, tq), dtype)
        zero = jnp.zeros((rows, tq), dtype)
        for v in range(n_virtual):
            in1 = v >= nk0
            thr = jnp.where(in1, thr1, thr0)
            if coarse:
                hit = jnp.where(xb_ref[v] >= thr, one_b, zero_b)
            else:
                hit = jnp.where(isc_ref[v] >= thr, 1.0, 0.0)
            part = hit[:rows]
            for r in range(1, tq // rows):
                part = part + hit[r * rows:(r + 1) * rows]
            acc0 = acc0 + jnp.where(in1, zero, part)
            acc1 = acc1 + jnp.where(in1, part, zero)
        return (jnp.sum(acc0.astype(F32), axis=0, keepdims=True),
                jnp.sum(acc1.astype(F32), axis=0, keepdims=True))

    def coarse_ordinal(j):
        low = jnp.where(j < 0x8000, jnp.int32(0xFFFF), jnp.int32(0))
        return lax.shift_left(j, 16) | low

    def coarse_pass(it, js):
        cands = [j | lax.shift_left(jnp.int32(1), 15 - it) for j in js]
        thrs = [_ordinal_to_float(coarse_ordinal(c)).astype(BF16) for c in cands]
        cnts = count_ge(*thrs, coarse=True)
        return tuple(jnp.where(n >= topk, c, j) for n, c, j in zip(cnts, cands, js))

    def fine_pass(it, ds, bases):
        cands = [d | lax.shift_left(jnp.int32(1), 16 - it) for d in ds]
        thrs = [_ordinal_to_float(b + c) for b, c in zip(bases, cands)]
        cnts = count_ge(*thrs, coarse=False)
        return tuple(jnp.where(n >= topk, c, d) for n, c, d in zip(cnts, cands, ds))

    zeros = (jnp.zeros((1, tq), jnp.int32),) * 2
    first_fine = _COARSE_PASSES // _PASSES_PER_CHUNK

    def coarse_body(v, js, diagonal_of=None):
        latent_chunk(v, diagonal_of)
        for r in range(_PASSES_PER_CHUNK):
            js = coarse_pass(_PASSES_PER_CHUNK * v + r, js)
        return js

    js = lax.fori_loop(1, first_fine, coarse_body, coarse_body(0, zeros, diagonal_of=0))
    js = [jnp.maximum(j, _ORD_NEG_INF >> 16) for j in js]
    bases = [coarse_ordinal(j) - jnp.int32(1 << 16) for j in js]

    def fine_body(v, ds):
        latent_chunk(v)
        for r in range(_PASSES_PER_CHUNK):
            ds = fine_pass(_PASSES_PER_CHUNK * (v - first_fine) + r, ds, bases)
        return ds

    ds = lax.fori_loop(first_fine, n_virtual - 1, fine_body, zeros)
    latent_chunk(n_virtual - 1, diagonal_of=1)
    ds = fine_pass(_FINE_PASSES - 1, ds, bases)

    t_us = [base + d for base, d in zip(bases, ds)]
    shorts = [(t_u >= 0) & (t_u < _ORD_LOWEST_FINITE) for t_u in t_us]
    thrs = [_ordinal_to_float(jnp.where(short, jnp.int32(_ORD_LOWEST_FINITE), t_u))
            for short, t_u in zip(shorts, t_us)]
    n_ges = [jnp.zeros((1, tq), F32)] * 2
    for v in range(n_virtual):
        in1 = v >= nk0
        sel = isc_ref[v] >= jnp.where(in1, thrs[1], thrs[0])
        bias_ref[v] = jnp.where(sel, 0.0, NEG_INF).T
        cnt = jnp.sum(jnp.where(sel, 1.0, 0.0), axis=0, keepdims=True)
        n_ges = [n_ges[0] + jnp.where(in1, 0.0, cnt), n_ges[1] + jnp.where(in1, cnt, 0.0)]

    for blk, thr, short, n_ge in zip(blocks, thrs, shorts, n_ges):
        off, nk = blk["off"], blk["nk"]
        tied = jnp.logical_and(n_ge > topk, jnp.logical_not(short))

        @pl.when(jnp.max(jnp.where(tied, 1.0, 0.0)) > 0.0)
        def _(thr=thr, off=off, nk=nk):
            def count_gt(c, acc):
                hit = jnp.where(isc_ref[off + c] > thr, 1.0, 0.0)
                return acc + jnp.sum(hit, axis=0, keepdims=True)
            n_gt = lax.fori_loop(0, nk, count_gt, jnp.zeros((1, tq), F32))
            need = topk - n_gt
            row = lax.broadcasted_iota(jnp.int32, (tq, tq), 0)
            col = lax.broadcasted_iota(jnp.int32, (tq, tq), 1)
            below = jnp.where(col < row, 1.0, 0.0).astype(BF16)

            def fix(c, seen):
                x = isc_ref[off + c]
                eq = jnp.where(x == thr, 1.0, 0.0)
                rank = _dot(below, eq.astype(BF16)) + seen
                sel = (x > thr) | ((x == thr) & (rank < need))
                bias_ref[off + c] = jnp.where(sel, 0.0, NEG_INF).T
                return seen + jnp.sum(eq, axis=0, keepdims=True)

            lax.fori_loop(0, nk, fix, jnp.zeros((1, tq), F32))

    _init_softmax_state(ma_ref, acca_ref)
    for blk in blocks:
        def attend(chunks, blk=blk):
            def chunk_fn(c):
                k0 = pl.multiple_of(c * tq, tq)
                bias = bias_ref[blk["off"] + c]

                def fn(h):
                    g = h // A_GROUP
                    s = _dot_nt(blk["aq"][0, h], ak_ref[0, g, pl.ds(k0, tq), :]) + bias
                    return s, av_ref[0, g, pl.ds(k0, tq), :]
                return fn

            _attend_chunks(A_HEADS, ma_ref.at[blk["idx"]], acca_ref.at[blk["idx"]],
                           [chunk_fn(c) for c in chunks])

        _for_chunk_groups(blk["nk"], attend)
        _store_heads(blk["oa"], acca_ref.at[blk["idx"]], A_HEADS)
        _store_heads(blk["ob"], accb_ref.at[blk["idx"]], B_HEADS)


def _attn_call(ik, iq, iw, aq, ak, av, qb, kb, vb, positions, tq, topk):
    b, heads, s, _ = aq.shape
    nq = s // tq
    n_pairs = nq // 2
    assert s % tq == 0 and nq % 2 == 0
    assert nq + 1 == (_COARSE_PASSES + _FINE_PASSES - 1) // _PASSES_PER_CHUNK + 1
    assert (nq + 1) * tq // (2 * SUBLANES) <= 256
    pos_col = positions[:, :, None]
    pos_row = positions[:, None, :]

    def per_seq(*shape):
        nd = len(shape)
        return pl.BlockSpec((1,) + shape, lambda i, j: (i,) + (0,) * nd)

    blk_index = (lambda j: j, lambda j: nq - 1 - j)

    def qhead(h, w, which):
        return pl.BlockSpec((1, h, tq, w), lambda i, j: (i, 0, blk_index[which](j), 0))

    def qrows(w, which):
        return pl.BlockSpec((1, tq, w), lambda i, j: (i, blk_index[which](j), 0))

    def qlanes(which):
        return pl.BlockSpec((1, 1, tq), lambda i, j: (i, 0, blk_index[which](j)))

    in_specs = [
        per_seq(s, IDX_DIM), per_seq(A_KV_HEADS, s, A_HEAD_DIM), per_seq(A_KV_HEADS, s, LANES),
        per_seq(B_HEADS, s, LANES), per_seq(B_HEADS, s, LANES),
        per_seq(s, 1), per_seq(1, s),
        qhead(IDX_HEADS, IDX_DIM, 0), qhead(IDX_HEADS, IDX_DIM, 1),
        qrows(LANES, 0), qrows(LANES, 1),
        qhead(A_HEADS, A_HEAD_DIM, 0), qhead(A_HEADS, A_HEAD_DIM, 1),
        qhead(B_HEADS, LANES, 0), qhead(B_HEADS, LANES, 1),
        qlanes(0), qlanes(1), qrows(1, 0), qrows(1, 1),
    ]
    width = heads * A_HEAD_DIM
    out_sds = jax.ShapeDtypeStruct((b, s // 2, width), BF16)
    out_lo = pl.BlockSpec((1, tq, width), lambda i, j: (i, j, 0))
    out_hi = pl.BlockSpec((1, tq, width), lambda i, j: (i, n_pairs - 1 - j, 0))
    state = pltpu.VMEM((2, heads, tq, LANES), F32)
    return pl.pallas_call(
        functools.partial(_attn_kernel, topk=topk, nq=nq),
        grid=(b, n_pairs),
        in_specs=in_specs,
        out_specs=[out_lo, out_hi, out_lo, out_hi],
        out_shape=[out_sds] * 4,
        scratch_shapes=[
            pltpu.VMEM((nq + 1, tq, tq), F32),
            pltpu.VMEM((nq + 1, tq, tq), BF16),
            pltpu.VMEM((nq + 1, tq, tq), F32),
            pltpu.VMEM((2, IDX_HEADS, tq), F32),
            state, state, state, state,
        ],
        compiler_params=pltpu.CompilerParams(
            dimension_semantics=("arbitrary", "arbitrary"),
            vmem_limit_bytes=VMEM_LIMIT_BYTES),
        name="attn",
    )(ik, ak, av, kb, vb, pos_col, pos_row, iq, iq, iw, iw, aq, aq, qb, qb,
      pos_row, pos_row, pos_col, pos_col)


def _shift_rows(u, prev_tail, shift, row8):
    rolled = pltpu.roll(u, shift, axis=0)
    head = jnp.where(row8 < shift, pltpu.roll(prev_tail, shift, axis=0),
                     rolled[:SUBLANES])
    return jnp.concatenate([head, rolled[SUBLANES:]], axis=0)


def _ffn_kernel(x_ref, oa_lo_ref, oa_hi_ref, ob_lo_ref, ob_hi_ref, ga_ref, gb_ref,
                wba_ref, wbb_ref, wout_ref, gffn_ref, wup_ref, cw_ref, cb_ref, wdown_ref,
                gfin_ref, out_ref, tail_ref, *, n_chunks, final_norm):
    j = pl.program_id(1)
    x = x_ref[0]
    tm = x.shape[0]
    first_half = j < pl.num_programs(1) // 2
    oa = jnp.where(first_half, oa_lo_ref[0], oa_hi_ref[0])
    ob = jnp.where(first_half, ob_lo_ref[0], ob_hi_ref[0])
    ya = _dot(oa, wba_ref[...])
    yb = _dot(ob, wbb_ref[...])
    merged = ga_ref[0].astype(F32) * ya + gb_ref[0].astype(F32) * yb
    h1 = x + _dot(merged.astype(BF16), wout_ref[...])
    ms = jnp.mean(h1 * h1, axis=-1, keepdims=True)
    hn = (h1 * lax.rsqrt(ms + EPS) * gffn_ref[...]).astype(BF16)

    fc = wdown_ref.shape[1]
    row8 = lax.broadcasted_iota(jnp.int32, (SUBLANES, fc), 0)

    @pl.when(j == 0)
    def _():
        tail_ref[...] = jnp.zeros_like(tail_ref)

    acc = h1
    for c in range(n_chunks):
        halves = []
        for part in range(2):
            idx = part * n_chunks + c
            u = _dot(hn, wup_ref[:, idx * fc:(idx + 1) * fc])
            prev_tail = tail_ref[idx]
            tail_ref[idx] = u[tm - SUBLANES:]
            y = (cb_ref[idx] + cw_ref[0, idx] * _shift_rows(u, prev_tail, 2, row8)
                 + cw_ref[1, idx] * _shift_rows(u, prev_tail, 1, row8)
                 + cw_ref[2, idx] * u)
            halves.append(y)
        gate, val = halves
        act = (gate * jax.nn.sigmoid(gate) * val).astype(BF16)
        acc = acc + _dot(act, wdown_ref[c])
    if final_norm:
        ms = jnp.mean(acc * acc, axis=-1, keepdims=True)
        acc = acc * lax.rsqrt(ms + EPS) * gfin_ref[...]
    out_ref[0] = acc


def _ffn_call(x, oa_lo, oa_hi, ob_lo, ob_hi, ga, gb, wba, wbb, wout, gffn, wup, cw, cb,
              wdown, gfin, tm, final_norm):
    b, s, d = x.shape
    n_chunks, fc, _ = wdown.shape
    n_parts = 2 * n_chunks
    assert wup.shape == (d, n_parts * fc) and fc % LANES == 0
    n_tiles = s // tm
    assert s % tm == 0 and n_tiles % 2 == 0
    n_half = n_tiles // 2
    tok = lambda w: pl.BlockSpec((1, tm, w), lambda i, j: (i, j, 0))
    lo = lambda w: pl.BlockSpec((1, tm, w), lambda i, j: (i, jnp.minimum(j, n_half - 1), 0))
    hi = lambda w: pl.BlockSpec((1, tm, w), lambda i, j: (i, jnp.maximum(j - n_half, 0), 0))
    wa, wb = oa_lo.shape[-1], ob_lo.shape[-1]
    weights = (wba, wbb, wout, gffn, wup, cw, cb, wdown, gfin)
    return pl.pallas_call(
        functools.partial(_ffn_kernel, n_chunks=n_chunks, final_norm=final_norm),
        grid=(b, n_tiles),
        in_specs=[tok(d), lo(wa), hi(wa), lo(wb), hi(wb), tok(d), tok(d)]
        + [_const_spec(w.shape) for w in weights],
        out_specs=tok(d),
        out_shape=jax.ShapeDtypeStruct((b, s, d), x.dtype),
        scratch_shapes=[pltpu.VMEM((n_parts, SUBLANES, fc), F32)],
        compiler_params=pltpu.CompilerParams(
            dimension_semantics=("arbitrary", "arbitrary"),
            vmem_limit_bytes=VMEM_LIMIT_BYTES),
        name="ffn",
    )(x, oa_lo, oa_hi, ob_lo, ob_hi, ga, gb, *weights)


def _prepare_ffn_weights(w_branch_a, w_branch_b, w_out, norm_ffn_g, w_up, conv_w,
                         conv_b, w_down, norm_final_g, n_chunks):
    d, two_f = w_up.shape
    fc = two_f // (2 * n_chunks)
    wup = w_up.astype(BF16)
    cw = conv_w.reshape(CONV_WIDTH, 2 * n_chunks, 1, fc)
    cb = conv_b.reshape(2 * n_chunks, 1, fc)
    wdown = w_down.reshape(n_chunks, fc, d).astype(BF16)
    return (w_branch_a.astype(BF16), w_branch_b.astype(BF16), w_out.astype(BF16),
            norm_ffn_g.reshape(1, -1), wup, cw, cb, wdown, norm_final_g.reshape(1, -1))


TM_PROJ = 512
TQ_ATTN = 256
TM_FFN = 512
FFN_CHUNKS = 1


def kernel(x, positions, norm_mix_g, w_in, idx_k_norm_g, q_a_norm_g, kv_a_norm_g, w_uq, w_uk, w_uv, w_branch_a, w_branch_b, w_out, norm_ffn_g, w_up, conv_w, conv_b, w_down, norm_final_g):
    b, s, _ = x.shape
    depth = w_in.shape[0]
    topk = min(INDEX_TOPK_MAX, s // 4)
    trig = _rope_trig(positions)
    expand = _rope_expansion()
    h = x
    for l in range(depth):
        pw = _prepare_proj_weights(norm_mix_g[l], w_in[l], idx_k_norm_g[l], q_a_norm_g[l],
                                   kv_a_norm_g[l], w_uq[l], w_uk[l], w_uv[l])
        aq, ak, av, iq, ik, iw, qb, kb, vb, ga, gb = _proj_call(h, trig, expand, *pw, tm=TM_PROJ)
        mixed = _attn_call(ik, iq, iw, aq, ak, av, qb, kb, vb, positions, TQ_ATTN, topk)
        fw = _prepare_ffn_weights(w_branch_a[l], w_branch_b[l], w_out[l], norm_ffn_g[l],
                                  w_up[l], conv_w[l], conv_b[l], w_down[l], norm_final_g,
                                  FFN_CHUNKS)
        h = _ffn_call(h, *mixed, ga, gb, *fw, tm=TM_FFN, final_norm=(l == depth - 1))
    return h
```

```python
import functools

import jax
import jax.numpy as jnp
import numpy as np
from jax import lax
from jax.experimental import pallas as pl
from jax.experimental.pallas import tpu as pltpu

ROPE_THETA = 500000.0
EPS = 1e-6
A_HEADS = 8
A_KV_HEADS = 2
A_GROUP = A_HEADS // A_KV_HEADS
A_HEAD_DIM = 64
A_ROT_DIM = 16
IDX_HEADS = 8
IDX_DIM = 64
INDEX_TOPK_MAX = 256
B_HEADS = 8
B_Q_RANK = 256
B_KV_RANK = 128
B_NOPE_DIM = 64
B_ROPE_DIM = 32
B_V_DIM = 64
CONV_WIDTH = 3

LANES = 128
SUBLANES = 8
VMEM_LIMIT_BYTES = 56 * 1024 * 1024

_T_IW = IDX_DIM + B_ROPE_DIM
assert IDX_DIM == B_NOPE_DIM and _T_IW + IDX_HEADS <= LANES


def _proj_layout(d_model):
    widths = dict(aq=A_HEADS * A_HEAD_DIM, ak=A_KV_HEADS * A_HEAD_DIM,
                  av=A_KV_HEADS * A_HEAD_DIM, iq=IDX_HEADS * IDX_DIM, cq=B_Q_RANK,
                  ckv=B_KV_RANK, ga=d_model, gb=d_model, tail=LANES)
    offsets, total = {}, 0
    for name, width in widths.items():
        assert width % LANES == 0
        offsets[name] = total
        total += width
    return offsets, total

F32 = jnp.float32
BF16 = jnp.bfloat16
NEG_INF = float("-inf")
LOG2_E = 1.4426950408889634
F32_LOWEST = float(np.finfo(np.float32).min)


def _dot(a, b):
    return jnp.dot(a, b, preferred_element_type=F32)


def _dot_nt(a, b):
    return lax.dot_general(a, b, (((1,), (1,)), ((), ())), preferred_element_type=F32)


def _rope_lanes(x, cos_t, sin_t, take_up, half):
    up = pltpu.roll(x, LANES - half, axis=1)
    dn = pltpu.roll(x, half, axis=1)
    return x * cos_t + jnp.where(take_up, up, dn) * sin_t


def _expand_rope_tables(trig_t, expand_ref):
    tm = trig_t[0].shape[1]
    used = sum(piece.shape[0] for piece in trig_t)
    rest = lax.broadcasted_iota(jnp.int32, (LANES - used, tm), 0)
    ones_then_zeros = jnp.where(rest == 0, 1.0, 0.0)
    t = jnp.concatenate(list(trig_t) + [ones_then_zeros], axis=0).T
    a1 = t.astype(BF16)
    r1 = t - a1.astype(F32)
    a2 = r1.astype(BF16)
    a3 = (r1 - a2.astype(F32)).astype(BF16)
    tables = _dot(jnp.concatenate([a1, a2, a3], axis=1), expand_ref[...])
    return [tables[:, i * LANES:(i + 1) * LANES] for i in range(4)]


def _proj_kernel(x_ref, ca_ref, sa_ref, cb_ref, sb_ref, expand_ref, gmix_ref, w_ref,
                 gik_ref, gq_ref, gkv_ref, wuq_ref, wuk_ref, wuv_ref,
                 aq_ref, ak_ref, av_ref, iq_ref, ik_ref, iw_ref, qb_ref, kb_ref,
                 vb_ref, ga_ref, gb_ref, *, offs):
    x = x_ref[0]
    tm, d_model = x.shape
    ms = jnp.mean(x * x, axis=-1, keepdims=True)
    hn = (x * lax.rsqrt(ms + EPS) * gmix_ref[...]).astype(BF16)

    lane = lax.broadcasted_iota(jnp.int32, (tm, LANES), 1)
    ca, sa, cb, sb = _expand_rope_tables(
        (ca_ref[0], sa_ref[0], cb_ref[0], sb_ref[0]), expand_ref)
    up_a = (lane % A_HEAD_DIM) < (A_ROT_DIM // 2)
    up_b = lane < (B_NOPE_DIM + B_ROPE_DIM // 2)
    v_one = jnp.where(lane == A_HEAD_DIM, 1.0, 0.0).astype(F32)
    low_half = lane < A_HEAD_DIM

    def proj(off, width):
        return _dot(hn, w_ref[:, off:off + width])

    def store_heads(ref, tile128, first_head, scale):
        t = tile128 * scale if scale != 1.0 else tile128
        ref[0, first_head] = t[:, :A_HEAD_DIM].astype(ref.dtype)
        ref[0, first_head + 1] = t[:, A_HEAD_DIM:].astype(ref.dtype)

    assert A_HEADS * A_HEAD_DIM == IDX_HEADS * IDX_DIM and A_HEAD_DIM == IDX_DIM
    n_head_pairs = A_HEADS // 2
    for off, ref, scale in ((offs["aq"], aq_ref, A_HEAD_DIM ** -0.5 * LOG2_E),
                            (offs["iq"], iq_ref, IDX_DIM ** -0.5)):
        full = proj(off, n_head_pairs * LANES)
        for j in range(n_head_pairs):
            t = _rope_lanes(full[:, j * LANES:(j + 1) * LANES], ca, sa, up_a,
                            A_ROT_DIM // 2)
            store_heads(ref, t, 2 * j, scale)

    assert A_KV_HEADS * A_HEAD_DIM == LANES and offs["av"] == offs["ak"] + LANES
    akv = proj(offs["ak"], 2 * LANES)
    store_heads(ak_ref, _rope_lanes(akv[:, :LANES], ca, sa, up_a, A_ROT_DIM // 2),
                0, 1.0)
    v2 = akv[:, LANES:]
    av_ref[0, 0] = (jnp.where(low_half, v2, 0.0) + v_one).astype(BF16)
    v2r = pltpu.roll(v2, A_HEAD_DIM, axis=1)
    av_ref[0, 1] = (jnp.where(low_half, v2r, 0.0) + v_one).astype(BF16)

    tail = proj(offs["tail"], LANES)
    ik_sq = jnp.where(low_half, tail * tail, 0.0)
    ik_ms = jnp.sum(ik_sq, axis=-1, keepdims=True) * (1.0 / IDX_DIM)
    ik_n = tail * lax.rsqrt(ik_ms + EPS) * gik_ref[...]
    ik_r = _rope_lanes(ik_n, ca, sa, up_a, A_ROT_DIM // 2)
    ik_ref[0] = ik_r[:, :IDX_DIM].astype(BF16)
    iw_ref[0] = tail * (IDX_HEADS ** -0.5)
    kr = _rope_lanes(tail, cb, sb, up_b, B_ROPE_DIM // 2)
    kr = jnp.where((lane >= B_NOPE_DIM) & (lane < B_NOPE_DIM + B_ROPE_DIM), kr, 0.0)

    cq = proj(offs["cq"], B_Q_RANK)
    cq_n = (cq * lax.rsqrt(jnp.mean(cq * cq, axis=-1, keepdims=True) + EPS)
            * gq_ref[...]).astype(BF16)
    ckv = proj(offs["ckv"], B_KV_RANK)
    ckv_n = (ckv * lax.rsqrt(jnp.mean(ckv * ckv, axis=-1, keepdims=True) + EPS)
             * gkv_ref[...]).astype(BF16)
    b_scale = (B_NOPE_DIM + B_ROPE_DIM) ** -0.5 * LOG2_E
    for h0 in range(0, B_HEADS, 2):
        cols = slice(h0 * LANES, (h0 + 2) * LANES)
        q2 = _dot(cq_n, wuq_ref[:, cols])
        k2 = _dot(ckv_n, wuk_ref[:, cols])
        v2 = _dot(ckv_n, wuv_ref[:, cols])
        for i in range(2):
            lanes = slice(i * LANES, (i + 1) * LANES)
            q = _rope_lanes(q2[:, lanes], cb, sb, up_b, B_ROPE_DIM // 2) * b_scale
            qb_ref[0, h0 + i] = q.astype(BF16)
            kb_ref[0, h0 + i] = (k2[:, lanes] + kr).astype(BF16)
            vb_ref[0, h0 + i] = (v2[:, lanes] + v_one).astype(BF16)

    ga_ref[0] = jax.nn.sigmoid(proj(offs["ga"], d_model)).astype(BF16)
    gb_ref[0] = jax.nn.sigmoid(proj(offs["gb"], d_model)).astype(BF16)


_R_CA = 0
_R_SA = _R_CA + A_ROT_DIM // 2
_R_CB = _R_SA + A_ROT_DIM // 2
_R_SB = _R_CB + B_ROPE_DIM // 2
_R_ONE = _R_SB + B_ROPE_DIM // 2


def _rope_trig(positions):
    pos = positions.astype(F32)[:, None, :]

    def cs(rot):
        inv = ROPE_THETA ** (-jnp.arange(0, rot, 2, dtype=F32) / rot)
        ang = pos * inv[None, :, None]
        return jnp.cos(ang), jnp.sin(ang)

    return cs(A_ROT_DIM) + cs(B_ROPE_DIM)


def _rope_expansion():
    e = np.zeros((LANES, 4 * LANES), np.float32)
    half_a, half_b = A_ROT_DIM // 2, B_ROPE_DIM // 2
    for j in range(LANES):
        d = j % A_HEAD_DIM
        if d < A_ROT_DIM:
            e[_R_CA + d % half_a, j] = 1.0
            e[_R_SA + d % half_a, LANES + j] = -1.0 if d < half_a else 1.0
        else:
            e[_R_ONE, j] = 1.0
        d = j - B_NOPE_DIM
        if 0 <= d < B_ROPE_DIM:
            e[_R_CB + d % half_b, 2 * LANES + j] = 1.0
            e[_R_SB + d % half_b, 3 * LANES + j] = -1.0 if d < half_b else 1.0
        else:
            e[_R_ONE, 2 * LANES + j] = 1.0
    return jnp.asarray(np.concatenate([e, e, e], axis=0), BF16)


def _regroup_w_in(w_in):
    d_model = w_in.shape[0]
    a_kv = A_KV_HEADS * A_HEAD_DIM
    splits = [A_HEADS * A_HEAD_DIM, a_kv, a_kv, IDX_HEADS * IDX_DIM, IDX_DIM, IDX_HEADS,
              B_Q_RANK, B_KV_RANK, B_ROPE_DIM, d_model, d_model]
    assert sum(splits) == w_in.shape[1]
    offs = np.concatenate([[0], np.cumsum(splits)])
    seg = [w_in[:, offs[i]:offs[i + 1]] for i in range(len(splits))]
    a_q, a_k, a_v, i_q, i_k, i_w, b_cq, b_ckv, b_kr, g_a, g_b = seg
    pad = jnp.zeros((d_model, LANES - _T_IW - IDX_HEADS), w_in.dtype)
    return jnp.concatenate(
        [a_q, a_k, a_v, i_q, b_cq, b_ckv, g_a, g_b, i_k, b_kr, i_w, pad], axis=1)


def _pad_heads(w, width):
    r, h, d = w.shape
    return jnp.pad(w, ((0, 0), (0, 0), (0, width - d))).reshape(r, h * width)


def _const_spec(shape):
    nd = len(shape)
    return pl.BlockSpec(shape, lambda *_: (0,) * nd)


def _proj_call(x, trig, expand, gmix, wcat, gik, gq, gkv, wuq, wuk, wuv, tm):
    b, s, d = x.shape
    grid = (b, s // tm)
    offs, total = _proj_layout(d)
    assert wcat.shape == (d, total)
    tok = lambda w: pl.BlockSpec((1, tm, w), lambda i, j: (i, j, 0))
    head = lambda h, w: pl.BlockSpec((1, h, tm, w), lambda i, j: (i, 0, j, 0))
    seq_on_lanes = lambda t: pl.BlockSpec((1, t.shape[1], tm), lambda i, j: (i, 0, j))
    consts = (expand, gmix, wcat, gik, gq, gkv, wuq, wuk, wuv)
    in_specs = ([tok(d)] + [seq_on_lanes(t) for t in trig]
                + [_const_spec(c.shape) for c in consts])
    sds = jax.ShapeDtypeStruct
    out_shape = [
        sds((b, A_HEADS, s, A_HEAD_DIM), BF16),
        sds((b, A_KV_HEADS, s, A_HEAD_DIM), BF16),
        sds((b, A_KV_HEADS, s, LANES), BF16),
        sds((b, IDX_HEADS, s, IDX_DIM), BF16),
        sds((b, s, IDX_DIM), BF16),
        sds((b, s, LANES), F32),
        sds((b, B_HEADS, s, LANES), BF16),
        sds((b, B_HEADS, s, LANES), BF16),
        sds((b, B_HEADS, s, LANES), BF16),
        sds((b, s, d), BF16),
        sds((b, s, d), BF16),
    ]
    out_specs = [
        head(A_HEADS, A_HEAD_DIM), head(A_KV_HEADS, A_HEAD_DIM),
        head(A_KV_HEADS, LANES), head(IDX_HEADS, IDX_DIM), tok(IDX_DIM),
        tok(LANES), head(B_HEADS, LANES), head(B_HEADS, LANES),
        head(B_HEADS, LANES), tok(d), tok(d)]
    return pl.pallas_call(
        functools.partial(_proj_kernel, offs=offs),
        grid=grid,
        in_specs=in_specs,
        out_specs=out_specs,
        out_shape=out_shape,
        compiler_params=pltpu.CompilerParams(
            dimension_semantics=("arbitrary", "arbitrary"),
            vmem_limit_bytes=VMEM_LIMIT_BYTES),
        name="proj",
    )(x, *trig, *consts)

def _prepare_proj_weights(norm_mix_g, w_in, idx_k_norm_g, q_a_norm_g, kv_a_norm_g,
                          w_uq, w_uk, w_uv):
    gmix = norm_mix_g.reshape(1, -1)
    wcat = _regroup_w_in(w_in).astype(BF16)
    gik = jnp.pad(idx_k_norm_g, (0, LANES - IDX_DIM)).reshape(1, LANES)
    gq = q_a_norm_g.reshape(1, -1)
    gkv = kv_a_norm_g.reshape(1, -1)
    wuq = _pad_heads(w_uq, LANES).astype(BF16)
    wuk = _pad_heads(w_uk, LANES).astype(BF16)
    wuv = _pad_heads(w_uv, LANES).astype(BF16)
    return gmix, wcat, gik, gq, gkv, wuq, wuk, wuv


_INT_MIN = -2 ** 31
_ORD_NEG_INF = 0x007FFFFF
_ORD_LOWEST_FINITE = 0x00800000


def _ordinal_to_float(u):
    o = u ^ jnp.int32(_INT_MIN)
    bits = o ^ ((o >> 31) & jnp.int32(0x7FFFFFFF))
    return lax.bitcast_convert_type(bits, F32)


def _softmax_update(s, v, m_old, acc, rows_may_be_empty=True):
    m_new = jnp.maximum(m_old, jnp.max(s, axis=-1, keepdims=True))
    if rows_may_be_empty:
        m_new = jnp.maximum(m_new, F32_LOWEST)
    p = jnp.concatenate(
        [jnp.exp2(s[:, j * LANES:(j + 1) * LANES] - m_new)
         for j in range(s.shape[1] // LANES)], axis=1)
    alpha = jnp.exp2(m_old - m_new)
    return m_new, alpha * acc + _dot(p.astype(BF16), v)


def _attend_chunks(n_heads, m_ref, acc_ref, chunk_fns):
    for h in range(n_heads):
        m, acc = m_ref[h], acc_ref[h]
        for fn in chunk_fns:
            m, acc = _softmax_update(*fn(h), m, acc)
        m_ref[h] = m
        acc_ref[h] = acc


def _for_chunk_groups(n, body):
    def group(i, carry):
        body(tuple(4 * i + r for r in range(4)))
        return carry

    lax.fori_loop(0, lax.shift_right_logical(n, 2), group, 0)
    done = n & ~3

    @pl.when((n & 2) != 0)
    def _():
        body((done, done + 1))

    @pl.when((n & 1) != 0)
    def _():
        body((n - 1,))


def _init_softmax_state(m_ref, acc_ref):
    m_ref[...] = jnp.full(m_ref.shape, NEG_INF, F32)
    acc_ref[...] = jnp.zeros(acc_ref.shape, F32)


def _finish_heads(acc_even, acc_odd, lane):
    o0 = acc_even * (1.0 / acc_even[:, B_V_DIM:B_V_DIM + 1])
    o1 = acc_odd * (1.0 / acc_odd[:, B_V_DIM:B_V_DIM + 1])
    return jnp.where(lane < B_V_DIM, o0, pltpu.roll(o1, B_V_DIM, axis=1))


def _store_heads(o_ref, acc_ref, n_heads):
    lane = lax.broadcasted_iota(jnp.int32, acc_ref.shape[1:], 1)
    for pair in range(n_heads // 2):
        o_ref[0, :, pair * LANES:(pair + 1) * LANES] = _finish_heads(
            acc_ref[2 * pair], acc_ref[2 * pair + 1], lane).astype(o_ref.dtype)


_COARSE_PASSES = 16
_FINE_PASSES = 17
_PASSES_PER_CHUNK = 4


def _attn_kernel(ik_ref, ak_ref, av_ref, kb_ref, vb_ref, posk_col_ref, posk_row_ref,
                 iq0_ref, iq1_ref, iw0_ref, iw1_ref, aq0_ref, aq1_ref, qb0_ref, qb1_ref,
                 pqr0_ref, pqr1_ref, pqc0_ref, pqc1_ref,
                 oa0_ref, oa1_ref, ob0_ref, ob1_ref,
                 isc_ref, xb_ref, bias_ref, wt_ref, ma_ref, acca_ref, mb_ref, accb_ref,
                 *, topk, nq):
    pi = pl.program_id(1)
    tq = aq0_ref.shape[2]
    half = tq // 2
    nk0 = pi + 1
    n_virtual = nq + 1
    blocks = (
        dict(idx=0, nk=nk0, off=0, iq=iq0_ref, iw=iw0_ref, aq=aq0_ref, pqr=pqr0_ref,
             oa=oa0_ref, ob=ob0_ref),
        dict(idx=1, nk=nq - pi, off=nk0, iq=iq1_ref, iw=iw1_ref, aq=aq1_ref, pqr=pqr1_ref,
             oa=oa1_ref, ob=ob1_ref),
    )

    for blk in blocks:
        wt_ref[blk["idx"]] = blk["iw"][0].T[_T_IW:_T_IW + IDX_HEADS, :]

    def idx_chunk(v):
        in1 = v >= nk0
        c = v - jnp.where(in1, nk0, 0)
        posq = jnp.where(in1, pqr1_ref[0], pqr0_ref[0])
        wt = jnp.where(in1, wt_ref[1], wt_ref[0])
        iqs = [jnp.where(in1, iq1_ref[0, h], iq0_ref[0, h]) for h in range(IDX_HEADS)]
        for sub in range(2):
            k0 = pl.multiple_of(c * tq + sub * half, half)
            ikc = ik_ref[0, pl.ds(k0, half), :]
            acc = jnp.zeros((half, tq), F32)
            for h in range(IDX_HEADS):
                d = _dot_nt(ikc, iqs[h])
                acc = acc + wt[h:h + 1, :] * jnp.maximum(d, 0.0)
            causal = posk_col_ref[0, pl.ds(k0, half), :] <= posq
            score = jnp.where(causal, acc, NEG_INF)
            isc_ref[v, sub * half:(sub + 1) * half, :] = score
            xb_ref[v, sub * half:(sub + 1) * half, :] = score.astype(BF16)

    for v in range(n_virtual):
        idx_chunk(v)

    _init_softmax_state(mb_ref, accb_ref)

    def latent_chunk(v):
        diagonal = v in (0, n_virtual - 1)
        if v == 0:
            block, c = 0, nk0 - 1
        elif v >= nq // 2:
            block, c = 1, v - nk0
        else:
            block = None
            in1 = v >= nk0
            c = jnp.where(in1, v - nk0, v - 1)
            b = in1.astype(jnp.int32)
        if block is not None:
            b = block
            q_ref, pqc_ref = (qb1_ref, pqc1_ref) if block else (qb0_ref, pqc0_ref)
        k0 = pl.multiple_of(c * tq, tq)
        if diagonal:
            causal = posk_row_ref[0, :, pl.ds(k0, tq)] <= pqc_ref[0]
        for h in range(B_HEADS):
            if block is not None:
                q = q_ref[0, h]
            else:
                q = jnp.where(in1, qb1_ref[0, h], qb0_ref[0, h])
            s = _dot_nt(q, kb_ref[0, h, pl.ds(k0, tq), :])
            if diagonal:
                s = jnp.where(causal, s, NEG_INF)
            m, acc = _softmax_update(s, vb_ref[0, h, pl.ds(k0, tq), :],
                                     mb_ref[b, h], accb_ref[b, h], rows_may_be_empty=False)
            mb_ref[b, h] = m
            accb_ref[b, h] = acc

    packed = 2 * SUBLANES
    one_b, zero_b = jnp.asarray(1, BF16), jnp.asarray(0, BF16)

    def count_ge(thr0, thr1, coarse):
        rows = packed if coarse else SUBLANES
        dtype = BF16 if coarse else F32
        acc0 = jnp.zeros((rows, tq), dtype)
        acc1 = jnp.zeros((rows, tq), dtype)
        zero = jnp.zeros((rows, tq), dtype)
        for v in range(n_virtual):
            in1 = v >= nk0
            thr = jnp.where(in1, thr1, thr0)
            if coarse:
                hit = jnp.where(xb_ref[v] >= thr, one_b, zero_b)
            else:
                hit = jnp.where(isc_ref[v] >= thr, 1.0, 0.0)
            part = hit[:rows]
            for r in range(1, tq // rows):
                part = part + hit[r * rows:(r + 1) * rows]
            acc0 = acc0 + jnp.where(in1, zero, part)
            acc1 = acc1 + jnp.where(in1, part, zero)
        return (jnp.sum(acc0.astype(F32), axis=0, keepdims=True),
                jnp.sum(acc1.astype(F32), axis=0, keepdims=True))

    def coarse_ordinal(j):
        low = jnp.where(j < 0x8000, jnp.int32(0xFFFF), jnp.int32(0))
        return lax.shift_left(j, 16) | low

    def coarse_pass(it, js):
        cands = [j | lax.shift_left(jnp.int32(1), 15 - it) for j in js]
        thrs = [_ordinal_to_float(coarse_ordinal(c)).astype(BF16) for c in cands]
        cnts = count_ge(*thrs, coarse=True)
        return tuple(jnp.where(n >= topk, c, j) for n, c, j in zip(cnts, cands, js))

    def fine_pass(it, ds, bases):
        cands = [d | lax.shift_left(jnp.int32(1), 16 - it) for d in ds]
        thrs = [_ordinal_to_float(b + c) for b, c in zip(bases, cands)]
        cnts = count_ge(*thrs, coarse=False)
        return tuple(jnp.where(n >= topk, c, d) for n, c, d in zip(cnts, cands, ds))

    zeros = (jnp.zeros((1, tq), jnp.int32),) * 2
    first_fine = _COARSE_PASSES // _PASSES_PER_CHUNK
    js = zeros
    for v in range(first_fine):
        latent_chunk(v)
        for r in range(_PASSES_PER_CHUNK):
            js = coarse_pass(_PASSES_PER_CHUNK * v + r, js)
    js = [jnp.maximum(j, _ORD_NEG_INF >> 16) for j in js]
    bases = [coarse_ordinal(j) - jnp.int32(1 << 16) for j in js]
    ds = zeros
    for v in range(first_fine, n_virtual):
        latent_chunk(v)
        first_pass = _PASSES_PER_CHUNK * (v - first_fine)
        for it in range(first_pass, min(first_pass + _PASSES_PER_CHUNK, _FINE_PASSES)):
            ds = fine_pass(it, ds, bases)

    t_us = [base + d for base, d in zip(bases, ds)]
    shorts = [(t_u >= 0) & (t_u < _ORD_LOWEST_FINITE) for t_u in t_us]
    thrs = [_ordinal_to_float(jnp.where(short, jnp.int32(_ORD_LOWEST_FINITE), t_u))
            for short, t_u in zip(shorts, t_us)]
    n_ges = [jnp.zeros((1, tq), F32)] * 2
    for v in range(n_virtual):
        in1 = v >= nk0
        sel = isc_ref[v] >= jnp.where(in1, thrs[1], thrs[0])
        bias_ref[v] = jnp.where(sel, 0.0, NEG_INF).T
        cnt = jnp.sum(jnp.where(sel, 1.0, 0.0), axis=0, keepdims=True)
        n_ges = [n_ges[0] + jnp.where(in1, 0.0, cnt), n_ges[1] + jnp.where(in1, cnt, 0.0)]

    for blk, thr, short, n_ge in zip(blocks, thrs, shorts, n_ges):
        off, nk = blk["off"], blk["nk"]
        tied = jnp.logical_and(n_ge > topk, jnp.logical_not(short))

        @pl.when(jnp.max(jnp.where(tied, 1.0, 0.0)) > 0.0)
        def _(thr=thr, off=off, nk=nk):
            def count_gt(c, acc):
                hit = jnp.where(isc_ref[off + c] > thr, 1.0, 0.0)
                return acc + jnp.sum(hit, axis=0, keepdims=True)
            n_gt = lax.fori_loop(0, nk, count_gt, jnp.zeros((1, tq), F32))
            need = topk - n_gt
            row = lax.broadcasted_iota(jnp.int32, (tq, tq), 0)
            col = lax.broadcasted_iota(jnp.int32, (tq, tq), 1)
            below = jnp.where(col < row, 1.0, 0.0).astype(BF16)

            def fix(c, seen):
                x = isc_ref[off + c]
                eq = jnp.where(x == thr, 1.0, 0.0)
                rank = _dot(below, eq.astype(BF16)) + seen
                sel = (x > thr) | ((x == thr) & (rank < need))
                bias_ref[off + c] = jnp.where(sel, 0.0, NEG_INF).T
                return seen + jnp.sum(eq, axis=0, keepdims=True)

            lax.fori_loop(0, nk, fix, jnp.zeros((1, tq), F32))

    _init_softmax_state(ma_ref, acca_ref)
    for blk in blocks:
        def attend(chunks, blk=blk):
            def chunk_fn(c):
                k0 = pl.multiple_of(c * tq, tq)
                bias = bias_ref[blk["off"] + c]

                def fn(h):
                    g = h // A_GROUP
                    s = _dot_nt(blk["aq"][0, h], ak_ref[0, g, pl.ds(k0, tq), :]) + bias
                    return s, av_ref[0, g, pl.ds(k0, tq), :]
                return fn

            _attend_chunks(A_HEADS, ma_ref.at[blk["idx"]], acca_ref.at[blk["idx"]],
                           [chunk_fn(c) for c in chunks])

        _for_chunk_groups(blk["nk"], attend)
        _store_heads(blk["oa"], acca_ref.at[blk["idx"]], A_HEADS)
        _store_heads(blk["ob"], accb_ref.at[blk["idx"]], B_HEADS)


def _attn_call(ik, iq, iw, aq, ak, av, qb, kb, vb, positions, tq, topk):
    b, heads, s, _ = aq.shape
    nq = s // tq
    n_pairs = nq // 2
    assert s % tq == 0 and nq % 2 == 0
    assert nq + 1 == (_COARSE_PASSES + _FINE_PASSES - 1) // _PASSES_PER_CHUNK + 1
    assert (nq + 1) * tq // (2 * SUBLANES) <= 256
    pos_col = positions[:, :, None]
    pos_row = positions[:, None, :]

    def per_seq(*shape):
        nd = len(shape)
        return pl.BlockSpec((1,) + shape, lambda i, j: (i,) + (0,) * nd)

    blk_index = (lambda j: j, lambda j: nq - 1 - j)

    def qhead(h, w, which):
        return pl.BlockSpec((1, h, tq, w), lambda i, j: (i, 0, blk_index[which](j), 0))

    def qrows(w, which):
        return pl.BlockSpec((1, tq, w), lambda i, j: (i, blk_index[which](j), 0))

    def qlanes(which):
        return pl.BlockSpec((1, 1, tq), lambda i, j: (i, 0, blk_index[which](j)))

    in_specs = [
        per_seq(s, IDX_DIM), per_seq(A_KV_HEADS, s, A_HEAD_DIM), per_seq(A_KV_HEADS, s, LANES),
        per_seq(B_HEADS, s, LANES), per_seq(B_HEADS, s, LANES),
        per_seq(s, 1), per_seq(1, s),
        qhead(IDX_HEADS, IDX_DIM, 0), qhead(IDX_HEADS, IDX_DIM, 1),
        qrows(LANES, 0), qrows(LANES, 1),
        qhead(A_HEADS, A_HEAD_DIM, 0), qhead(A_HEADS, A_HEAD_DIM, 1),
        qhead(B_HEADS, LANES, 0), qhead(B_HEADS, LANES, 1),
        qlanes(0), qlanes(1), qrows(1, 0), qrows(1, 1),
    ]
    width = heads * A_HEAD_DIM
    out_sds = jax.ShapeDtypeStruct((b, s // 2, width), BF16)
    out_lo = pl.BlockSpec((1, tq, width), lambda i, j: (i, j, 0))
    out_hi = pl.BlockSpec((1, tq, width), lambda i, j: (i, n_pairs - 1 - j, 0))
    state = pltpu.VMEM((2, heads, tq, LANES), F32)
    return pl.pallas_call(
        functools.partial(_attn_kernel, topk=topk, nq=nq),
        grid=(b, n_pairs),
        in_specs=in_specs,
        out_specs=[out_lo, out_hi, out_lo, out_hi],
        out_shape=[out_sds] * 4,
        scratch_shapes=[
            pltpu.VMEM((nq + 1, tq, tq), F32),
            pltpu.VMEM((nq + 1, tq, tq), BF16),
            pltpu.VMEM((nq + 1, tq, tq), F32),
            pltpu.VMEM((2, IDX_HEADS, tq), F32),
            state, state, state, state,
        ],
        compiler_params=pltpu.CompilerParams(
            dimension_semantics=("arbitrary", "arbitrary"),
            vmem_limit_bytes=VMEM_LIMIT_BYTES),
        name="attn",
    )(ik, ak, av, kb, vb, pos_col, pos_row, iq, iq, iw, iw, aq, aq, qb, qb,
      pos_row, pos_row, pos_col, pos_col)


def _shift_rows(u, prev_tail, shift, row8):
    rolled = pltpu.roll(u, shift, axis=0)
    head = jnp.where(row8 < shift, pltpu.roll(prev_tail, shift, axis=0),
                     rolled[:SUBLANES])
    return jnp.concatenate([head, rolled[SUBLANES:]], axis=0)


def _ffn_kernel(x_ref, oa_lo_ref, oa_hi_ref, ob_lo_ref, ob_hi_ref, ga_ref, gb_ref,
                wba_ref, wbb_ref, wout_ref, gffn_ref, wup_ref, cw_ref, cb_ref, wdown_ref,
                gfin_ref, out_ref, tail_ref, *, n_chunks, final_norm):
    j = pl.program_id(1)
    x = x_ref[0]
    tm = x.shape[0]
    first_half = j < pl.num_programs(1) // 2
    oa = jnp.where(first_half, oa_lo_ref[0], oa_hi_ref[0])
    ob = jnp.where(first_half, ob_lo_ref[0], ob_hi_ref[0])
    ya = _dot(oa, wba_ref[...])
    yb = _dot(ob, wbb_ref[...])
    merged = ga_ref[0].astype(F32) * ya + gb_ref[0].astype(F32) * yb
    h1 = x + _dot(merged.astype(BF16), wout_ref[...])
    ms = jnp.mean(h1 * h1, axis=-1, keepdims=True)
    hn = (h1 * lax.rsqrt(ms + EPS) * gffn_ref[...]).astype(BF16)

    fc = wdown_ref.shape[1]
    row8 = lax.broadcasted_iota(jnp.int32, (SUBLANES, fc), 0)

    @pl.when(j == 0)
    def _():
        tail_ref[...] = jnp.zeros_like(tail_ref)

    acc = h1
    for c in range(n_chunks):
        halves = []
        for part in range(2):
            idx = part * n_chunks + c
            u = _dot(hn, wup_ref[:, idx * fc:(idx + 1) * fc])
            prev_tail = tail_ref[idx]
            tail_ref[idx] = u[tm - SUBLANES:]
            y = (cb_ref[idx] + cw_ref[0, idx] * _shift_rows(u, prev_tail, 2, row8)
                 + cw_ref[1, idx] * _shift_rows(u, prev_tail, 1, row8)
                 + cw_ref[2, idx] * u)
            halves.append(y)
        gate, val = halves
        act = (gate * jax.nn.sigmoid(gate) * val).astype(BF16)
        acc = acc + _dot(act, wdown_ref[c])
    if final_norm:
        ms = jnp.mean(acc * acc, axis=-1, keepdims=True)
        acc = acc * lax.rsqrt(ms + EPS) * gfin_ref[...]
    out_ref[0] = acc


def _ffn_call(x, oa_lo, oa_hi, ob_lo, ob_hi, ga, gb, wba, wbb, wout, gffn, wup, cw, cb,
              wdown, gfin, tm, final_norm):
    b, s, d = x.shape
    n_chunks, fc, _ = wdown.shape
    n_parts = 2 * n_chunks
    assert wup.shape == (d, n_parts * fc) and fc % LANES == 0
    n_tiles = s // tm
    assert s % tm == 0 and n_tiles % 2 == 0
    n_half = n_tiles // 2
    tok = lambda w: pl.BlockSpec((1, tm, w), lambda i, j: (i, j, 0))
    lo = lambda w: pl.BlockSpec((1, tm, w), lambda i, j: (i, jnp.minimum(j, n_half - 1), 0))
    hi = lambda w: pl.BlockSpec((1, tm, w), lambda i, j: (i, jnp.maximum(j - n_half, 0), 0))
    wa, wb = oa_lo.shape[-1], ob_lo.shape[-1]
    weights = (wba, wbb, wout, gffn, wup, cw, cb, wdown, gfin)
    return pl.pallas_call(
        functools.partial(_ffn_kernel, n_chunks=n_chunks, final_norm=final_norm),
        grid=(b, n_tiles),
        in_specs=[tok(d), lo(wa), hi(wa), lo(wb), hi(wb), tok(d), tok(d)]
        + [_const_spec(w.shape) for w in weights],
        out_specs=tok(d),
        out_shape=jax.ShapeDtypeStruct((b, s, d), x.dtype),
        scratch_shapes=[pltpu.VMEM((n_parts, SUBLANES, fc), F32)],
        compiler_params=pltpu.CompilerParams(
            dimension_semantics=("arbitrary", "arbitrary"),
            vmem_limit_bytes=VMEM_LIMIT_BYTES),
        name="ffn",
    )(x, oa_lo, oa_hi, ob_lo, ob_hi, ga, gb, *weights)


def _prepare_ffn_weights(w_branch_a, w_branch_b, w_out, norm_ffn_g, w_up, conv_w,
                         conv_b, w_down, norm_final_g, n_chunks):
    d, two_f = w_up.shape
    fc = two_f // (2 * n_chunks)
    wup = w_up.astype(BF16)
    cw = conv_w.reshape(CONV_WIDTH, 2 * n_chunks, 1, fc)
    cb = conv_b.reshape(2 * n_chunks, 1, fc)
    wdown = w_down.reshape(n_chunks, fc, d).astype(BF16)
    return (w_branch_a.astype(BF16), w_branch_b.astype(BF16), w_out.astype(BF16),
            norm_ffn_g.reshape(1, -1), wup, cw, cb, wdown, norm_final_g.reshape(1, -1))


TM_PROJ = 512
TQ_ATTN = 256
TM_FFN = 512
FFN_CHUNKS = 1


def kernel(x, positions, norm_mix_g, w_in, idx_k_norm_g, q_a_norm_g, kv_a_norm_g, w_uq, w_uk, w_uv, w_branch_a, w_branch_b, w_out, norm_ffn_g, w_up, conv_w, conv_b, w_down, norm_final_g):
    b, s, _ = x.shape
    depth = w_in.shape[0]
    topk = min(INDEX_TOPK_MAX, s // 4)
    trig = _rope_trig(positions)
    expand = _rope_expansion()
    h = x
    for l in range(depth):
        pw = _prepare_proj_weights(norm_mix_g[l], w_in[l], idx_k_norm_g[l], q_a_norm_g[l],
                                   kv_a_norm_g[l], w_uq[l], w_uk[l], w_uv[l])
        aq, ak, av, iq, ik, iw, qb, kb, vb, ga, gb = _proj_call(h, trig, expand, *pw, tm=TM_PROJ)
        mixed = _attn_call(ik, iq, iw, aq, ak, av, qb, kb, vb, positions, TQ_ATTN, topk)
        fw = _prepare_ffn_weights(w_branch_a[l], w_branch_b[l], w_out[l], norm_ffn_g[l],
                                  w_up[l], conv_w[l], conv_b[l], w_down[l], norm_final_g,
                                  FFN_CHUNKS)
        h = _ffn_call(h, *mixed, ga, gb, *fw, tm=TM_FFN, final_norm=(l == depth - 1))
    return h
```

```python
import functools

import jax
import jax.numpy as jnp
import numpy as np
from jax import lax
from jax.experimental import pallas as pl
from jax.experimental.pallas import tpu as pltpu

ROPE_THETA = 500000.0
EPS = 1e-6
A_HEADS = 8
A_KV_HEADS = 2
A_GROUP = A_HEADS // A_KV_HEADS
A_HEAD_DIM = 64
A_ROT_DIM = 16
IDX_HEADS = 8
IDX_DIM = 64
INDEX_TOPK_MAX = 256
B_HEADS = 8
B_Q_RANK = 256
B_KV_RANK = 128
B_NOPE_DIM = 64
B_ROPE_DIM = 32
B_V_DIM = 64
CONV_WIDTH = 3

LANES = 128
SUBLANES = 8
VMEM_LIMIT_BYTES = 56 * 1024 * 1024

_T_IW = IDX_DIM + B_ROPE_DIM
assert IDX_DIM == B_NOPE_DIM and _T_IW + IDX_HEADS <= LANES


def _proj_layout(d_model):
    widths = dict(aq=A_HEADS * A_HEAD_DIM, ak=A_KV_HEADS * A_HEAD_DIM,
                  av=A_KV_HEADS * A_HEAD_DIM, iq=IDX_HEADS * IDX_DIM, cq=B_Q_RANK,
                  ckv=B_KV_RANK, ga=d_model, gb=d_model, tail=LANES)
    offsets, total = {}, 0
    for name, width in widths.items():
        assert width % LANES == 0
        offsets[name] = total
        total += width
    return offsets, total

F32 = jnp.float32
BF16 = jnp.bfloat16
NEG_INF = float("-inf")
LOG2_E = 1.4426950408889634
F32_LOWEST = float(np.finfo(np.float32).min)


def _dot(a, b):
    return jnp.dot(a, b, preferred_element_type=F32)


def _dot_nt(a, b):
    return lax.dot_general(a, b, (((1,), (1,)), ((), ())), preferred_element_type=F32)


def _rope_lanes(x, cos_t, sin_t, take_up, half):
    up = pltpu.roll(x, LANES - half, axis=1)
    dn = pltpu.roll(x, half, axis=1)
    return x * cos_t + jnp.where(take_up, up, dn) * sin_t


def _expand_rope_tables(trig_t, expand_ref):
    tm = trig_t[0].shape[1]
    used = sum(piece.shape[0] for piece in trig_t)
    rest = lax.broadcasted_iota(jnp.int32, (LANES - used, tm), 0)
    ones_then_zeros = jnp.where(rest == 0, 1.0, 0.0)
    t = jnp.concatenate(list(trig_t) + [ones_then_zeros], axis=0).T
    a1 = t.astype(BF16)
    r1 = t - a1.astype(F32)
    a2 = r1.astype(BF16)
    a3 = (r1 - a2.astype(F32)).astype(BF16)
    tables = _dot(jnp.concatenate([a1, a2, a3], axis=1), expand_ref[...])
    return [tables[:, i * LANES:(i + 1) * LANES] for i in range(4)]


def _proj_kernel(x_ref, ca_ref, sa_ref, cb_ref, sb_ref, expand_ref, gmix_ref, w_ref,
                 gik_ref, gq_ref, gkv_ref, wuq_ref, wuk_ref, wuv_ref,
                 aq_ref, ak_ref, av_ref, iq_ref, ik_ref, iw_ref, qb_ref, kb_ref,
                 vb_ref, ga_ref, gb_ref, *, offs):
    x = x_ref[0]
    tm, d_model = x.shape
    ms = jnp.mean(x * x, axis=-1, keepdims=True)
    hn = (x * lax.rsqrt(ms + EPS) * gmix_ref[...]).astype(BF16)

    lane = lax.broadcasted_iota(jnp.int32, (tm, LANES), 1)
    ca, sa, cb, sb = _expand_rope_tables(
        (ca_ref[0], sa_ref[0], cb_ref[0], sb_ref[0]), expand_ref)
    up_a = (lane % A_HEAD_DIM) < (A_ROT_DIM // 2)
    up_b = lane < (B_NOPE_DIM + B_ROPE_DIM // 2)
    v_one = jnp.where(lane == A_HEAD_DIM, 1.0, 0.0).astype(F32)
    low_half = lane < A_HEAD_DIM

    def proj(off, width):
        return _dot(hn, w_ref[:, off:off + width])

    def store_heads(ref, tile128, first_head, scale):
        t = tile128 * scale if scale != 1.0 else tile128
        ref[0, first_head] = t[:, :A_HEAD_DIM].astype(ref.dtype)
        ref[0, first_head + 1] = t[:, A_HEAD_DIM:].astype(ref.dtype)

    assert A_HEADS * A_HEAD_DIM == IDX_HEADS * IDX_DIM and A_HEAD_DIM == IDX_DIM
    n_head_pairs = A_HEADS // 2
    for off, ref, scale in ((offs["aq"], aq_ref, A_HEAD_DIM ** -0.5 * LOG2_E),
                            (offs["iq"], iq_ref, IDX_DIM ** -0.5)):
        full = proj(off, n_head_pairs * LANES)
        for j in range(n_head_pairs):
            t = _rope_lanes(full[:, j * LANES:(j + 1) * LANES], ca, sa, up_a,
                            A_ROT_DIM // 2)
            store_heads(ref, t, 2 * j, scale)

    assert A_KV_HEADS * A_HEAD_DIM == LANES and offs["av"] == offs["ak"] + LANES
    akv = proj(offs["ak"], 2 * LANES)
    store_heads(ak_ref, _rope_lanes(akv[:, :LANES], ca, sa, up_a, A_ROT_DIM // 2),
                0, 1.0)
    v2 = akv[:, LANES:]
    av_ref[0, 0] = (jnp.where(low_half, v2, 0.0) + v_one).astype(BF16)
    v2r = pltpu.roll(v2, A_HEAD_DIM, axis=1)
    av_ref[0, 1] = (jnp.where(low_half, v2r, 0.0) + v_one).astype(BF16)

    tail = proj(offs["tail"], LANES)
    ik_sq = jnp.where(low_half, tail * tail, 0.0)
    ik_ms = jnp.sum(ik_sq, axis=-1, keepdims=True) * (1.0 / IDX_DIM)
    ik_n = tail * lax.rsqrt(ik_ms + EPS) * gik_ref[...]
    ik_r = _rope_lanes(ik_n, ca, sa, up_a, A_ROT_DIM // 2)
    ik_ref[0] = ik_r[:, :IDX_DIM].astype(BF16)
    iw_ref[0] = tail * (IDX_HEADS ** -0.5)
    kr = _rope_lanes(tail, cb, sb, up_b, B_ROPE_DIM // 2)
    kr = jnp.where((lane >= B_NOPE_DIM) & (lane < B_NOPE_DIM + B_ROPE_DIM), kr, 0.0)

    cq = proj(offs["cq"], B_Q_RANK)
    cq_n = (cq * lax.rsqrt(jnp.mean(cq * cq, axis=-1, keepdims=True) + EPS)
            * gq_ref[...]).astype(BF16)
    ckv = proj(offs["ckv"], B_KV_RANK)
    ckv_n = (ckv * lax.rsqrt(jnp.mean(ckv * ckv, axis=-1, keepdims=True) + EPS)
             * gkv_ref[...]).astype(BF16)
    b_scale = (B_NOPE_DIM + B_ROPE_DIM) ** -0.5 * LOG2_E
    for h0 in range(0, B_HEADS, 2):
        cols = slice(h0 * LANES, (h0 + 2) * LANES)
        q2 = _dot(cq_n, wuq_ref[:, cols])
        k2 = _dot(ckv_n, wuk_ref[:, cols])
        v2 = _dot(ckv_n, wuv_ref[:, cols])
        for i in range(2):
            lanes = slice(i * LANES, (i + 1) * LANES)
            q = _rope_lanes(q2[:, lanes], cb, sb, up_b, B_ROPE_DIM // 2) * b_scale
            qb_ref[0, h0 + i] = q.astype(BF16)
            kb_ref[0, h0 + i] = (k2[:, lanes] + kr).astype(BF16)
            vb_ref[0, h0 + i] = (v2[:, lanes] + v_one).astype(BF16)

    ga_ref[0] = jax.nn.sigmoid(proj(offs["ga"], d_model)).astype(BF16)
    gb_ref[0] = jax.nn.sigmoid(proj(offs["gb"], d_model)).astype(BF16)


_R_CA = 0
_R_SA = _R_CA + A_ROT_DIM // 2
_R_CB = _R_SA + A_ROT_DIM // 2
_R_SB = _R_CB + B_ROPE_DIM // 2
_R_ONE = _R_SB + B_ROPE_DIM // 2


def _rope_trig(positions):
    pos = positions.astype(F32)[:, None, :]

    def cs(rot):
        inv = ROPE_THETA ** (-jnp.arange(0, rot, 2, dtype=F32) / rot)
        ang = pos * inv[None, :, None]
        return jnp.cos(ang), jnp.sin(ang)

    return cs(A_ROT_DIM) + cs(B_ROPE_DIM)


def _rope_expansion():
    e = np.zeros((LANES, 4 * LANES), np.float32)
    half_a, half_b = A_ROT_DIM // 2, B_ROPE_DIM // 2
    for j in range(LANES):
        d = j % A_HEAD_DIM
        if d < A_ROT_DIM:
            e[_R_CA + d % half_a, j] = 1.0
            e[_R_SA + d % half_a, LANES + j] = -1.0 if d < half_a else 1.0
        else:
            e[_R_ONE, j] = 1.0
        d = j - B_NOPE_DIM
        if 0 <= d < B_ROPE_DIM:
            e[_R_CB + d % half_b, 2 * LANES + j] = 1.0
            e[_R_SB + d % half_b, 3 * LANES + j] = -1.0 if d < half_b else 1.0
        else:
            e[_R_ONE, 2 * LANES + j] = 1.0
    return jnp.asarray(np.concatenate([e, e, e], axis=0), BF16)


def _regroup_w_in(w_in):
    d_model = w_in.shape[0]
    a_kv = A_KV_HEADS * A_HEAD_DIM
    splits = [A_HEADS * A_HEAD_DIM, a_kv, a_kv, IDX_HEADS * IDX_DIM, IDX_DIM, IDX_HEADS,
              B_Q_RANK, B_KV_RANK, B_ROPE_DIM, d_model, d_model]
    assert sum(splits) == w_in.shape[1]
    offs = np.concatenate([[0], np.cumsum(splits)])
    seg = [w_in[:, offs[i]:offs[i + 1]] for i in range(len(splits))]
    a_q, a_k, a_v, i_q, i_k, i_w, b_cq, b_ckv, b_kr, g_a, g_b = seg
    pad = jnp.zeros((d_model, LANES - _T_IW - IDX_HEADS), w_in.dtype)
    return jnp.concatenate(
        [a_q, a_k, a_v, i_q, b_cq, b_ckv, g_a, g_b, i_k, b_kr, i_w, pad], axis=1)


def _pad_heads(w, width):
    r, h, d = w.shape
    return jnp.pad(w, ((0, 0), (0, 0), (0, width - d))).reshape(r, h * width)


def _const_spec(shape):
    nd = len(shape)
    return pl.BlockSpec(shape, lambda *_: (0,) * nd)


def _proj_call(x, trig, expand, gmix, wcat, gik, gq, gkv, wuq, wuk, wuv, tm):
    b, s, d = x.shape
    grid = (b, s // tm)
    offs, total = _proj_layout(d)
    assert wcat.shape == (d, total)
    tok = lambda w: pl.BlockSpec((1, tm, w), lambda i, j: (i, j, 0))
    head = lambda h, w: pl.BlockSpec((1, h, tm, w), lambda i, j: (i, 0, j, 0))
    seq_on_lanes = lambda t: pl.BlockSpec((1, t.shape[1], tm), lambda i, j: (i, 0, j))
    consts = (expand, gmix, wcat, gik, gq, gkv, wuq, wuk, wuv)
    in_specs = ([tok(d)] + [seq_on_lanes(t) for t in trig]
                + [_const_spec(c.shape) for c in consts])
    sds = jax.ShapeDtypeStruct
    out_shape = [
        sds((b, A_HEADS, s, A_HEAD_DIM), BF16),
        sds((b, A_KV_HEADS, s, A_HEAD_DIM), BF16),
        sds((b, A_KV_HEADS, s, LANES), BF16),
        sds((b, IDX_HEADS, s, IDX_DIM), BF16),
        sds((b, s, IDX_DIM), BF16),
        sds((b, s, LANES), F32),
        sds((b, B_HEADS, s, LANES), BF16),
        sds((b, B_HEADS, s, LANES), BF16),
        sds((b, B_HEADS, s, LANES), BF16),
        sds((b, s, d), BF16),
        sds((b, s, d), BF16),
    ]
    out_specs = [
        head(A_HEADS, A_HEAD_DIM), head(A_KV_HEADS, A_HEAD_DIM),
        head(A_KV_HEADS, LANES), head(IDX_HEADS, IDX_DIM), tok(IDX_DIM),
        tok(LANES), head(B_HEADS, LANES), head(B_HEADS, LANES),
        head(B_HEADS, LANES), tok(d), tok(d)]
    return pl.pallas_call(
        functools.partial(_proj_kernel, offs=offs),
        grid=grid,
        in_specs=in_specs,
        out_specs=out_specs,
        out_shape=out_shape,
        compiler_params=pltpu.CompilerParams(
            dimension_semantics=("arbitrary", "arbitrary"),
            vmem_limit_bytes=VMEM_LIMIT_BYTES),
        name="proj",
    )(x, *trig, *consts)

def _prepare_proj_weights(norm_mix_g, w_in, idx_k_norm_g, q_a_norm_g, kv_a_norm_g,
                          w_uq, w_uk, w_uv):
    gmix = norm_mix_g.reshape(1, -1)
    wcat = _regroup_w_in(w_in).astype(BF16)
    gik = jnp.pad(idx_k_norm_g, (0, LANES - IDX_DIM)).reshape(1, LANES)
    gq = q_a_norm_g.reshape(1, -1)
    gkv = kv_a_norm_g.reshape(1, -1)
    wuq = _pad_heads(w_uq, LANES).astype(BF16)
    wuk = _pad_heads(w_uk, LANES).astype(BF16)
    wuv = _pad_heads(w_uv, LANES).astype(BF16)
    return gmix, wcat, gik, gq, gkv, wuq, wuk, wuv


_INT_MIN = -2 ** 31
_ORD_NEG_INF = 0x007FFFFF
_ORD_LOWEST_FINITE = 0x00800000


def _ordinal_to_float(u):
    o = u ^ jnp.int32(_INT_MIN)
    bits = o ^ ((o >> 31) & jnp.int32(0x7FFFFFFF))
    return lax.bitcast_convert_type(bits, F32)


def _softmax_update(s, v, m_old, acc, rows_may_be_empty=True):
    m_new = jnp.maximum(m_old, jnp.max(s, axis=-1, keepdims=True))
    if rows_may_be_empty:
        m_new = jnp.maximum(m_new, F32_LOWEST)
    p = jnp.concatenate(
        [jnp.exp2(s[:, j * LANES:(j + 1) * LANES] - m_new)
         for j in range(s.shape[1] // LANES)], axis=1)
    alpha = jnp.exp2(m_old - m_new)
    return m_new, alpha * acc + _dot(p.astype(BF16), v)


def _init_softmax_state(m_ref, acc_ref):
    m_ref[...] = jnp.full(m_ref.shape, NEG_INF, F32)
    acc_ref[...] = jnp.zeros(acc_ref.shape, F32)


def _finish_heads(acc_even, acc_odd, lane):
    o0 = acc_even * (1.0 / acc_even[:, B_V_DIM:B_V_DIM + 1])
    o1 = acc_odd * (1.0 / acc_odd[:, B_V_DIM:B_V_DIM + 1])
    return jnp.where(lane < B_V_DIM, o0, pltpu.roll(o1, B_V_DIM, axis=1))


def _store_heads(o_ref, acc_ref, n_heads):
    lane = lax.broadcasted_iota(jnp.int32, acc_ref.shape[1:], 1)
    for pair in range(n_heads // 2):
        o_ref[0, :, pair * LANES:(pair + 1) * LANES] = _finish_heads(
            acc_ref[2 * pair], acc_ref[2 * pair + 1], lane).astype(o_ref.dtype)


_COARSE_PASSES = 16
_FINE_PASSES = 17
_PASSES_PER_CHUNK = 4


def _attn_kernel(ik_ref, ak_ref, av_ref, kb_ref, vb_ref, posk_col_ref, posk_row_ref,
                 iq0_ref, iq1_ref, iw0_ref, iw1_ref, aq0_ref, aq1_ref, qb0_ref, qb1_ref,
                 pqr0_ref, pqr1_ref, pqc0_ref, pqc1_ref,
                 oa0_ref, oa1_ref, ob0_ref, ob1_ref,
                 isc_ref, xb_ref, bias_ref, wt_ref, mb_ref, accb_ref,
                 *, topk, nq):
    pi = pl.program_id(1)
    tq = aq0_ref.shape[2]
    half = tq // 2
    nk0 = pi + 1
    n_virtual = nq + 1
    blocks = (
        dict(idx=0, nk=nk0, off=0, iq=iq0_ref, iw=iw0_ref, aq=aq0_ref, pqr=pqr0_ref,
             oa=oa0_ref, ob=ob0_ref),
        dict(idx=1, nk=nq - pi, off=nk0, iq=iq1_ref, iw=iw1_ref, aq=aq1_ref, pqr=pqr1_ref,
             oa=oa1_ref, ob=ob1_ref),
    )

    for blk in blocks:
        wt_ref[blk["idx"]] = blk["iw"][0].T[_T_IW:_T_IW + IDX_HEADS, :]

    def idx_chunk(v):
        in1 = v >= nk0
        c = v - jnp.where(in1, nk0, 0)
        posq = jnp.where(in1, pqr1_ref[0], pqr0_ref[0])
        wt = jnp.where(in1, wt_ref[1], wt_ref[0])
        iqs = [jnp.where(in1, iq1_ref[0, h], iq0_ref[0, h]) for h in range(IDX_HEADS)]
        for sub in range(2):
            k0 = pl.multiple_of(c * tq + sub * half, half)
            ikc = ik_ref[0, pl.ds(k0, half), :]
            acc = jnp.zeros((half, tq), F32)
            for h in range(IDX_HEADS):
                d = _dot_nt(ikc, iqs[h])
                acc = acc + wt[h:h + 1, :] * jnp.maximum(d, 0.0)
            causal = posk_col_ref[0, pl.ds(k0, half), :] <= posq
            score = jnp.where(causal, acc, NEG_INF)
            isc_ref[v, sub * half:(sub + 1) * half, :] = score
            xb_ref[v, sub * half:(sub + 1) * half, :] = score.astype(BF16)

    for v in range(n_virtual):
        idx_chunk(v)

    _init_softmax_state(mb_ref, accb_ref)

    def latent_chunk(v):
        diagonal = v in (0, n_virtual - 1)
        if v == 0:
            block, c = 0, nk0 - 1
        elif v >= nq // 2:
            block, c = 1, v - nk0
        else:
            block = None
            in1 = v >= nk0
            c = jnp.where(in1, v - nk0, v - 1)
            b = in1.astype(jnp.int32)
        if block is not None:
            b = block
            q_ref, pqc_ref = (qb1_ref, pqc1_ref) if block else (qb0_ref, pqc0_ref)
        k0 = pl.multiple_of(c * tq, tq)
        if diagonal:
            causal = posk_row_ref[0, :, pl.ds(k0, tq)] <= pqc_ref[0]
        for h in range(B_HEADS):
            if block is not None:
                q = q_ref[0, h]
            else:
                q = jnp.where(in1, qb1_ref[0, h], qb0_ref[0, h])
            s = _dot_nt(q, kb_ref[0, h, pl.ds(k0, tq), :])
            if diagonal:
                s = jnp.where(causal, s, NEG_INF)
            m, acc = _softmax_update(s, vb_ref[0, h, pl.ds(k0, tq), :],
                                     mb_ref[b, h], accb_ref[b, h], rows_may_be_empty=False)
            mb_ref[b, h] = m
            accb_ref[b, h] = acc

    packed = 2 * SUBLANES
    one_b, zero_b = jnp.asarray(1, BF16), jnp.asarray(0, BF16)

    def count_ge(thr0, thr1, coarse):
        rows = packed if coarse else SUBLANES
        dtype = BF16 if coarse else F32
        acc0 = jnp.zeros((rows, tq), dtype)
        acc1 = jnp.zeros((rows, tq), dtype)
        zero = jnp.zeros((rows, tq), dtype)
        for v in range(n_virtual):
            in1 = v >= nk0
            thr = jnp.where(in1, thr1, thr0)
            if coarse:
                hit = jnp.where(xb_ref[v] >= thr, one_b, zero_b)
            else:
                hit = jnp.where(isc_ref[v] >= thr, 1.0, 0.0)
            part = hit[:rows]
            for r in range(1, tq // rows):
                part = part + hit[r * rows:(r + 1) * rows]
            acc0 = acc0 + jnp.where(in1, zero, part)
            acc1 = acc1 + jnp.where(in1, part, zero)
        return (jnp.sum(acc0.astype(F32), axis=0, keepdims=True),
                jnp.sum(acc1.astype(F32), axis=0, keepdims=True))

    def coarse_ordinal(j):
        low = jnp.where(j < 0x8000, jnp.int32(0xFFFF), jnp.int32(0))
        return lax.shift_left(j, 16) | low

    def coarse_pass(it, js):
        cands = [j | lax.shift_left(jnp.int32(1), 15 - it) for j in js]
        thrs = [_ordinal_to_float(coarse_ordinal(c)).astype(BF16) for c in cands]
        cnts = count_ge(*thrs, coarse=True)
        return tuple(jnp.where(n >= topk, c, j) for n, c, j in zip(cnts, cands, js))

    def fine_pass(it, ds, bases):
        cands = [d | lax.shift_left(jnp.int32(1), 16 - it) for d in ds]
        thrs = [_ordinal_to_float(b + c) for b, c in zip(bases, cands)]
        cnts = count_ge(*thrs, coarse=False)
        return tuple(jnp.where(n >= topk, c, d) for n, c, d in zip(cnts, cands, ds))

    zeros = (jnp.zeros((1, tq), jnp.int32),) * 2
    first_fine = _COARSE_PASSES // _PASSES_PER_CHUNK
    js = zeros
    for v in range(first_fine):
        latent_chunk(v)
        for r in range(_PASSES_PER_CHUNK):
            js = coarse_pass(_PASSES_PER_CHUNK * v + r, js)
    js = [jnp.maximum(j, _ORD_NEG_INF >> 16) for j in js]
    bases = [coarse_ordinal(j) - jnp.int32(1 << 16) for j in js]
    ds = zeros
    for v in range(first_fine, n_virtual):
        latent_chunk(v)
        first_pass = _PASSES_PER_CHUNK * (v - first_fine)
        for it in range(first_pass, min(first_pass + _PASSES_PER_CHUNK, _FINE_PASSES)):
            ds = fine_pass(it, ds, bases)

    t_us = [base + d for base, d in zip(bases, ds)]
    shorts = [(t_u >= 0) & (t_u < _ORD_LOWEST_FINITE) for t_u in t_us]
    thrs = [_ordinal_to_float(jnp.where(short, jnp.int32(_ORD_LOWEST_FINITE), t_u))
            for short, t_u in zip(shorts, t_us)]
    n_ges = [jnp.zeros((1, tq), F32)] * 2
    for v in range(n_virtual):
        in1 = v >= nk0
        sel = isc_ref[v] >= jnp.where(in1, thrs[1], thrs[0])
        bias_ref[v] = jnp.where(sel, 0.0, NEG_INF).T
        cnt = jnp.sum(jnp.where(sel, 1.0, 0.0), axis=0, keepdims=True)
        n_ges = [n_ges[0] + jnp.where(in1, 0.0, cnt), n_ges[1] + jnp.where(in1, cnt, 0.0)]

    for blk, thr, short, n_ge in zip(blocks, thrs, shorts, n_ges):
        off, nk = blk["off"], blk["nk"]
        tied = jnp.logical_and(n_ge > topk, jnp.logical_not(short))

        @pl.when(jnp.max(jnp.where(tied, 1.0, 0.0)) > 0.0)
        def _(thr=thr, off=off, nk=nk):
            def count_gt(c, acc):
                hit = jnp.where(isc_ref[off + c] > thr, 1.0, 0.0)
                return acc + jnp.sum(hit, axis=0, keepdims=True)
            n_gt = lax.fori_loop(0, nk, count_gt, jnp.zeros((1, tq), F32))
            need = topk - n_gt
            row = lax.broadcasted_iota(jnp.int32, (tq, tq), 0)
            col = lax.broadcasted_iota(jnp.int32, (tq, tq), 1)
            below = jnp.where(col < row, 1.0, 0.0).astype(BF16)

            def fix(c, seen):
                x = isc_ref[off + c]
                eq = jnp.where(x == thr, 1.0, 0.0)
                rank = _dot(below, eq.astype(BF16)) + seen
                sel = (x > thr) | ((x == thr) & (rank < need))
                bias_ref[off + c] = jnp.where(sel, 0.0, NEG_INF).T
                return seen + jnp.sum(eq, axis=0, keepdims=True)

            lax.fori_loop(0, nk, fix, jnp.zeros((1, tq), F32))

    m_init = jnp.full((tq, LANES), NEG_INF, F32)
    acc_init = jnp.zeros((tq, LANES), F32)
    states = [(m_init, acc_init)] * A_HEADS
    block0_acc = [acc_init] * A_HEADS
    for v in range(n_virtual):
        static_block = 0 if v == 0 else (1 if v >= nq // 2 else None)
        in1 = v >= nk0
        c = {0: v, 1: v - nk0, None: jnp.where(in1, v - nk0, v)}[static_block]
        k0 = pl.multiple_of(c * tq, tq)
        bias = bias_ref[v]
        may_start_block1 = 1 <= v <= nq // 2
        starts_block1 = v == nk0
        for h in range(A_HEADS):
            g = h // A_GROUP
            if static_block is None:
                q = jnp.where(in1, aq1_ref[0, h], aq0_ref[0, h])
            else:
                q = (aq1_ref if static_block else aq0_ref)[0, h]
            m, acc = states[h]
            if may_start_block1:
                block0_acc[h] = jnp.where(starts_block1, acc, block0_acc[h])
                m = jnp.where(starts_block1, m_init, m)
                acc = jnp.where(starts_block1, acc_init, acc)
            s = _dot_nt(q, ak_ref[0, g, pl.ds(k0, tq), :]) + bias
            states[h] = _softmax_update(s, av_ref[0, g, pl.ds(k0, tq), :], m, acc)
    lane = lax.broadcasted_iota(jnp.int32, (tq, LANES), 1)
    for pair in range(A_HEADS // 2):
        lanes = slice(pair * LANES, (pair + 1) * LANES)
        oa0_ref[0, :, lanes] = _finish_heads(
            block0_acc[2 * pair], block0_acc[2 * pair + 1], lane).astype(oa0_ref.dtype)
        oa1_ref[0, :, lanes] = _finish_heads(
            states[2 * pair][1], states[2 * pair + 1][1], lane).astype(oa1_ref.dtype)
    for blk in blocks:
        _store_heads(blk["ob"], accb_ref.at[blk["idx"]], B_HEADS)


def _attn_call(ik, iq, iw, aq, ak, av, qb, kb, vb, positions, tq, topk):
    b, heads, s, _ = aq.shape
    nq = s // tq
    n_pairs = nq // 2
    assert s % tq == 0 and nq % 2 == 0
    assert nq + 1 == (_COARSE_PASSES + _FINE_PASSES - 1) // _PASSES_PER_CHUNK + 1
    assert (nq + 1) * tq // (2 * SUBLANES) <= 256
    pos_col = positions[:, :, None]
    pos_row = positions[:, None, :]

    def per_seq(*shape):
        nd = len(shape)
        return pl.BlockSpec((1,) + shape, lambda i, j: (i,) + (0,) * nd)

    blk_index = (lambda j: j, lambda j: nq - 1 - j)

    def qhead(h, w, which):
        return pl.BlockSpec((1, h, tq, w), lambda i, j: (i, 0, blk_index[which](j), 0))

    def qrows(w, which):
        return pl.BlockSpec((1, tq, w), lambda i, j: (i, blk_index[which](j), 0))

    def qlanes(which):
        return pl.BlockSpec((1, 1, tq), lambda i, j: (i, 0, blk_index[which](j)))

    in_specs = [
        per_seq(s, IDX_DIM), per_seq(A_KV_HEADS, s, A_HEAD_DIM), per_seq(A_KV_HEADS, s, LANES),
        per_seq(B_HEADS, s, LANES), per_seq(B_HEADS, s, LANES),
        per_seq(s, 1), per_seq(1, s),
        qhead(IDX_HEADS, IDX_DIM, 0), qhead(IDX_HEADS, IDX_DIM, 1),
        qrows(LANES, 0), qrows(LANES, 1),
        qhead(A_HEADS, A_HEAD_DIM, 0), qhead(A_HEADS, A_HEAD_DIM, 1),
        qhead(B_HEADS, LANES, 0), qhead(B_HEADS, LANES, 1),
        qlanes(0), qlanes(1), qrows(1, 0), qrows(1, 1),
    ]
    width = heads * A_HEAD_DIM
    out_sds = jax.ShapeDtypeStruct((b, s // 2, width), BF16)
    out_lo = pl.BlockSpec((1, tq, width), lambda i, j: (i, j, 0))
    out_hi = pl.BlockSpec((1, tq, width), lambda i, j: (i, n_pairs - 1 - j, 0))
    state = pltpu.VMEM((2, heads, tq, LANES), F32)
    return pl.pallas_call(
        functools.partial(_attn_kernel, topk=topk, nq=nq),
        grid=(b, n_pairs),
        in_specs=in_specs,
        out_specs=[out_lo, out_hi, out_lo, out_hi],
        out_shape=[out_sds] * 4,
        scratch_shapes=[
            pltpu.VMEM((nq + 1, tq, tq), F32),
            pltpu.VMEM((nq + 1, tq, tq), BF16),
            pltpu.VMEM((nq + 1, tq, tq), F32),
            pltpu.VMEM((2, IDX_HEADS, tq), F32),
            state, state,
        ],
        compiler_params=pltpu.CompilerParams(
            dimension_semantics=("arbitrary", "arbitrary"),
            vmem_limit_bytes=VMEM_LIMIT_BYTES),
        name="attn",
    )(ik, ak, av, kb, vb, pos_col, pos_row, iq, iq, iw, iw, aq, aq, qb, qb,
      pos_row, pos_row, pos_col, pos_col)


def _shift_rows(u, prev_tail, shift, row8):
    rolled = pltpu.roll(u, shift, axis=0)
    head = jnp.where(row8 < shift, pltpu.roll(prev_tail, shift, axis=0),
                     rolled[:SUBLANES])
    return jnp.concatenate([head, rolled[SUBLANES:]], axis=0)


def _ffn_kernel(x_ref, oa_lo_ref, oa_hi_ref, ob_lo_ref, ob_hi_ref, ga_ref, gb_ref,
                wba_ref, wbb_ref, wout_ref, gffn_ref, wup_ref, cw_ref, cb_ref, wdown_ref,
                gfin_ref, out_ref, tail_ref, *, n_chunks, final_norm):
    j = pl.program_id(1)
    x = x_ref[0]
    tm = x.shape[0]
    first_half = j < pl.num_programs(1) // 2
    oa = jnp.where(first_half, oa_lo_ref[0], oa_hi_ref[0])
    ob = jnp.where(first_half, ob_lo_ref[0], ob_hi_ref[0])
    ya = _dot(oa, wba_ref[...])
    yb = _dot(ob, wbb_ref[...])
    merged = ga_ref[0].astype(F32) * ya + gb_ref[0].astype(F32) * yb
    h1 = x + _dot(merged.astype(BF16), wout_ref[...])
    ms = jnp.mean(h1 * h1, axis=-1, keepdims=True)
    hn = (h1 * lax.rsqrt(ms + EPS) * gffn_ref[...]).astype(BF16)

    fc = wdown_ref.shape[1]
    row8 = lax.broadcasted_iota(jnp.int32, (SUBLANES, fc), 0)

    @pl.when(j == 0)
    def _():
        tail_ref[...] = jnp.zeros_like(tail_ref)

    acc = h1
    for c in range(n_chunks):
        halves = []
        for part in range(2):
            idx = part * n_chunks + c
            u = _dot(hn, wup_ref[:, idx * fc:(idx + 1) * fc])
            prev_tail = tail_ref[idx]
            tail_ref[idx] = u[tm - SUBLANES:]
            y = (cb_ref[idx] + cw_ref[0, idx] * _shift_rows(u, prev_tail, 2, row8)
                 + cw_ref[1, idx] * _shift_rows(u, prev_tail, 1, row8)
                 + cw_ref[2, idx] * u)
            halves.append(y)
        gate, val = halves
        act = (gate * jax.nn.sigmoid(gate) * val).astype(BF16)
        acc = acc + _dot(act, wdown_ref[c])
    if final_norm:
        ms = jnp.mean(acc * acc, axis=-1, keepdims=True)
        acc = acc * lax.rsqrt(ms + EPS) * gfin_ref[...]
    out_ref[0] = acc


def _ffn_call(x, oa_lo, oa_hi, ob_lo, ob_hi, ga, gb, wba, wbb, wout, gffn, wup, cw, cb,
              wdown, gfin, tm, final_norm):
    b, s, d = x.shape
    n_chunks, fc, _ = wdown.shape
    n_parts = 2 * n_chunks
    assert wup.shape == (d, n_parts * fc) and fc % LANES == 0
    n_tiles = s // tm
    assert s % tm == 0 and n_tiles % 2 == 0
    n_half = n_tiles // 2
    tok = lambda w: pl.BlockSpec((1, tm, w), lambda i, j: (i, j, 0))
    lo = lambda w: pl.BlockSpec((1, tm, w), lambda i, j: (i, jnp.minimum(j, n_half - 1), 0))
    hi = lambda w: pl.BlockSpec((1, tm, w), lambda i, j: (i, jnp.maximum(j - n_half, 0), 0))
    wa, wb = oa_lo.shape[-1], ob_lo.shape[-1]
    weights = (wba, wbb, wout, gffn, wup, cw, cb, wdown, gfin)
    return pl.pallas_call(
        functools.partial(_ffn_kernel, n_chunks=n_chunks, final_norm=final_norm),
        grid=(b, n_tiles),
        in_specs=[tok(d), lo(wa), hi(wa), lo(wb), hi(wb), tok(d), tok(d)]
        + [_const_spec(w.shape) for w in weights],
        out_specs=tok(d),
        out_shape=jax.ShapeDtypeStruct((b, s, d), x.dtype),
        scratch_shapes=[pltpu.VMEM((n_parts, SUBLANES, fc), F32)],
        compiler_params=pltpu.CompilerParams(
            dimension_semantics=("arbitrary", "arbitrary"),
            vmem_limit_bytes=VMEM_LIMIT_BYTES),
        name="ffn",
    )(x, oa_lo, oa_hi, ob_lo, ob_hi, ga, gb, *weights)


def _prepare_ffn_weights(w_branch_a, w_branch_b, w_out, norm_ffn_g, w_up, conv_w,
                         conv_b, w_down, norm_final_g, n_chunks):
    d, two_f = w_up.shape
    fc = two_f // (2 * n_chunks)
    wup = w_up.astype(BF16)
    cw = conv_w.reshape(CONV_WIDTH, 2 * n_chunks, 1, fc)
    cb = conv_b.reshape(2 * n_chunks, 1, fc)
    wdown = w_down.reshape(n_chunks, fc, d).astype(BF16)
    return (w_branch_a.astype(BF16), w_branch_b.astype(BF16), w_out.astype(BF16),
            norm_ffn_g.reshape(1, -1), wup, cw, cb, wdown, norm_final_g.reshape(1, -1))


TM_PROJ = 512
TQ_ATTN = 256
TM_FFN = 512
FFN_CHUNKS = 1


def kernel(x, positions, norm_mix_g, w_in, idx_k_norm_g, q_a_norm_g, kv_a_norm_g, w_uq, w_uk, w_uv, w_branch_a, w_branch_b, w_out, norm_ffn_g, w_up, conv_w, conv_b, w_down, norm_final_g):
    b, s, _ = x.shape
    depth = w_in.shape[0]
    topk = min(INDEX_TOPK_MAX, s // 4)
    trig = _rope_trig(positions)
    expand = _rope_expansion()
    h = x
    for l in range(depth):
        pw = _prepare_proj_weights(norm_mix_g[l], w_in[l], idx_k_norm_g[l], q_a_norm_g[l],
                                   kv_a_norm_g[l], w_uq[l], w_uk[l], w_uv[l])
        aq, ak, av, iq, ik, iw, qb, kb, vb, ga, gb = _proj_call(h, trig, expand, *pw, tm=TM_PROJ)
        mixed = _attn_call(ik, iq, iw, aq, ak, av, qb, kb, vb, positions, TQ_ATTN, topk)
        fw = _prepare_ffn_weights(w_branch_a[l], w_branch_b[l], w_out[l], norm_ffn_g[l],
                                  w_up[l], conv_w[l], conv_b[l], w_down[l], norm_final_g,
                                  FFN_CHUNKS)
        h = _ffn_call(h, *mixed, ga, gb, *fw, tm=TM_FFN, final_norm=(l == depth - 1))
    return h
```

```python
import functools

import jax
import jax.numpy as jnp
import numpy as np
from jax import lax
from jax.experimental import pallas as pl
from jax.experimental.pallas import tpu as pltpu

ROPE_THETA = 500000.0
EPS = 1e-6
A_HEADS = 8
A_KV_HEADS = 2
A_GROUP = A_HEADS // A_KV_HEADS
A_HEAD_DIM = 64
A_ROT_DIM = 16
IDX_HEADS = 8
IDX_DIM = 64
INDEX_TOPK_MAX = 256
B_HEADS = 8
B_Q_RANK = 256
B_KV_RANK = 128
B_NOPE_DIM = 64
B_ROPE_DIM = 32
B_V_DIM = 64
CONV_WIDTH = 3

LANES = 128
SUBLANES = 8
VMEM_LIMIT_BYTES = 56 * 1024 * 1024

_T_IW = IDX_DIM + B_ROPE_DIM
assert IDX_DIM == B_NOPE_DIM and _T_IW + IDX_HEADS <= LANES


def _proj_layout(d_model):
    widths = dict(aq=A_HEADS * A_HEAD_DIM, ak=A_KV_HEADS * A_HEAD_DIM,
                  av=A_KV_HEADS * A_HEAD_DIM, iq=IDX_HEADS * IDX_DIM, cq=B_Q_RANK,
                  ckv=B_KV_RANK, ga=d_model, gb=d_model, tail=LANES)
    offsets, total = {}, 0
    for name, width in widths.items():
        assert width % LANES == 0
        offsets[name] = total
        total += width
    return offsets, total

F32 = jnp.float32
BF16 = jnp.bfloat16
NEG_INF = float("-inf")
LOG2_E = 1.4426950408889634
F32_LOWEST = float(np.finfo(np.float32).min)


def _dot(a, b):
    return jnp.dot(a, b, preferred_element_type=F32)


def _dot_nt(a, b):
    return lax.dot_general(a, b, (((1,), (1,)), ((), ())), preferred_element_type=F32)


def _rope_lanes(x, cos_t, sin_t, take_up, half):
    up = pltpu.roll(x, LANES - half, axis=1)
    dn = pltpu.roll(x, half, axis=1)
    return x * cos_t + jnp.where(take_up, up, dn) * sin_t


def _expand_rope_tables(trig_t, expand_ref):
    tm = trig_t[0].shape[1]
    used = sum(piece.shape[0] for piece in trig_t)
    rest = lax.broadcasted_iota(jnp.int32, (LANES - used, tm), 0)
    ones_then_zeros = jnp.where(rest == 0, 1.0, 0.0)
    t = jnp.concatenate(list(trig_t) + [ones_then_zeros], axis=0).T
    a1 = t.astype(BF16)
    r1 = t - a1.astype(F32)
    a2 = r1.astype(BF16)
    a3 = (r1 - a2.astype(F32)).astype(BF16)
    tables = _dot(jnp.concatenate([a1, a2, a3], axis=1), expand_ref[...])
    return [tables[:, i * LANES:(i + 1) * LANES] for i in range(4)]


def _proj_kernel(x_ref, ca_ref, sa_ref, cb_ref, sb_ref, expand_ref, gmix_ref, w_ref,
                 gik_ref, gq_ref, gkv_ref, wuq_ref, wuk_ref, wuv_ref,
                 aq_ref, ak_ref, av_ref, iq_ref, ik_ref, iw_ref, qb_ref, kb_ref,
                 vb_ref, ga_ref, gb_ref, *, offs):
    x = x_ref[0]
    tm, d_model = x.shape
    ms = jnp.mean(x * x, axis=-1, keepdims=True)
    hn = (x * lax.rsqrt(ms + EPS) * gmix_ref[...]).astype(BF16)

    lane = lax.broadcasted_iota(jnp.int32, (tm, LANES), 1)
    ca, sa, cb, sb = _expand_rope_tables(
        (ca_ref[0], sa_ref[0], cb_ref[0], sb_ref[0]), expand_ref)
    up_a = (lane % A_HEAD_DIM) < (A_ROT_DIM // 2)
    up_b = lane < (B_NOPE_DIM + B_ROPE_DIM // 2)
    v_one = jnp.where(lane == A_HEAD_DIM, 1.0, 0.0).astype(F32)
    low_half = lane < A_HEAD_DIM

    def proj(off, width):
        return _dot(hn, w_ref[:, off:off + width])

    def store_heads(ref, tile128, first_head, scale):
        t = tile128 * scale if scale != 1.0 else tile128
        ref[0, first_head] = t[:, :A_HEAD_DIM].astype(ref.dtype)
        ref[0, first_head + 1] = t[:, A_HEAD_DIM:].astype(ref.dtype)

    assert A_HEADS * A_HEAD_DIM == IDX_HEADS * IDX_DIM and A_HEAD_DIM == IDX_DIM
    n_head_pairs = A_HEADS // 2
    for off, ref, scale in ((offs["aq"], aq_ref, A_HEAD_DIM ** -0.5 * LOG2_E),
                            (offs["iq"], iq_ref, IDX_DIM ** -0.5)):
        full = proj(off, n_head_pairs * LANES)
        for j in range(n_head_pairs):
            t = _rope_lanes(full[:, j * LANES:(j + 1) * LANES], ca, sa, up_a,
                            A_ROT_DIM // 2)
            store_heads(ref, t, 2 * j, scale)

    assert A_KV_HEADS * A_HEAD_DIM == LANES and offs["av"] == offs["ak"] + LANES
    akv = proj(offs["ak"], 2 * LANES)
    store_heads(ak_ref, _rope_lanes(akv[:, :LANES], ca, sa, up_a, A_ROT_DIM // 2),
                0, 1.0)
    v2 = akv[:, LANES:]
    av_ref[0, 0] = (jnp.where(low_half, v2, 0.0) + v_one).astype(BF16)
    v2r = pltpu.roll(v2, A_HEAD_DIM, axis=1)
    av_ref[0, 1] = (jnp.where(low_half, v2r, 0.0) + v_one).astype(BF16)

    tail = proj(offs["tail"], LANES)
    ik_sq = jnp.where(low_half, tail * tail, 0.0)
    ik_ms = jnp.sum(ik_sq, axis=-1, keepdims=True) * (1.0 / IDX_DIM)
    ik_n = tail * lax.rsqrt(ik_ms + EPS) * gik_ref[...]
    ik_r = _rope_lanes(ik_n, ca, sa, up_a, A_ROT_DIM // 2)
    ik_ref[0] = ik_r[:, :IDX_DIM].astype(BF16)
    iw_ref[0] = tail * (IDX_HEADS ** -0.5)
    kr = _rope_lanes(tail, cb, sb, up_b, B_ROPE_DIM // 2)
    kr = jnp.where((lane >= B_NOPE_DIM) & (lane < B_NOPE_DIM + B_ROPE_DIM), kr, 0.0)

    cq = proj(offs["cq"], B_Q_RANK)
    cq_n = (cq * lax.rsqrt(jnp.mean(cq * cq, axis=-1, keepdims=True) + EPS)
            * gq_ref[...]).astype(BF16)
    ckv = proj(offs["ckv"], B_KV_RANK)
    ckv_n = (ckv * lax.rsqrt(jnp.mean(ckv * ckv, axis=-1, keepdims=True) + EPS)
             * gkv_ref[...]).astype(BF16)
    b_scale = (B_NOPE_DIM + B_ROPE_DIM) ** -0.5 * LOG2_E
    for h0 in range(0, B_HEADS, 2):
        cols = slice(h0 * LANES, (h0 + 2) * LANES)
        q2 = _dot(cq_n, wuq_ref[:, cols])
        k2 = _dot(ckv_n, wuk_ref[:, cols])
        v2 = _dot(ckv_n, wuv_ref[:, cols])
        for i in range(2):
            lanes = slice(i * LANES, (i + 1) * LANES)
            q = _rope_lanes(q2[:, lanes], cb, sb, up_b, B_ROPE_DIM // 2) * b_scale
            qb_ref[0, h0 + i] = q.astype(BF16)
            kb_ref[0, h0 + i] = (k2[:, lanes] + kr).astype(BF16)
            vb_ref[0, h0 + i] = (v2[:, lanes] + v_one).astype(BF16)

    ga_ref[0] = jax.nn.sigmoid(proj(offs["ga"], d_model)).astype(BF16)
    gb_ref[0] = jax.nn.sigmoid(proj(offs["gb"], d_model)).astype(BF16)


_R_CA = 0
_R_SA = _R_CA + A_ROT_DIM // 2
_R_CB = _R_SA + A_ROT_DIM // 2
_R_SB = _R_CB + B_ROPE_DIM // 2
_R_ONE = _R_SB + B_ROPE_DIM // 2


def _rope_trig(positions):
    pos = positions.astype(F32)[:, None, :]

    def cs(rot):
        inv = ROPE_THETA ** (-jnp.arange(0, rot, 2, dtype=F32) / rot)
        ang = pos * inv[None, :, None]
        return jnp.cos(ang), jnp.sin(ang)

    return cs(A_ROT_DIM) + cs(B_ROPE_DIM)


def _rope_expansion():
    e = np.zeros((LANES, 4 * LANES), np.float32)
    half_a, half_b = A_ROT_DIM // 2, B_ROPE_DIM // 2
    for j in range(LANES):
        d = j % A_HEAD_DIM
        if d < A_ROT_DIM:
            e[_R_CA + d % half_a, j] = 1.0
            e[_R_SA + d % half_a, LANES + j] = -1.0 if d < half_a else 1.0
        else:
            e[_R_ONE, j] = 1.0
        d = j - B_NOPE_DIM
        if 0 <= d < B_ROPE_DIM:
            e[_R_CB + d % half_b, 2 * LANES + j] = 1.0
            e[_R_SB + d % half_b, 3 * LANES + j] = -1.0 if d < half_b else 1.0
        else:
            e[_R_ONE, 2 * LANES + j] = 1.0
    return jnp.asarray(np.concatenate([e, e, e], axis=0), BF16)


def _regroup_w_in(w_in):
    d_model = w_in.shape[0]
    a_kv = A_KV_HEADS * A_HEAD_DIM
    splits = [A_HEADS * A_HEAD_DIM, a_kv, a_kv, IDX_HEADS * IDX_DIM, IDX_DIM, IDX_HEADS,
              B_Q_RANK, B_KV_RANK, B_ROPE_DIM, d_model, d_model]
    assert sum(splits) == w_in.shape[1]
    offs = np.concatenate([[0], np.cumsum(splits)])
    seg = [w_in[:, offs[i]:offs[i + 1]] for i in range(len(splits))]
    a_q, a_k, a_v, i_q, i_k, i_w, b_cq, b_ckv, b_kr, g_a, g_b = seg
    pad = jnp.zeros((d_model, LANES - _T_IW - IDX_HEADS), w_in.dtype)
    return jnp.concatenate(
        [a_q, a_k, a_v, i_q, b_cq, b_ckv, g_a, g_b, i_k, b_kr, i_w, pad], axis=1)


def _pad_heads(w, width):
    r, h, d = w.shape
    return jnp.pad(w, ((0, 0), (0, 0), (0, width - d))).reshape(r, h * width)


def _const_spec(shape):
    nd = len(shape)
    return pl.BlockSpec(shape, lambda *_: (0,) * nd)


def _proj_call(x, trig, expand, gmix, wcat, gik, gq, gkv, wuq, wuk, wuv, tm):
    b, s, d = x.shape
    grid = (b, s // tm)
    offs, total = _proj_layout(d)
    assert wcat.shape == (d, total)
    tok = lambda w: pl.BlockSpec((1, tm, w), lambda i, j: (i, j, 0))
    head = lambda h, w: pl.BlockSpec((1, h, tm, w), lambda i, j: (i, 0, j, 0))
    seq_on_lanes = lambda t: pl.BlockSpec((1, t.shape[1], tm), lambda i, j: (i, 0, j))
    consts = (expand, gmix, wcat, gik, gq, gkv, wuq, wuk, wuv)
    in_specs = ([tok(d)] + [seq_on_lanes(t) for t in trig]
                + [_const_spec(c.shape) for c in consts])
    sds = jax.ShapeDtypeStruct
    out_shape = [
        sds((b, A_HEADS, s, A_HEAD_DIM), BF16),
        sds((b, A_KV_HEADS, s, A_HEAD_DIM), BF16),
        sds((b, A_KV_HEADS, s, LANES), BF16),
        sds((b, IDX_HEADS, s, IDX_DIM), BF16),
        sds((b, s, IDX_DIM), BF16),
        sds((b, s, LANES), F32),
        sds((b, B_HEADS, s, LANES), BF16),
        sds((b, B_HEADS, s, LANES), BF16),
        sds((b, B_HEADS, s, LANES), BF16),
        sds((b, s, d), BF16),
        sds((b, s, d), BF16),
    ]
    out_specs = [
        head(A_HEADS, A_HEAD_DIM), head(A_KV_HEADS, A_HEAD_DIM),
        head(A_KV_HEADS, LANES), head(IDX_HEADS, IDX_DIM), tok(IDX_DIM),
        tok(LANES), head(B_HEADS, LANES), head(B_HEADS, LANES),
        head(B_HEADS, LANES), tok(d), tok(d)]
    return pl.pallas_call(
        functools.partial(_proj_kernel, offs=offs),
        grid=grid,
        in_specs=in_specs,
        out_specs=out_specs,
        out_shape=out_shape,
        compiler_params=pltpu.CompilerParams(
            dimension_semantics=("arbitrary", "arbitrary"),
            vmem_limit_bytes=VMEM_LIMIT_BYTES),
        name="proj",
    )(x, *trig, *consts)

def _prepare_proj_weights(norm_mix_g, w_in, idx_k_norm_g, q_a_norm_g, kv_a_norm_g,
                          w_uq, w_uk, w_uv):
    gmix = norm_mix_g.reshape(1, -1)
    wcat = _regroup_w_in(w_in).astype(BF16)
    gik = jnp.pad(idx_k_norm_g, (0, LANES - IDX_DIM)).reshape(1, LANES)
    gq = q_a_norm_g.reshape(1, -1)
    gkv = kv_a_norm_g.reshape(1, -1)
    wuq = _pad_heads(w_uq, LANES).astype(BF16)
    wuk = _pad_heads(w_uk, LANES).astype(BF16)
    wuv = _pad_heads(w_uv, LANES).astype(BF16)
    return gmix, wcat, gik, gq, gkv, wuq, wuk, wuv


_INT_MIN = -2 ** 31
_ORD_NEG_INF = 0x007FFFFF
_ORD_LOWEST_FINITE = 0x00800000


def _ordinal_to_float(u):
    o = u ^ jnp.int32(_INT_MIN)
    bits = o ^ ((o >> 31) & jnp.int32(0x7FFFFFFF))
    return lax.bitcast_convert_type(bits, F32)


def _softmax_update(s, v, m_old, acc, rows_may_be_empty=True):
    m_new = jnp.maximum(m_old, jnp.max(s, axis=-1, keepdims=True))
    if rows_may_be_empty:
        m_new = jnp.maximum(m_new, F32_LOWEST)
    p = jnp.concatenate(
        [jnp.exp2(s[:, j * LANES:(j + 1) * LANES] - m_new)
         for j in range(s.shape[1] // LANES)], axis=1)
    alpha = jnp.exp2(m_old - m_new)
    return m_new, alpha * acc + _dot(p.astype(BF16), v)


def _attend_chunks(n_heads, m_ref, acc_ref, chunk_fns):
    for h in range(n_heads):
        m, acc = m_ref[h], acc_ref[h]
        for fn in chunk_fns:
            m, acc = _softmax_update(*fn(h), m, acc)
        m_ref[h] = m
        acc_ref[h] = acc


def _for_chunk_groups(n, body):
    def group(i, carry):
        body(tuple(4 * i + r for r in range(4)))
        return carry

    lax.fori_loop(0, lax.shift_right_logical(n, 2), group, 0)
    done = n & ~3

    @pl.when((n & 2) != 0)
    def _():
        body((done, done + 1))

    @pl.when((n & 1) != 0)
    def _():
        body((n - 1,))


def _init_softmax_state(m_ref, acc_ref):
    m_ref[...] = jnp.full(m_ref.shape, NEG_INF, F32)
    acc_ref[...] = jnp.zeros(acc_ref.shape, F32)


def _finish_heads(acc_even, acc_odd, lane):
    o0 = acc_even * (1.0 / acc_even[:, B_V_DIM:B_V_DIM + 1])
    o1 = acc_odd * (1.0 / acc_odd[:, B_V_DIM:B_V_DIM + 1])
    return jnp.where(lane < B_V_DIM, o0, pltpu.roll(o1, B_V_DIM, axis=1))


def _store_heads(o_ref, acc_ref, n_heads):
    lane = lax.broadcasted_iota(jnp.int32, acc_ref.shape[1:], 1)
    for pair in range(n_heads // 2):
        o_ref[0, :, pair * LANES:(pair + 1) * LANES] = _finish_heads(
            acc_ref[2 * pair], acc_ref[2 * pair + 1], lane).astype(o_ref.dtype)


_COARSE_PASSES = 16
_FINE_PASSES = 17
_PASSES_PER_CHUNK = 4


def _attn_kernel(ik_ref, ak_ref, av_ref, kb_ref, vb_ref, posk_col_ref, posk_row_ref,
                 iq0_ref, iq1_ref, iw0_ref, iw1_ref, aq0_ref, aq1_ref, qb0_ref, qb1_ref,
                 pqr0_ref, pqr1_ref, pqc0_ref, pqc1_ref,
                 oa0_ref, oa1_ref, ob0_ref, ob1_ref,
                 isc_ref, xb_ref, bias_ref, wt_ref, ma_ref, acca_ref, mb_ref, accb_ref,
                 *, topk, nq):
    pi = pl.program_id(1)
    tq = aq0_ref.shape[2]
    half = tq // 2
    nk0 = pi + 1
    n_virtual = nq + 1
    blocks = (
        dict(idx=0, nk=nk0, off=0, iq=iq0_ref, iw=iw0_ref, aq=aq0_ref, pqr=pqr0_ref,
             oa=oa0_ref, ob=ob0_ref),
        dict(idx=1, nk=nq - pi, off=nk0, iq=iq1_ref, iw=iw1_ref, aq=aq1_ref, pqr=pqr1_ref,
             oa=oa1_ref, ob=ob1_ref),
    )

    for blk in blocks:
        wt_ref[blk["idx"]] = blk["iw"][0].T[_T_IW:_T_IW + IDX_HEADS, :]

    def pick(v, of_block0, of_block1):
        if v == 0:
            return of_block0()
        if v >= nq // 2:
            return of_block1()
        return jnp.where(v >= nk0, of_block1(), of_block0())

    def route(v, part, acc0, acc1):
        if v == 0:
            return acc0 + part, acc1
        if v >= nq // 2:
            return acc0, acc1 + part
        in1 = v >= nk0
        zero = jnp.zeros_like(part)
        return acc0 + jnp.where(in1, zero, part), acc1 + jnp.where(in1, part, zero)

    def idx_chunk(v):
        c = pick(v, lambda: jnp.int32(v), lambda: v - nk0)
        posq = pick(v, lambda: pqr0_ref[0], lambda: pqr1_ref[0])
        wt = pick(v, lambda: wt_ref[0], lambda: wt_ref[1])
        iqs = [pick(v, lambda: iq0_ref[0, h], lambda: iq1_ref[0, h])
               for h in range(IDX_HEADS)]
        for sub in range(2):
            k0 = pl.multiple_of(c * tq + sub * half, half)
            ikc = ik_ref[0, pl.ds(k0, half), :]
            acc = jnp.zeros((half, tq), F32)
            for h in range(IDX_HEADS):
                d = _dot_nt(ikc, iqs[h])
                acc = acc + wt[h:h + 1, :] * jnp.maximum(d, 0.0)
            causal = posk_col_ref[0, pl.ds(k0, half), :] <= posq
            score = jnp.where(causal, acc, NEG_INF)
            isc_ref[v, sub * half:(sub + 1) * half, :] = score
            xb_ref[v, sub * half:(sub + 1) * half, :] = score.astype(BF16)

    for v in range(n_virtual):
        idx_chunk(v)

    _init_softmax_state(mb_ref, accb_ref)

    def latent_chunk(v):
        diagonal = v in (0, n_virtual - 1)
        if v == 0:
            block, c = 0, nk0 - 1
        elif v >= nq // 2:
            block, c = 1, v - nk0
        else:
            block = None
            in1 = v >= nk0
            c = jnp.where(in1, v - nk0, v - 1)
            b = in1.astype(jnp.int32)
        if block is not None:
            b = block
            q_ref, pqc_ref = (qb1_ref, pqc1_ref) if block else (qb0_ref, pqc0_ref)
        k0 = pl.multiple_of(c * tq, tq)
        if diagonal:
            causal = posk_row_ref[0, :, pl.ds(k0, tq)] <= pqc_ref[0]
        for h in range(B_HEADS):
            if block is not None:
                q = q_ref[0, h]
            else:
                q = jnp.where(in1, qb1_ref[0, h], qb0_ref[0, h])
            s = _dot_nt(q, kb_ref[0, h, pl.ds(k0, tq), :])
            if diagonal:
                s = jnp.where(causal, s, NEG_INF)
            m, acc = _softmax_update(s, vb_ref[0, h, pl.ds(k0, tq), :],
                                     mb_ref[b, h], accb_ref[b, h], rows_may_be_empty=False)
            mb_ref[b, h] = m
            accb_ref[b, h] = acc

    packed = 2 * SUBLANES
    one_b, zero_b = jnp.asarray(1, BF16), jnp.asarray(0, BF16)

    def count_ge(thr0, thr1, coarse):
        rows = packed if coarse else SUBLANES
        dtype = BF16 if coarse else F32
        acc0 = jnp.zeros((rows, tq), dtype)
        acc1 = jnp.zeros((rows, tq), dtype)
        for v in range(n_virtual):
            thr = pick(v, lambda: thr0, lambda: thr1)
            if coarse:
                hit = jnp.where(xb_ref[v] >= thr, one_b, zero_b)
            else:
                hit = jnp.where(isc_ref[v] >= thr, 1.0, 0.0)
            part = hit[:rows]
            for r in range(1, tq // rows):
                part = part + hit[r * rows:(r + 1) * rows]
            acc0, acc1 = route(v, part, acc0, acc1)
        return (jnp.sum(acc0.astype(F32), axis=0, keepdims=True),
                jnp.sum(acc1.astype(F32), axis=0, keepdims=True))

    def coarse_ordinal(j):
        low = jnp.where(j < 0x8000, jnp.int32(0xFFFF), jnp.int32(0))
        return lax.shift_left(j, 16) | low

    def coarse_pass(it, js):
        cands = [j | lax.shift_left(jnp.int32(1), 15 - it) for j in js]
        thrs = [_ordinal_to_float(coarse_ordinal(c)).astype(BF16) for c in cands]
        cnts = count_ge(*thrs, coarse=True)
        return tuple(jnp.where(n >= topk, c, j) for n, c, j in zip(cnts, cands, js))

    def fine_pass(it, ds, bases):
        cands = [d | lax.shift_left(jnp.int32(1), 16 - it) for d in ds]
        thrs = [_ordinal_to_float(b + c) for b, c in zip(bases, cands)]
        cnts = count_ge(*thrs, coarse=False)
        return tuple(jnp.where(n >= topk, c, d) for n, c, d in zip(cnts, cands, ds))

    zeros = (jnp.zeros((1, tq), jnp.int32),) * 2
    first_fine = _COARSE_PASSES // _PASSES_PER_CHUNK
    js = zeros
    for v in range(first_fine):
        latent_chunk(v)
        for r in range(_PASSES_PER_CHUNK):
            js = coarse_pass(_PASSES_PER_CHUNK * v + r, js)
    js = [jnp.maximum(j, _ORD_NEG_INF >> 16) for j in js]
    bases = [coarse_ordinal(j) - jnp.int32(1 << 16) for j in js]
    ds = zeros
    for v in range(first_fine, n_virtual):
        latent_chunk(v)
        first_pass = _PASSES_PER_CHUNK * (v - first_fine)
        for it in range(first_pass, min(first_pass + _PASSES_PER_CHUNK, _FINE_PASSES)):
            ds = fine_pass(it, ds, bases)

    t_us = [base + d for base, d in zip(bases, ds)]
    shorts = [(t_u >= 0) & (t_u < _ORD_LOWEST_FINITE) for t_u in t_us]
    thrs = [_ordinal_to_float(jnp.where(short, jnp.int32(_ORD_LOWEST_FINITE), t_u))
            for short, t_u in zip(shorts, t_us)]
    n_ges = [jnp.zeros((1, tq), F32)] * 2
    for v in range(n_virtual):
        sel = isc_ref[v] >= pick(v, lambda: thrs[0], lambda: thrs[1])
        bias_ref[v] = jnp.where(sel, 0.0, NEG_INF).T
        cnt = jnp.sum(jnp.where(sel, 1.0, 0.0), axis=0, keepdims=True)
        n_ges = route(v, cnt, *n_ges)

    for blk, thr, short, n_ge in zip(blocks, thrs, shorts, n_ges):
        off, nk = blk["off"], blk["nk"]
        tied = jnp.logical_and(n_ge > topk, jnp.logical_not(short))

        @pl.when(jnp.max(jnp.where(tied, 1.0, 0.0)) > 0.0)
        def _(thr=thr, off=off, nk=nk):
            def count_gt(c, acc):
                hit = jnp.where(isc_ref[off + c] > thr, 1.0, 0.0)
                return acc + jnp.sum(hit, axis=0, keepdims=True)
            n_gt = lax.fori_loop(0, nk, count_gt, jnp.zeros((1, tq), F32))
            need = topk - n_gt
            row = lax.broadcasted_iota(jnp.int32, (tq, tq), 0)
            col = lax.broadcasted_iota(jnp.int32, (tq, tq), 1)
            below = jnp.where(col < row, 1.0, 0.0).astype(BF16)

            def fix(c, seen):
                x = isc_ref[off + c]
                eq = jnp.where(x == thr, 1.0, 0.0)
                rank = _dot(below, eq.astype(BF16)) + seen
                sel = (x > thr) | ((x == thr) & (rank < need))
                bias_ref[off + c] = jnp.where(sel, 0.0, NEG_INF).T
                return seen + jnp.sum(eq, axis=0, keepdims=True)

            lax.fori_loop(0, nk, fix, jnp.zeros((1, tq), F32))

    _init_softmax_state(ma_ref, acca_ref)
    for blk in blocks:
        def attend(chunks, blk=blk):
            def chunk_fn(c):
                k0 = pl.multiple_of(c * tq, tq)
                bias = bias_ref[blk["off"] + c]

                def fn(h):
                    g = h // A_GROUP
                    s = _dot_nt(blk["aq"][0, h], ak_ref[0, g, pl.ds(k0, tq), :]) + bias
                    return s, av_ref[0, g, pl.ds(k0, tq), :]
                return fn

            _attend_chunks(A_HEADS, ma_ref.at[blk["idx"]], acca_ref.at[blk["idx"]],
                           [chunk_fn(c) for c in chunks])

        _for_chunk_groups(blk["nk"], attend)
        _store_heads(blk["oa"], acca_ref.at[blk["idx"]], A_HEADS)
        _store_heads(blk["ob"], accb_ref.at[blk["idx"]], B_HEADS)


def _attn_call(ik, iq, iw, aq, ak, av, qb, kb, vb, positions, tq, topk):
    b, heads, s, _ = aq.shape
    nq = s // tq
    n_pairs = nq // 2
    assert s % tq == 0 and nq % 2 == 0
    assert nq + 1 == (_COARSE_PASSES + _FINE_PASSES - 1) // _PASSES_PER_CHUNK + 1
    assert (nq + 1) * tq // (2 * SUBLANES) <= 256
    pos_col = positions[:, :, None]
    pos_row = positions[:, None, :]

    def per_seq(*shape):
        nd = len(shape)
        return pl.BlockSpec((1,) + shape, lambda i, j: (i,) + (0,) * nd)

    blk_index = (lambda j: j, lambda j: nq - 1 - j)

    def qhead(h, w, which):
        return pl.BlockSpec((1, h, tq, w), lambda i, j: (i, 0, blk_index[which](j), 0))

    def qrows(w, which):
        return pl.BlockSpec((1, tq, w), lambda i, j: (i, blk_index[which](j), 0))

    def qlanes(which):
        return pl.BlockSpec((1, 1, tq), lambda i, j: (i, 0, blk_index[which](j)))

    in_specs = [
        per_seq(s, IDX_DIM), per_seq(A_KV_HEADS, s, A_HEAD_DIM), per_seq(A_KV_HEADS, s, LANES),
        per_seq(B_HEADS, s, LANES), per_seq(B_HEADS, s, LANES),
        per_seq(s, 1), per_seq(1, s),
        qhead(IDX_HEADS, IDX_DIM, 0), qhead(IDX_HEADS, IDX_DIM, 1),
        qrows(LANES, 0), qrows(LANES, 1),
        qhead(A_HEADS, A_HEAD_DIM, 0), qhead(A_HEADS, A_HEAD_DIM, 1),
        qhead(B_HEADS, LANES, 0), qhead(B_HEADS, LANES, 1),
        qlanes(0), qlanes(1), qrows(1, 0), qrows(1, 1),
    ]
    width = heads * A_HEAD_DIM
    out_sds = jax.ShapeDtypeStruct((b, s // 2, width), BF16)
    out_lo = pl.BlockSpec((1, tq, width), lambda i, j: (i, j, 0))
    out_hi = pl.BlockSpec((1, tq, width), lambda i, j: (i, n_pairs - 1 - j, 0))
    state = pltpu.VMEM((2, heads, tq, LANES), F32)
    return pl.pallas_call(
        functools.partial(_attn_kernel, topk=topk, nq=nq),
        grid=(b, n_pairs),
        in_specs=in_specs,
        out_specs=[out_lo, out_hi, out_lo, out_hi],
        out_shape=[out_sds] * 4,
        scratch_shapes=[
            pltpu.VMEM((nq + 1, tq, tq), F32),
            pltpu.VMEM((nq + 1, tq, tq), BF16),
            pltpu.VMEM((nq + 1, tq, tq), F32),
            pltpu.VMEM((2, IDX_HEADS, tq), F32),
            state, state, state, state,
        ],
        compiler_params=pltpu.CompilerParams(
            dimension_semantics=("arbitrary", "arbitrary"),
            vmem_limit_bytes=VMEM_LIMIT_BYTES),
        name="attn",
    )(ik, ak, av, kb, vb, pos_col, pos_row, iq, iq, iw, iw, aq, aq, qb, qb,
      pos_row, pos_row, pos_col, pos_col)


def _shift_rows(u, prev_tail, shift, row8):
    rolled = pltpu.roll(u, shift, axis=0)
    head = jnp.where(row8 < shift, pltpu.roll(prev_tail, shift, axis=0),
                     rolled[:SUBLANES])
    return jnp.concatenate([head, rolled[SUBLANES:]], axis=0)


def _ffn_kernel(x_ref, oa_lo_ref, oa_hi_ref, ob_lo_ref, ob_hi_ref, ga_ref, gb_ref,
                wba_ref, wbb_ref, wout_ref, gffn_ref, wup_ref, cw_ref, cb_ref, wdown_ref,
                gfin_ref, out_ref, tail_ref, *, n_chunks, final_norm):
    j = pl.program_id(1)
    x = x_ref[0]
    tm = x.shape[0]
    first_half = j < pl.num_programs(1) // 2
    oa = jnp.where(first_half, oa_lo_ref[0], oa_hi_ref[0])
    ob = jnp.where(first_half, ob_lo_ref[0], ob_hi_ref[0])
    ya = _dot(oa, wba_ref[...])
    yb = _dot(ob, wbb_ref[...])
    merged = ga_ref[0].astype(F32) * ya + gb_ref[0].astype(F32) * yb
    h1 = x + _dot(merged.astype(BF16), wout_ref[...])
    ms = jnp.mean(h1 * h1, axis=-1, keepdims=True)
    hn = (h1 * lax.rsqrt(ms + EPS) * gffn_ref[...]).astype(BF16)

    fc = wdown_ref.shape[1]
    row8 = lax.broadcasted_iota(jnp.int32, (SUBLANES, fc), 0)

    @pl.when(j == 0)
    def _():
        tail_ref[...] = jnp.zeros_like(tail_ref)

    acc = h1
    for c in range(n_chunks):
        halves = []
        for part in range(2):
            idx = part * n_chunks + c
            u = _dot(hn, wup_ref[:, idx * fc:(idx + 1) * fc])
            prev_tail = tail_ref[idx]
            tail_ref[idx] = u[tm - SUBLANES:]
            y = (cb_ref[idx] + cw_ref[0, idx] * _shift_rows(u, prev_tail, 2, row8)
                 + cw_ref[1, idx] * _shift_rows(u, prev_tail, 1, row8)
                 + cw_ref[2, idx] * u)
            halves.append(y)
        gate, val = halves
        act = (gate * jax.nn.sigmoid(gate) * val).astype(BF16)
        acc = acc + _dot(act, wdown_ref[c])
    if final_norm:
        ms = jnp.mean(acc * acc, axis=-1, keepdims=True)
        acc = acc * lax.rsqrt(ms + EPS) * gfin_ref[...]
    out_ref[0] = acc


def _ffn_call(x, oa_lo, oa_hi, ob_lo, ob_hi, ga, gb, wba, wbb, wout, gffn, wup, cw, cb,
              wdown, gfin, tm, final_norm):
    b, s, d = x.shape
    n_chunks, fc, _ = wdown.shape
    n_parts = 2 * n_chunks
    assert wup.shape == (d, n_parts * fc) and fc % LANES == 0
    n_tiles = s // tm
    assert s % tm == 0 and n_tiles % 2 == 0
    n_half = n_tiles // 2
    tok = lambda w: pl.BlockSpec((1, tm, w), lambda i, j: (i, j, 0))
    lo = lambda w: pl.BlockSpec((1, tm, w), lambda i, j: (i, jnp.minimum(j, n_half - 1), 0))
    hi = lambda w: pl.BlockSpec((1, tm, w), lambda i, j: (i, jnp.maximum(j - n_half, 0), 0))
    wa, wb = oa_lo.shape[-1], ob_lo.shape[-1]
    weights = (wba, wbb, wout, gffn, wup, cw, cb, wdown, gfin)
    return pl.pallas_call(
        functools.partial(_ffn_kernel, n_chunks=n_chunks, final_norm=final_norm),
        grid=(b, n_tiles),
        in_specs=[tok(d), lo(wa), hi(wa), lo(wb), hi(wb), tok(d), tok(d)]
        + [_const_spec(w.shape) for w in weights],
        out_specs=tok(d),
        out_shape=jax.ShapeDtypeStruct((b, s, d), x.dtype),
        scratch_shapes=[pltpu.VMEM((n_parts, SUBLANES, fc), F32)],
        compiler_params=pltpu.CompilerParams(
            dimension_semantics=("arbitrary", "arbitrary"),
            vmem_limit_bytes=VMEM_LIMIT_BYTES),
        name="ffn",
    )(x, oa_lo, oa_hi, ob_lo, ob_hi, ga, gb, *weights)


def _prepare_ffn_weights(w_branch_a, w_branch_b, w_out, norm_ffn_g, w_up, conv_w,
                         conv_b, w_down, norm_final_g, n_chunks):
    d, two_f = w_up.shape
    fc = two_f // (2 * n_chunks)
    wup = w_up.astype(BF16)
    cw = conv_w.reshape(CONV_WIDTH, 2 * n_chunks, 1, fc)
    cb = conv_b.reshape(2 * n_chunks, 1, fc)
    wdown = w_down.reshape(n_chunks, fc, d).astype(BF16)
    return (w_branch_a.astype(BF16), w_branch_b.astype(BF16), w_out.astype(BF16),
            norm_ffn_g.reshape(1, -1), wup, cw, cb, wdown, norm_final_g.reshape(1, -1))


TM_PROJ = 512
TQ_ATTN = 256
TM_FFN = 512
FFN_CHUNKS = 1


def kernel(x, positions, norm_mix_g, w_in, idx_k_norm_g, q_a_norm_g, kv_a_norm_g, w_uq, w_uk, w_uv, w_branch_a, w_branch_b, w_out, norm_ffn_g, w_up, conv_w, conv_b, w_down, norm_final_g):
    b, s, _ = x.shape
    depth = w_in.shape[0]
    topk = min(INDEX_TOPK_MAX, s // 4)
    trig = _rope_trig(positions)
    expand = _rope_expansion()
    h = x
    for l in range(depth):
        pw = _prepare_proj_weights(norm_mix_g[l], w_in[l], idx_k_norm_g[l], q_a_norm_g[l],
                                   kv_a_norm_g[l], w_uq[l], w_uk[l], w_uv[l])
        aq, ak, av, iq, ik, iw, qb, kb, vb, ga, gb = _proj_call(h, trig, expand, *pw, tm=TM_PROJ)
        mixed = _attn_call(ik, iq, iw, aq, ak, av, qb, kb, vb, positions, TQ_ATTN, topk)
        fw = _prepare_ffn_weights(w_branch_a[l], w_branch_b[l], w_out[l], norm_ffn_g[l],
                                  w_up[l], conv_w[l], conv_b[l], w_down[l], norm_final_g,
                                  FFN_CHUNKS)
        h = _ffn_call(h, *mixed, ga, gb, *fw, tm=TM_FFN, final_norm=(l == depth - 1))
    return h
```

```python
import functools

import jax
import jax.numpy as jnp
import numpy as np
from jax import lax
from jax.experimental import pallas as pl
from jax.experimental.pallas import tpu as pltpu

ROPE_THETA = 500000.0
EPS = 1e-6
A_HEADS = 8
A_KV_HEADS = 2
A_GROUP = A_HEADS // A_KV_HEADS
A_HEAD_DIM = 64
A_ROT_DIM = 16
IDX_HEADS = 8
IDX_DIM = 64
INDEX_TOPK_MAX = 256
B_HEADS = 8
B_Q_RANK = 256
B_KV_RANK = 128
B_NOPE_DIM = 64
B_ROPE_DIM = 32
B_V_DIM = 64
CONV_WIDTH = 3

LANES = 128
SUBLANES = 8
VMEM_LIMIT_BYTES = 56 * 1024 * 1024

_T_IW = IDX_DIM + B_ROPE_DIM
assert IDX_DIM == B_NOPE_DIM and _T_IW + IDX_HEADS <= LANES


def _proj_layout(d_model):
    widths = dict(aq=A_HEADS * A_HEAD_DIM, ak=A_KV_HEADS * A_HEAD_DIM,
                  av=A_KV_HEADS * A_HEAD_DIM, iq=IDX_HEADS * IDX_DIM, cq=B_Q_RANK,
                  ckv=B_KV_RANK, ga=d_model, gb=d_model, tail=LANES)
    offsets, total = {}, 0
    for name, width in widths.items():
        assert width % LANES == 0
        offsets[name] = total
        total += width
    return offsets, total

F32 = jnp.float32
BF16 = jnp.bfloat16
NEG_INF = float("-inf")
LOG2_E = 1.4426950408889634
F32_LOWEST = float(np.finfo(np.float32).min)


def _dot(a, b):
    return jnp.dot(a, b, preferred_element_type=F32)


def _dot_nt(a, b):
    return lax.dot_general(a, b, (((1,), (1,)), ((), ())), preferred_element_type=F32)


def _rope_lanes(x, cos_t, sin_t, take_up, half):
    up = pltpu.roll(x, LANES - half, axis=1)
    dn = pltpu.roll(x, half, axis=1)
    return x * cos_t + jnp.where(take_up, up, dn) * sin_t


def _expand_rope_tables(trig_t, expand_ref):
    tm = trig_t[0].shape[1]
    used = sum(piece.shape[0] for piece in trig_t)
    rest = lax.broadcasted_iota(jnp.int32, (LANES - used, tm), 0)
    ones_then_zeros = jnp.where(rest == 0, 1.0, 0.0)
    t = jnp.concatenate(list(trig_t) + [ones_then_zeros], axis=0).T
    a1 = t.astype(BF16)
    r1 = t - a1.astype(F32)
    a2 = r1.astype(BF16)
    a3 = (r1 - a2.astype(F32)).astype(BF16)
    tables = _dot(jnp.concatenate([a1, a2, a3], axis=1), expand_ref[...])
    return [tables[:, i * LANES:(i + 1) * LANES] for i in range(4)]


def _proj_kernel(x_ref, ca_ref, sa_ref, cb_ref, sb_ref, expand_ref, gmix_ref, w_ref,
                 gik_ref, gq_ref, gkv_ref, wuq_ref, wuk_ref, wuv_ref,
                 aq_ref, ak_ref, av_ref, iq_ref, ik_ref, iw_ref, qb_ref, kb_ref,
                 vb_ref, ga_ref, gb_ref, *, offs):
    x = x_ref[0]
    tm, d_model = x.shape
    ms = jnp.mean(x * x, axis=-1, keepdims=True)
    hn = (x * lax.rsqrt(ms + EPS) * gmix_ref[...]).astype(BF16)

    lane = lax.broadcasted_iota(jnp.int32, (tm, LANES), 1)
    ca, sa, cb, sb = _expand_rope_tables(
        (ca_ref[0], sa_ref[0], cb_ref[0], sb_ref[0]), expand_ref)
    up_a = (lane % A_HEAD_DIM) < (A_ROT_DIM // 2)
    up_b = lane < (B_NOPE_DIM + B_ROPE_DIM // 2)
    v_one = jnp.where(lane == A_HEAD_DIM, 1.0, 0.0).astype(F32)
    low_half = lane < A_HEAD_DIM

    def proj(off, width):
        return _dot(hn, w_ref[:, off:off + width])

    def store_heads(ref, tile128, first_head, scale):
        t = tile128 * scale if scale != 1.0 else tile128
        ref[0, first_head] = t[:, :A_HEAD_DIM].astype(ref.dtype)
        ref[0, first_head + 1] = t[:, A_HEAD_DIM:].astype(ref.dtype)

    assert A_HEADS * A_HEAD_DIM == IDX_HEADS * IDX_DIM and A_HEAD_DIM == IDX_DIM
    n_head_pairs = A_HEADS // 2
    for off, ref, scale in ((offs["aq"], aq_ref, A_HEAD_DIM ** -0.5 * LOG2_E),
                            (offs["iq"], iq_ref, IDX_DIM ** -0.5)):
        full = proj(off, n_head_pairs * LANES)
        for j in range(n_head_pairs):
            t = _rope_lanes(full[:, j * LANES:(j + 1) * LANES], ca, sa, up_a,
                            A_ROT_DIM // 2)
            store_heads(ref, t, 2 * j, scale)

    assert A_KV_HEADS * A_HEAD_DIM == LANES and offs["av"] == offs["ak"] + LANES
    akv = proj(offs["ak"], 2 * LANES)
    store_heads(ak_ref, _rope_lanes(akv[:, :LANES], ca, sa, up_a, A_ROT_DIM // 2),
                0, 1.0)
    v2 = akv[:, LANES:]
    av_ref[0, 0] = (jnp.where(low_half, v2, 0.0) + v_one).astype(BF16)
    v2r = pltpu.roll(v2, A_HEAD_DIM, axis=1)
    av_ref[0, 1] = (jnp.where(low_half, v2r, 0.0) + v_one).astype(BF16)

    tail = proj(offs["tail"], LANES)
    ik_sq = jnp.where(low_half, tail * tail, 0.0)
    ik_ms = jnp.sum(ik_sq, axis=-1, keepdims=True) * (1.0 / IDX_DIM)
    ik_n = tail * lax.rsqrt(ik_ms + EPS) * gik_ref[...]
    ik_r = _rope_lanes(ik_n, ca, sa, up_a, A_ROT_DIM // 2)
    ik_ref[0] = ik_r[:, :IDX_DIM].astype(BF16)
    iw_ref[0] = tail * (IDX_HEADS ** -0.5)
    kr = _rope_lanes(tail, cb, sb, up_b, B_ROPE_DIM // 2)
    kr = jnp.where((lane >= B_NOPE_DIM) & (lane < B_NOPE_DIM + B_ROPE_DIM), kr, 0.0)

    cq = proj(offs["cq"], B_Q_RANK)
    cq_n = (cq * lax.rsqrt(jnp.mean(cq * cq, axis=-1, keepdims=True) + EPS)
            * gq_ref[...]).astype(BF16)
    ckv = proj(offs["ckv"], B_KV_RANK)
    ckv_n = (ckv * lax.rsqrt(jnp.mean(ckv * ckv, axis=-1, keepdims=True) + EPS)
             * gkv_ref[...]).astype(BF16)
    b_scale = (B_NOPE_DIM + B_ROPE_DIM) ** -0.5 * LOG2_E
    for h0 in range(0, B_HEADS, 2):
        cols = slice(h0 * LANES, (h0 + 2) * LANES)
        q2 = _dot(cq_n, wuq_ref[:, cols])
        k2 = _dot(ckv_n, wuk_ref[:, cols])
        v2 = _dot(ckv_n, wuv_ref[:, cols])
        for i in range(2):
            lanes = slice(i * LANES, (i + 1) * LANES)
            q = _rope_lanes(q2[:, lanes], cb, sb, up_b, B_ROPE_DIM // 2) * b_scale
            qb_ref[0, h0 + i] = q.astype(BF16)
            kb_ref[0, h0 + i] = (k2[:, lanes] + kr).astype(BF16)
            vb_ref[0, h0 + i] = (v2[:, lanes] + v_one).astype(BF16)

    ga_ref[0] = jax.nn.sigmoid(proj(offs["ga"], d_model)).astype(BF16)
    gb_ref[0] = jax.nn.sigmoid(proj(offs["gb"], d_model)).astype(BF16)


_R_CA = 0
_R_SA = _R_CA + A_ROT_DIM // 2
_R_CB = _R_SA + A_ROT_DIM // 2
_R_SB = _R_CB + B_ROPE_DIM // 2
_R_ONE = _R_SB + B_ROPE_DIM // 2


def _rope_trig(positions):
    pos = positions.astype(F32)[:, None, :]

    def cs(rot):
        inv = ROPE_THETA ** (-jnp.arange(0, rot, 2, dtype=F32) / rot)
        ang = pos * inv[None, :, None]
        return jnp.cos(ang), jnp.sin(ang)

    return cs(A_ROT_DIM) + cs(B_ROPE_DIM)


def _rope_expansion():
    e = np.zeros((LANES, 4 * LANES), np.float32)
    half_a, half_b = A_ROT_DIM // 2, B_ROPE_DIM // 2
    for j in range(LANES):
        d = j % A_HEAD_DIM
        if d < A_ROT_DIM:
            e[_R_CA + d % half_a, j] = 1.0
            e[_R_SA + d % half_a, LANES + j] = -1.0 if d < half_a else 1.0
        else:
            e[_R_ONE, j] = 1.0
        d = j - B_NOPE_DIM
        if 0 <= d < B_ROPE_DIM:
            e[_R_CB + d % half_b, 2 * LANES + j] = 1.0
            e[_R_SB + d % half_b, 3 * LANES + j] = -1.0 if d < half_b else 1.0
        else:
            e[_R_ONE, 2 * LANES + j] = 1.0
    return jnp.asarray(np.concatenate([e, e, e], axis=0), BF16)


def _regroup_w_in(w_in):
    d_model = w_in.shape[0]
    a_kv = A_KV_HEADS * A_HEAD_DIM
    splits = [A_HEADS * A_HEAD_DIM, a_kv, a_kv, IDX_HEADS * IDX_DIM, IDX_DIM, IDX_HEADS,
              B_Q_RANK, B_KV_RANK, B_ROPE_DIM, d_model, d_model]
    assert sum(splits) == w_in.shape[1]
    offs = np.concatenate([[0], np.cumsum(splits)])
    seg = [w_in[:, offs[i]:offs[i + 1]] for i in range(len(splits))]
    a_q, a_k, a_v, i_q, i_k, i_w, b_cq, b_ckv, b_kr, g_a, g_b = seg
    pad = jnp.zeros((d_model, LANES - _T_IW - IDX_HEADS), w_in.dtype)
    return jnp.concatenate(
        [a_q, a_k, a_v, i_q, b_cq, b_ckv, g_a, g_b, i_k, b_kr, i_w, pad], axis=1)


def _pad_heads(w, width):
    r, h, d = w.shape
    return jnp.pad(w, ((0, 0), (0, 0), (0, width - d))).reshape(r, h * width)


def _const_spec(shape):
    nd = len(shape)
    return pl.BlockSpec(shape, lambda *_: (0,) * nd)


def _proj_call(x, trig, expand, gmix, wcat, gik, gq, gkv, wuq, wuk, wuv, tm):
    b, s, d = x.shape
    grid = (b, s // tm)
    offs, total = _proj_layout(d)
    assert wcat.shape == (d, total)
    tok = lambda w: pl.BlockSpec((1, tm, w), lambda i, j: (i, j, 0))
    head = lambda h, w: pl.BlockSpec((1, h, tm, w), lambda i, j: (i, 0, j, 0))
    seq_on_lanes = lambda t: pl.BlockSpec((1, t.shape[1], tm), lambda i, j: (i, 0, j))
    consts = (expand, gmix, wcat, gik, gq, gkv, wuq, wuk, wuv)
    in_specs = ([tok(d)] + [seq_on_lanes(t) for t in trig]
                + [_const_spec(c.shape) for c in consts])
    sds = jax.ShapeDtypeStruct
    out_shape = [
        sds((b, A_HEADS, s, A_HEAD_DIM), BF16),
        sds((b, A_KV_HEADS, s, A_HEAD_DIM), BF16),
        sds((b, A_KV_HEADS, s, LANES), BF16),
        sds((b, IDX_HEADS, s, IDX_DIM), BF16),
        sds((b, s, IDX_DIM), BF16),
        sds((b, s, LANES), F32),
        sds((b, B_HEADS, s, LANES), BF16),
        sds((b, B_HEADS, s, LANES), BF16),
        sds((b, B_HEADS, s, LANES), BF16),
        sds((b, s, d), BF16),
        sds((b, s, d), BF16),
    ]
    out_specs = [
        head(A_HEADS, A_HEAD_DIM), head(A_KV_HEADS, A_HEAD_DIM),
        head(A_KV_HEADS, LANES), head(IDX_HEADS, IDX_DIM), tok(IDX_DIM),
        tok(LANES), head(B_HEADS, LANES), head(B_HEADS, LANES),
        head(B_HEADS, LANES), tok(d), tok(d)]
    return pl.pallas_call(
        functools.partial(_proj_kernel, offs=offs),
        grid=grid,
        in_specs=in_specs,
        out_specs=out_specs,
        out_shape=out_shape,
        compiler_params=pltpu.CompilerParams(
            dimension_semantics=("arbitrary", "arbitrary"),
            vmem_limit_bytes=VMEM_LIMIT_BYTES),
        name="proj",
    )(x, *trig, *consts)

def _prepare_proj_weights(norm_mix_g, w_in, idx_k_norm_g, q_a_norm_g, kv_a_norm_g,
                          w_uq, w_uk, w_uv):
    gmix = norm_mix_g.reshape(1, -1)
    wcat = _regroup_w_in(w_in).astype(BF16)
    gik = jnp.pad(idx_k_norm_g, (0, LANES - IDX_DIM)).reshape(1, LANES)
    gq = q_a_norm_g.reshape(1, -1)
    gkv = kv_a_norm_g.reshape(1, -1)
    wuq = _pad_heads(w_uq, LANES).astype(BF16)
    wuk = _pad_heads(w_uk, LANES).astype(BF16)
    wuv = _pad_heads(w_uv, LANES).astype(BF16)
    return gmix, wcat, gik, gq, gkv, wuq, wuk, wuv


_INT_MIN = -2 ** 31
_ORD_NEG_INF = 0x007FFFFF
_ORD_LOWEST_FINITE = 0x00800000


def _ordinal_to_float(u):
    o = u ^ jnp.int32(_INT_MIN)
    bits = o ^ ((o >> 31) & jnp.int32(0x7FFFFFFF))
    return lax.bitcast_convert_type(bits, F32)


def _softmax_update(s, v, m_old, acc, rows_may_be_empty=True):
    m_new = jnp.maximum(m_old, jnp.max(s, axis=-1, keepdims=True))
    if rows_may_be_empty:
        m_new = jnp.maximum(m_new, F32_LOWEST)
    p = jnp.concatenate(
        [jnp.exp2(s[:, j * LANES:(j + 1) * LANES] - m_new)
         for j in range(s.shape[1] // LANES)], axis=1)
    alpha = jnp.exp2(m_old - m_new)
    return m_new, alpha * acc + _dot(p.astype(BF16), v)


def _attend_chunks(n_heads, m_ref, acc_ref, chunk_fns):
    for h in range(n_heads):
        m, acc = m_ref[h], acc_ref[h]
        for fn in chunk_fns:
            m, acc = _softmax_update(*fn(h), m, acc)
        m_ref[h] = m
        acc_ref[h] = acc


def _for_chunk_groups(n, body):
    def group(i, carry):
        body(tuple(4 * i + r for r in range(4)))
        return carry

    lax.fori_loop(0, lax.shift_right_logical(n, 2), group, 0)
    done = n & ~3
    for left in (1, 2, 3):
        @pl.when((n & 3) == left)
        def _(left=left):
            body(tuple(done + r for r in range(left)))


def _init_softmax_state(m_ref, acc_ref):
    m_ref[...] = jnp.full(m_ref.shape, NEG_INF, F32)
    acc_ref[...] = jnp.zeros(acc_ref.shape, F32)


def _finish_heads(acc_even, acc_odd, lane):
    o0 = acc_even * (1.0 / acc_even[:, B_V_DIM:B_V_DIM + 1])
    o1 = acc_odd * (1.0 / acc_odd[:, B_V_DIM:B_V_DIM + 1])
    return jnp.where(lane < B_V_DIM, o0, pltpu.roll(o1, B_V_DIM, axis=1))


def _store_heads(o_ref, acc_ref, n_heads):
    lane = lax.broadcasted_iota(jnp.int32, acc_ref.shape[1:], 1)
    for pair in range(n_heads // 2):
        o_ref[0, :, pair * LANES:(pair + 1) * LANES] = _finish_heads(
            acc_ref[2 * pair], acc_ref[2 * pair + 1], lane).astype(o_ref.dtype)


_COARSE_PASSES = 16
_FINE_PASSES = 17
_PASSES_PER_CHUNK = 4


def _attn_kernel(ik_ref, ak_ref, av_ref, kb_ref, vb_ref, posk_col_ref, posk_row_ref,
                 iq0_ref, iq1_ref, iw0_ref, iw1_ref, aq0_ref, aq1_ref, qb0_ref, qb1_ref,
                 pqr0_ref, pqr1_ref, pqc0_ref, pqc1_ref,
                 oa0_ref, oa1_ref, ob0_ref, ob1_ref,
                 isc_ref, xb_ref, bias_ref, wt_ref, ma_ref, acca_ref, mb_ref, accb_ref,
                 *, topk, nq):
    pi = pl.program_id(1)
    tq = aq0_ref.shape[2]
    half = tq // 2
    nk0 = pi + 1
    n_virtual = nq + 1
    blocks = (
        dict(idx=0, nk=nk0, off=0, iq=iq0_ref, iw=iw0_ref, aq=aq0_ref, pqr=pqr0_ref,
             oa=oa0_ref, ob=ob0_ref),
        dict(idx=1, nk=nq - pi, off=nk0, iq=iq1_ref, iw=iw1_ref, aq=aq1_ref, pqr=pqr1_ref,
             oa=oa1_ref, ob=ob1_ref),
    )

    for blk in blocks:
        wt_ref[blk["idx"]] = blk["iw"][0].T[_T_IW:_T_IW + IDX_HEADS, :]

    def pick(v, of_block0, of_block1):
        if v == 0:
            return of_block0()
        if v >= nq // 2:
            return of_block1()
        return jnp.where(v >= nk0, of_block1(), of_block0())

    def route(v, part, acc0, acc1):
        if v == 0:
            return acc0 + part, acc1
        if v >= nq // 2:
            return acc0, acc1 + part
        in1 = v >= nk0
        zero = jnp.zeros_like(part)
        return acc0 + jnp.where(in1, zero, part), acc1 + jnp.where(in1, part, zero)

    def idx_chunk(v):
        c = pick(v, lambda: jnp.int32(v), lambda: v - nk0)
        posq = pick(v, lambda: pqr0_ref[0], lambda: pqr1_ref[0])
        wt = pick(v, lambda: wt_ref[0], lambda: wt_ref[1])
        iqs = [pick(v, lambda: iq0_ref[0, h], lambda: iq1_ref[0, h])
               for h in range(IDX_HEADS)]
        for sub in range(2):
            k0 = pl.multiple_of(c * tq + sub * half, half)
            ikc = ik_ref[0, pl.ds(k0, half), :]
            acc = jnp.zeros((half, tq), F32)
            for h in range(IDX_HEADS):
                d = _dot_nt(ikc, iqs[h])
                acc = acc + wt[h:h + 1, :] * jnp.maximum(d, 0.0)
            causal = posk_col_ref[0, pl.ds(k0, half), :] <= posq
            score = jnp.where(causal, acc, NEG_INF)
            isc_ref[v, sub * half:(sub + 1) * half, :] = score
            xb_ref[v, sub * half:(sub + 1) * half, :] = score.astype(BF16)

    for v in range(n_virtual):
        idx_chunk(v)

    _init_softmax_state(mb_ref, accb_ref)

    def latent_chunk(v):
        diagonal = v in (0, n_virtual - 1)
        if v == 0:
            block, c = 0, nk0 - 1
        elif v >= nq // 2:
            block, c = 1, v - nk0
        else:
            block = None
            in1 = v >= nk0
            c = jnp.where(in1, v - nk0, v - 1)
            b = in1.astype(jnp.int32)
        if block is not None:
            b = block
            q_ref, pqc_ref = (qb1_ref, pqc1_ref) if block else (qb0_ref, pqc0_ref)
        k0 = pl.multiple_of(c * tq, tq)
        if diagonal:
            causal = posk_row_ref[0, :, pl.ds(k0, tq)] <= pqc_ref[0]
        for h in range(B_HEADS):
            if block is not None:
                q = q_ref[0, h]
            else:
                q = jnp.where(in1, qb1_ref[0, h], qb0_ref[0, h])
            s = _dot_nt(q, kb_ref[0, h, pl.ds(k0, tq), :])
            if diagonal:
                s = jnp.where(causal, s, NEG_INF)
            m, acc = _softmax_update(s, vb_ref[0, h, pl.ds(k0, tq), :],
                                     mb_ref[b, h], accb_ref[b, h], rows_may_be_empty=False)
            mb_ref[b, h] = m
            accb_ref[b, h] = acc

    packed = 2 * SUBLANES
    one_b, zero_b = jnp.asarray(1, BF16), jnp.asarray(0, BF16)

    def count_ge(thr0, thr1, coarse):
        rows = packed if coarse else SUBLANES
        dtype = BF16 if coarse else F32
        acc0 = jnp.zeros((rows, tq), dtype)
        acc1 = jnp.zeros((rows, tq), dtype)
        for v in range(n_virtual):
            thr = pick(v, lambda: thr0, lambda: thr1)
            if coarse:
                hit = jnp.where(xb_ref[v] >= thr, one_b, zero_b)
            else:
                hit = jnp.where(isc_ref[v] >= thr, 1.0, 0.0)
            part = hit[:rows]
            for r in range(1, tq // rows):
                part = part + hit[r * rows:(r + 1) * rows]
            acc0, acc1 = route(v, part, acc0, acc1)
        return (jnp.sum(acc0.astype(F32), axis=0, keepdims=True),
                jnp.sum(acc1.astype(F32), axis=0, keepdims=True))

    def coarse_ordinal(j):
        low = jnp.where(j < 0x8000, jnp.int32(0xFFFF), jnp.int32(0))
        return lax.shift_left(j, 16) | low

    def coarse_pass(it, js):
        cands = [j | lax.shift_left(jnp.int32(1), 15 - it) for j in js]
        thrs = [_ordinal_to_float(coarse_ordinal(c)).astype(BF16) for c in cands]
        cnts = count_ge(*thrs, coarse=True)
        return tuple(jnp.where(n >= topk, c, j) for n, c, j in zip(cnts, cands, js))

    def fine_pass(it, ds, bases):
        cands = [d | lax.shift_left(jnp.int32(1), 16 - it) for d in ds]
        thrs = [_ordinal_to_float(b + c) for b, c in zip(bases, cands)]
        cnts = count_ge(*thrs, coarse=False)
        return tuple(jnp.where(n >= topk, c, d) for n, c, d in zip(cnts, cands, ds))

    zeros = (jnp.zeros((1, tq), jnp.int32),) * 2
    first_fine = _COARSE_PASSES // _PASSES_PER_CHUNK
    js = zeros
    for v in range(first_fine):
        latent_chunk(v)
        for r in range(_PASSES_PER_CHUNK):
            js = coarse_pass(_PASSES_PER_CHUNK * v + r, js)
    js = [jnp.maximum(j, _ORD_NEG_INF >> 16) for j in js]
    bases = [coarse_ordinal(j) - jnp.int32(1 << 16) for j in js]
    ds = zeros
    for v in range(first_fine, n_virtual):
        latent_chunk(v)
        first_pass = _PASSES_PER_CHUNK * (v - first_fine)
        for it in range(first_pass, min(first_pass + _PASSES_PER_CHUNK, _FINE_PASSES)):
            ds = fine_pass(it, ds, bases)

    t_us = [base + d for base, d in zip(bases, ds)]
    shorts = [(t_u >= 0) & (t_u < _ORD_LOWEST_FINITE) for t_u in t_us]
    thrs = [_ordinal_to_float(jnp.where(short, jnp.int32(_ORD_LOWEST_FINITE), t_u))
            for short, t_u in zip(shorts, t_us)]
    n_ges = [jnp.zeros((1, tq), F32)] * 2
    for v in range(n_virtual):
        sel = isc_ref[v] >= pick(v, lambda: thrs[0], lambda: thrs[1])
        bias_ref[v] = jnp.where(sel, 0.0, NEG_INF).T
        cnt = jnp.sum(jnp.where(sel, 1.0, 0.0), axis=0, keepdims=True)
        n_ges = route(v, cnt, *n_ges)

    for blk, thr, short, n_ge in zip(blocks, thrs, shorts, n_ges):
        off, nk = blk["off"], blk["nk"]
        tied = jnp.logical_and(n_ge > topk, jnp.logical_not(short))

        @pl.when(jnp.max(jnp.where(tied, 1.0, 0.0)) > 0.0)
        def _(thr=thr, off=off, nk=nk):
            def count_gt(c, acc):
                hit = jnp.where(isc_ref[off + c] > thr, 1.0, 0.0)
                return acc + jnp.sum(hit, axis=0, keepdims=True)
            n_gt = lax.fori_loop(0, nk, count_gt, jnp.zeros((1, tq), F32))
            need = topk - n_gt
            row = lax.broadcasted_iota(jnp.int32, (tq, tq), 0)
            col = lax.broadcasted_iota(jnp.int32, (tq, tq), 1)
            below = jnp.where(col < row, 1.0, 0.0).astype(BF16)

            def fix(c, seen):
                x = isc_ref[off + c]
                eq = jnp.where(x == thr, 1.0, 0.0)
                rank = _dot(below, eq.astype(BF16)) + seen
                sel = (x > thr) | ((x == thr) & (rank < need))
                bias_ref[off + c] = jnp.where(sel, 0.0, NEG_INF).T
                return seen + jnp.sum(eq, axis=0, keepdims=True)

            lax.fori_loop(0, nk, fix, jnp.zeros((1, tq), F32))

    _init_softmax_state(ma_ref, acca_ref)
    for blk in blocks:
        def attend(chunks, blk=blk):
            def chunk_fn(c):
                k0 = pl.multiple_of(c * tq, tq)
                bias = bias_ref[blk["off"] + c]

                def fn(h):
                    g = h // A_GROUP
                    s = _dot_nt(blk["aq"][0, h], ak_ref[0, g, pl.ds(k0, tq), :]) + bias
                    return s, av_ref[0, g, pl.ds(k0, tq), :]
                return fn

            _attend_chunks(A_HEADS, ma_ref.at[blk["idx"]], acca_ref.at[blk["idx"]],
                           [chunk_fn(c) for c in chunks])

        _for_chunk_groups(blk["nk"], attend)
        _store_heads(blk["oa"], acca_ref.at[blk["idx"]], A_HEADS)
        _store_heads(blk["ob"], accb_ref.at[blk["idx"]], B_HEADS)


def _attn_call(ik, iq, iw, aq, ak, av, qb, kb, vb, positions, tq, topk):
    b, heads, s, _ = aq.shape
    nq = s // tq
    n_pairs = nq // 2
    assert s % tq == 0 and nq % 2 == 0
    assert nq + 1 == (_COARSE_PASSES + _FINE_PASSES - 1) // _PASSES_PER_CHUNK + 1
    assert (nq + 1) * tq // (2 * SUBLANES) <= 256
    pos_col = positions[:, :, None]
    pos_row = positions[:, None, :]

    def per_seq(*shape):
        nd = len(shape)
        return pl.BlockSpec((1,) + shape, lambda i, j: (i,) + (0,) * nd)

    blk_index = (lambda j: j, lambda j: nq - 1 - j)

    def qhead(h, w, which):
        return pl.BlockSpec((1, h, tq, w), lambda i, j: (i, 0, blk_index[which](j), 0))

    def qrows(w, which):
        return pl.BlockSpec((1, tq, w), lambda i, j: (i, blk_index[which](j), 0))

    def qlanes(which):
        return pl.BlockSpec((1, 1, tq), lambda i, j: (i, 0, blk_index[which](j)))

    in_specs = [
        per_seq(s, IDX_DIM), per_seq(A_KV_HEADS, s, A_HEAD_DIM), per_seq(A_KV_HEADS, s, LANES),
        per_seq(B_HEADS, s, LANES), per_seq(B_HEADS, s, LANES),
        per_seq(s, 1), per_seq(1, s),
        qhead(IDX_HEADS, IDX_DIM, 0), qhead(IDX_HEADS, IDX_DIM, 1),
        qrows(LANES, 0), qrows(LANES, 1),
        qhead(A_HEADS, A_HEAD_DIM, 0), qhead(A_HEADS, A_HEAD_DIM, 1),
        qhead(B_HEADS, LANES, 0), qhead(B_HEADS, LANES, 1),
        qlanes(0), qlanes(1), qrows(1, 0), qrows(1, 1),
    ]
    width = heads * A_HEAD_DIM
    out_sds = jax.ShapeDtypeStruct((b, s // 2, width), BF16)
    out_lo = pl.BlockSpec((1, tq, width), lambda i, j: (i, j, 0))
    out_hi = pl.BlockSpec((1, tq, width), lambda i, j: (i, n_pairs - 1 - j, 0))
    state = pltpu.VMEM((2, heads, tq, LANES), F32)
    return pl.pallas_call(
        functools.partial(_attn_kernel, topk=topk, nq=nq),
        grid=(b, n_pairs),
        in_specs=in_specs,
        out_specs=[out_lo, out_hi, out_lo, out_hi],
        out_shape=[out_sds] * 4,
        scratch_shapes=[
            pltpu.VMEM((nq + 1, tq, tq), F32),
            pltpu.VMEM((nq + 1, tq, tq), BF16),
            pltpu.VMEM((nq + 1, tq, tq), F32),
            pltpu.VMEM((2, IDX_HEADS, tq), F32),
            state, state, state, state,
        ],
        compiler_params=pltpu.CompilerParams(
            dimension_semantics=("arbitrary", "arbitrary"),
            vmem_limit_bytes=VMEM_LIMIT_BYTES),
        name="attn",
    )(ik, ak, av, kb, vb, pos_col, pos_row, iq, iq, iw, iw, aq, aq, qb, qb,
      pos_row, pos_row, pos_col, pos_col)


def _shift_rows(u, prev_tail, shift, row8):
    rolled = pltpu.roll(u, shift, axis=0)
    head = jnp.where(row8 < shift, pltpu.roll(prev_tail, shift, axis=0),
                     rolled[:SUBLANES])
    return jnp.concatenate([head, rolled[SUBLANES:]], axis=0)


def _ffn_kernel(x_ref, oa_lo_ref, oa_hi_ref, ob_lo_ref, ob_hi_ref, ga_ref, gb_ref,
                wba_ref, wbb_ref, wout_ref, gffn_ref, wup_ref, cw_ref, cb_ref, wdown_ref,
                gfin_ref, out_ref, tail_ref, *, n_chunks, final_norm):
    j = pl.program_id(1)
    x = x_ref[0]
    tm = x.shape[0]
    first_half = j < pl.num_programs(1) // 2
    oa = jnp.where(first_half, oa_lo_ref[0], oa_hi_ref[0])
    ob = jnp.where(first_half, ob_lo_ref[0], ob_hi_ref[0])
    ya = _dot(oa, wba_ref[...])
    yb = _dot(ob, wbb_ref[...])
    merged = ga_ref[0].astype(F32) * ya + gb_ref[0].astype(F32) * yb
    h1 = x + _dot(merged.astype(BF16), wout_ref[...])
    ms = jnp.mean(h1 * h1, axis=-1, keepdims=True)
    hn = (h1 * lax.rsqrt(ms + EPS) * gffn_ref[...]).astype(BF16)

    fc = wdown_ref.shape[1]
    row8 = lax.broadcasted_iota(jnp.int32, (SUBLANES, fc), 0)

    @pl.when(j == 0)
    def _():
        tail_ref[...] = jnp.zeros_like(tail_ref)

    acc = h1
    for c in range(n_chunks):
        halves = []
        for part in range(2):
            idx = part * n_chunks + c
            u = _dot(hn, wup_ref[:, idx * fc:(idx + 1) * fc])
            prev_tail = tail_ref[idx]
            tail_ref[idx] = u[tm - SUBLANES:]
            y = (cb_ref[idx] + cw_ref[0, idx] * _shift_rows(u, prev_tail, 2, row8)
                 + cw_ref[1, idx] * _shift_rows(u, prev_tail, 1, row8)
                 + cw_ref[2, idx] * u)
            halves.append(y)
        gate, val = halves
        act = (gate * jax.nn.sigmoid(gate) * val).astype(BF16)
        acc = acc + _dot(act, wdown_ref[c])
    if final_norm:
        ms = jnp.mean(acc * acc, axis=-1, keepdims=True)
        acc = acc * lax.rsqrt(ms + EPS) * gfin_ref[...]
    out_ref[0] = acc


def _ffn_call(x, oa_lo, oa_hi, ob_lo, ob_hi, ga, gb, wba, wbb, wout, gffn, wup, cw, cb,
              wdown, gfin, tm, final_norm):
    b, s, d = x.shape
    n_chunks, fc, _ = wdown.shape
    n_parts = 2 * n_chunks
    assert wup.shape == (d, n_parts * fc) and fc % LANES == 0
    n_tiles = s // tm
    assert s % tm == 0 and n_tiles % 2 == 0
    n_half = n_tiles // 2
    tok = lambda w: pl.BlockSpec((1, tm, w), lambda i, j: (i, j, 0))
    lo = lambda w: pl.BlockSpec((1, tm, w), lambda i, j: (i, jnp.minimum(j, n_half - 1), 0))
    hi = lambda w: pl.BlockSpec((1, tm, w), lambda i, j: (i, jnp.maximum(j - n_half, 0), 0))
    wa, wb = oa_lo.shape[-1], ob_lo.shape[-1]
    weights = (wba, wbb, wout, gffn, wup, cw, cb, wdown, gfin)
    return pl.pallas_call(
        functools.partial(_ffn_kernel, n_chunks=n_chunks, final_norm=final_norm),
        grid=(b, n_tiles),
        in_specs=[tok(d), lo(wa), hi(wa), lo(wb), hi(wb), tok(d), tok(d)]
        + [_const_spec(w.shape) for w in weights],
        out_specs=tok(d),
        out_shape=jax.ShapeDtypeStruct((b, s, d), x.dtype),
        scratch_shapes=[pltpu.VMEM((n_parts, SUBLANES, fc), F32)],
        compiler_params=pltpu.CompilerParams(
            dimension_semantics=("arbitrary", "arbitrary"),
            vmem_limit_bytes=VMEM_LIMIT_BYTES),
        name="ffn",
    )(x, oa_lo, oa_hi, ob_lo, ob_hi, ga, gb, *weights)


def _prepare_ffn_weights(w_branch_a, w_branch_b, w_out, norm_ffn_g, w_up, conv_w,
                         conv_b, w_down, norm_final_g, n_chunks):
    d, two_f = w_up.shape
    fc = two_f // (2 * n_chunks)
    wup = w_up.astype(BF16)
    cw = conv_w.reshape(CONV_WIDTH, 2 * n_chunks, 1, fc)
    cb = conv_b.reshape(2 * n_chunks, 1, fc)
    wdown = w_down.reshape(n_chunks, fc, d).astype(BF16)
    return (w_branch_a.astype(BF16), w_branch_b.astype(BF16), w_out.astype(BF16),
            norm_ffn_g.reshape(1, -1), wup, cw, cb, wdown, norm_final_g.reshape(1, -1))


TM_PROJ = 512
TQ_ATTN = 256
TM_FFN = 512
FFN_CHUNKS = 1


def kernel(x, positions, norm_mix_g, w_in, idx_k_norm_g, q_a_norm_g, kv_a_norm_g, w_uq, w_uk, w_uv, w_branch_a, w_branch_b, w_out, norm_ffn_g, w_up, conv_w, conv_b, w_down, norm_final_g):
    b, s, _ = x.shape
    depth = w_in.shape[0]
    topk = min(INDEX_TOPK_MAX, s // 4)
    trig = _rope_trig(positions)
    expand = _rope_expansion()
    h = x
    for l in range(depth):
        pw = _prepare_proj_weights(norm_mix_g[l], w_in[l], idx_k_norm_g[l], q_a_norm_g[l],
                                   kv_a_norm_g[l], w_uq[l], w_uk[l], w_uv[l])
        aq, ak, av, iq, ik, iw, qb, kb, vb, ga, gb = _proj_call(h, trig, expand, *pw, tm=TM_PROJ)
        mixed = _attn_call(ik, iq, iw, aq, ak, av, qb, kb, vb, positions, TQ_ATTN, topk)
        fw = _prepare_ffn_weights(w_branch_a[l], w_branch_b[l], w_out[l], norm_ffn_g[l],
                                  w_up[l], conv_w[l], conv_b[l], w_down[l], norm_final_g,
                                  FFN_CHUNKS)
        h = _ffn_call(h, *mixed, ga, gb, *fw, tm=TM_FFN, final_norm=(l == depth - 1))
    return h
```

```python
import functools

import jax
import jax.numpy as jnp
import numpy as np
from jax import lax
from jax.experimental import pallas as pl
from jax.experimental.pallas import tpu as pltpu

ROPE_THETA = 500000.0
EPS = 1e-6
A_HEADS = 8
A_KV_HEADS = 2
A_GROUP = A_HEADS // A_KV_HEADS
A_HEAD_DIM = 64
A_ROT_DIM = 16
IDX_HEADS = 8
IDX_DIM = 64
INDEX_TOPK_MAX = 256
B_HEADS = 8
B_Q_RANK = 256
B_KV_RANK = 128
B_NOPE_DIM = 64
B_ROPE_DIM = 32
B_V_DIM = 64
CONV_WIDTH = 3

LANES = 128
SUBLANES = 8
VMEM_LIMIT_BYTES = 56 * 1024 * 1024

_T_IW = IDX_DIM + B_ROPE_DIM
assert IDX_DIM == B_NOPE_DIM and _T_IW + IDX_HEADS <= LANES


def _proj_layout(d_model):
    widths = dict(aq=A_HEADS * A_HEAD_DIM, ak=A_KV_HEADS * A_HEAD_DIM,
                  av=A_KV_HEADS * A_HEAD_DIM, iq=IDX_HEADS * IDX_DIM, cq=B_Q_RANK,
                  ckv=B_KV_RANK, ga=d_model, gb=d_model, tail=LANES)
    offsets, total = {}, 0
    for name, width in widths.items():
        assert width % LANES == 0
        offsets[name] = total
        total += width
    return offsets, total

F32 = jnp.float32
BF16 = jnp.bfloat16
NEG_INF = float("-inf")
LOG2_E = 1.4426950408889634
F32_LOWEST = float(np.finfo(np.float32).min)


def _dot(a, b):
    return jnp.dot(a, b, preferred_element_type=F32)


def _dot_nt(a, b):
    return lax.dot_general(a, b, (((1,), (1,)), ((), ())), preferred_element_type=F32)


def _rope_lanes(x, cos_t, sin_t, take_up, half):
    up = pltpu.roll(x, LANES - half, axis=1)
    dn = pltpu.roll(x, half, axis=1)
    return x * cos_t + jnp.where(take_up, up, dn) * sin_t


def _expand_rope_tables(trig_t, expand_ref):
    tm = trig_t[0].shape[1]
    used = sum(piece.shape[0] for piece in trig_t)
    rest = lax.broadcasted_iota(jnp.int32, (LANES - used, tm), 0)
    ones_then_zeros = jnp.where(rest == 0, 1.0, 0.0)
    t = jnp.concatenate(list(trig_t) + [ones_then_zeros], axis=0).T
    a1 = t.astype(BF16)
    r1 = t - a1.astype(F32)
    a2 = r1.astype(BF16)
    a3 = (r1 - a2.astype(F32)).astype(BF16)
    tables = _dot(jnp.concatenate([a1, a2, a3], axis=1), expand_ref[...])
    return [tables[:, i * LANES:(i + 1) * LANES] for i in range(4)]


def _proj_kernel(x_ref, ca_ref, sa_ref, cb_ref, sb_ref, expand_ref, gmix_ref, w_ref,
                 gik_ref, gq_ref, gkv_ref, wuq_ref, wuk_ref, wuv_ref,
                 aq_ref, ak_ref, av_ref, iq_ref, ik_ref, iw_ref, qb_ref, kb_ref,
                 vb_ref, ga_ref, gb_ref, *, offs):
    x = x_ref[0]
    tm, d_model = x.shape
    ms = jnp.mean(x * x, axis=-1, keepdims=True)
    hn = (x * lax.rsqrt(ms + EPS) * gmix_ref[...]).astype(BF16)

    lane = lax.broadcasted_iota(jnp.int32, (tm, LANES), 1)
    ca, sa, cb, sb = _expand_rope_tables(
        (ca_ref[0], sa_ref[0], cb_ref[0], sb_ref[0]), expand_ref)
    up_a = (lane % A_HEAD_DIM) < (A_ROT_DIM // 2)
    up_b = lane < (B_NOPE_DIM + B_ROPE_DIM // 2)
    v_one = jnp.where(lane == A_HEAD_DIM, 1.0, 0.0).astype(F32)
    low_half = lane < A_HEAD_DIM

    def proj(off, width):
        return _dot(hn, w_ref[:, off:off + width])

    def store_heads(ref, tile128, first_head, scale):
        t = tile128 * scale if scale != 1.0 else tile128
        ref[0, first_head] = t[:, :A_HEAD_DIM].astype(ref.dtype)
        ref[0, first_head + 1] = t[:, A_HEAD_DIM:].astype(ref.dtype)

    assert A_HEADS * A_HEAD_DIM == IDX_HEADS * IDX_DIM and A_HEAD_DIM == IDX_DIM
    n_head_pairs = A_HEADS // 2
    for off, ref, scale in ((offs["aq"], aq_ref, A_HEAD_DIM ** -0.5 * LOG2_E),
                            (offs["iq"], iq_ref, IDX_DIM ** -0.5)):
        full = proj(off, n_head_pairs * LANES)
        for j in range(n_head_pairs):
            t = _rope_lanes(full[:, j * LANES:(j + 1) * LANES], ca, sa, up_a,
                            A_ROT_DIM // 2)
            store_heads(ref, t, 2 * j, scale)

    assert A_KV_HEADS * A_HEAD_DIM == LANES and offs["av"] == offs["ak"] + LANES
    akv = proj(offs["ak"], 2 * LANES)
    store_heads(ak_ref, _rope_lanes(akv[:, :LANES], ca, sa, up_a, A_ROT_DIM // 2),
                0, 1.0)
    v2 = akv[:, LANES:]
    av_ref[0, 0] = (jnp.where(low_half, v2, 0.0) + v_one).astype(BF16)
    v2r = pltpu.roll(v2, A_HEAD_DIM, axis=1)
    av_ref[0, 1] = (jnp.where(low_half, v2r, 0.0) + v_one).astype(BF16)

    tail = proj(offs["tail"], LANES)
    ik_sq = jnp.where(low_half, tail * tail, 0.0)
    ik_ms = jnp.sum(ik_sq, axis=-1, keepdims=True) * (1.0 / IDX_DIM)
    ik_n = tail * lax.rsqrt(ik_ms + EPS) * gik_ref[...]
    ik_r = _rope_lanes(ik_n, ca, sa, up_a, A_ROT_DIM // 2)
    ik_ref[0] = ik_r[:, :IDX_DIM].astype(BF16)
    iw_ref[0] = tail * (IDX_HEADS ** -0.5)
    kr = _rope_lanes(tail, cb, sb, up_b, B_ROPE_DIM // 2)
    kr = jnp.where((lane >= B_NOPE_DIM) & (lane < B_NOPE_DIM + B_ROPE_DIM), kr, 0.0)

    cq = proj(offs["cq"], B_Q_RANK)
    cq_n = (cq * lax.rsqrt(jnp.mean(cq * cq, axis=-1, keepdims=True) + EPS)
            * gq_ref[...]).astype(BF16)
    ckv = proj(offs["ckv"], B_KV_RANK)
    ckv_n = (ckv * lax.rsqrt(jnp.mean(ckv * ckv, axis=-1, keepdims=True) + EPS)
             * gkv_ref[...]).astype(BF16)
    b_scale = (B_NOPE_DIM + B_ROPE_DIM) ** -0.5 * LOG2_E
    for h0 in range(0, B_HEADS, 2):
        cols = slice(h0 * LANES, (h0 + 2) * LANES)
        q2 = _dot(cq_n, wuq_ref[:, cols])
        k2 = _dot(ckv_n, wuk_ref[:, cols])
        v2 = _dot(ckv_n, wuv_ref[:, cols])
        for i in range(2):
            lanes = slice(i * LANES, (i + 1) * LANES)
            q = _rope_lanes(q2[:, lanes], cb, sb, up_b, B_ROPE_DIM // 2) * b_scale
            qb_ref[0, h0 + i] = q.astype(BF16)
            kb_ref[0, h0 + i] = (k2[:, lanes] + kr).astype(BF16)
            vb_ref[0, h0 + i] = (v2[:, lanes] + v_one).astype(BF16)

    ga_ref[0] = jax.nn.sigmoid(proj(offs["ga"], d_model)).astype(BF16)
    gb_ref[0] = jax.nn.sigmoid(proj(offs["gb"], d_model)).astype(BF16)


_R_CA = 0
_R_SA = _R_CA + A_ROT_DIM // 2
_R_CB = _R_SA + A_ROT_DIM // 2
_R_SB = _R_CB + B_ROPE_DIM // 2
_R_ONE = _R_SB + B_ROPE_DIM // 2


def _rope_trig(positions):
    pos = positions.astype(F32)[:, None, :]

    def cs(rot):
        inv = ROPE_THETA ** (-jnp.arange(0, rot, 2, dtype=F32) / rot)
        ang = pos * inv[None, :, None]
        return jnp.cos(ang), jnp.sin(ang)

    return cs(A_ROT_DIM) + cs(B_ROPE_DIM)


def _rope_expansion():
    e = np.zeros((LANES, 4 * LANES), np.float32)
    half_a, half_b = A_ROT_DIM // 2, B_ROPE_DIM // 2
    for j in range(LANES):
        d = j % A_HEAD_DIM
        if d < A_ROT_DIM:
            e[_R_CA + d % half_a, j] = 1.0
            e[_R_SA + d % half_a, LANES + j] = -1.0 if d < half_a else 1.0
        else:
            e[_R_ONE, j] = 1.0
        d = j - B_NOPE_DIM
        if 0 <= d < B_ROPE_DIM:
            e[_R_CB + d % half_b, 2 * LANES + j] = 1.0
            e[_R_SB + d % half_b, 3 * LANES + j] = -1.0 if d < half_b else 1.0
        else:
            e[_R_ONE, 2 * LANES + j] = 1.0
    return jnp.asarray(np.concatenate([e, e, e], axis=0), BF16)


def _regroup_w_in(w_in):
    d_model = w_in.shape[0]
    a_kv = A_KV_HEADS * A_HEAD_DIM
    splits = [A_HEADS * A_HEAD_DIM, a_kv, a_kv, IDX_HEADS * IDX_DIM, IDX_DIM, IDX_HEADS,
              B_Q_RANK, B_KV_RANK, B_ROPE_DIM, d_model, d_model]
    assert sum(splits) == w_in.shape[1]
    offs = np.concatenate([[0], np.cumsum(splits)])
    seg = [w_in[:, offs[i]:offs[i + 1]] for i in range(len(splits))]
    a_q, a_k, a_v, i_q, i_k, i_w, b_cq, b_ckv, b_kr, g_a, g_b = seg
    pad = jnp.zeros((d_model, LANES - _T_IW - IDX_HEADS), w_in.dtype)
    return jnp.concatenate(
        [a_q, a_k, a_v, i_q, b_cq, b_ckv, g_a, g_b, i_k, b_kr, i_w, pad], axis=1)


def _pad_heads(w, width):
    r, h, d = w.shape
    return jnp.pad(w, ((0, 0), (0, 0), (0, width - d))).reshape(r, h * width)


def _const_spec(shape):
    nd = len(shape)
    return pl.BlockSpec(shape, lambda *_: (0,) * nd)


def _proj_call(x, trig, expand, gmix, wcat, gik, gq, gkv, wuq, wuk, wuv, tm):
    b, s, d = x.shape
    grid = (b, s // tm)
    offs, total = _proj_layout(d)
    assert wcat.shape == (d, total)
    tok = lambda w: pl.BlockSpec((1, tm, w), lambda i, j: (i, j, 0))
    head = lambda h, w: pl.BlockSpec((1, h, tm, w), lambda i, j: (i, 0, j, 0))
    seq_on_lanes = lambda t: pl.BlockSpec((1, t.shape[1], tm), lambda i, j: (i, 0, j))
    consts = (expand, gmix, wcat, gik, gq, gkv, wuq, wuk, wuv)
    in_specs = ([tok(d)] + [seq_on_lanes(t) for t in trig]
                + [_const_spec(c.shape) for c in consts])
    sds = jax.ShapeDtypeStruct
    out_shape = [
        sds((b, A_HEADS, s, A_HEAD_DIM), BF16),
        sds((b, A_KV_HEADS, s, A_HEAD_DIM), BF16),
        sds((b, A_KV_HEADS, s, LANES), BF16),
        sds((b, IDX_HEADS, s, IDX_DIM), BF16),
        sds((b, s, IDX_DIM), BF16),
        sds((b, s, LANES), F32),
        sds((b, B_HEADS, s, LANES), BF16),
        sds((b, B_HEADS, s, LANES), BF16),
        sds((b, B_HEADS, s, LANES), BF16),
        sds((b, s, d), BF16),
        sds((b, s, d), BF16),
    ]
    out_specs = [
        head(A_HEADS, A_HEAD_DIM), head(A_KV_HEADS, A_HEAD_DIM),
        head(A_KV_HEADS, LANES), head(IDX_HEADS, IDX_DIM), tok(IDX_DIM),
        tok(LANES), head(B_HEADS, LANES), head(B_HEADS, LANES),
        head(B_HEADS, LANES), tok(d), tok(d)]
    return pl.pallas_call(
        functools.partial(_proj_kernel, offs=offs),
        grid=grid,
        in_specs=in_specs,
        out_specs=out_specs,
        out_shape=out_shape,
        compiler_params=pltpu.CompilerParams(
            dimension_semantics=("arbitrary", "arbitrary"),
            vmem_limit_bytes=VMEM_LIMIT_BYTES),
        name="proj",
    )(x, *trig, *consts)

def _prepare_proj_weights(norm_mix_g, w_in, idx_k_norm_g, q_a_norm_g, kv_a_norm_g,
                          w_uq, w_uk, w_uv):
    gmix = norm_mix_g.reshape(1, -1)
    wcat = _regroup_w_in(w_in).astype(BF16)
    gik = jnp.pad(idx_k_norm_g, (0, LANES - IDX_DIM)).reshape(1, LANES)
    gq = q_a_norm_g.reshape(1, -1)
    gkv = kv_a_norm_g.reshape(1, -1)
    wuq = _pad_heads(w_uq, LANES).astype(BF16)
    wuk = _pad_heads(w_uk, LANES).astype(BF16)
    wuv = _pad_heads(w_uv, LANES).astype(BF16)
    return gmix, wcat, gik, gq, gkv, wuq, wuk, wuv


_INT_MIN = -2 ** 31
_ORD_NEG_INF = 0x007FFFFF
_ORD_LOWEST_FINITE = 0x00800000


def _ordinal_to_float(u):
    o = u ^ jnp.int32(_INT_MIN)
    bits = o ^ ((o >> 31) & jnp.int32(0x7FFFFFFF))
    return lax.bitcast_convert_type(bits, F32)


def _softmax_update(s, v, m_old, acc, rows_may_be_empty=True):
    m_new = jnp.maximum(m_old, jnp.max(s, axis=-1, keepdims=True))
    if rows_may_be_empty:
        m_new = jnp.maximum(m_new, F32_LOWEST)
    p = jnp.concatenate(
        [jnp.exp2(s[:, j * LANES:(j + 1) * LANES] - m_new)
         for j in range(s.shape[1] // LANES)], axis=1)
    alpha = jnp.exp2(m_old - m_new)
    return m_new, alpha * acc + _dot(p.astype(BF16), v)


def _attend_chunks(n_heads, m_ref, acc_ref, chunk_fns):
    for h in range(n_heads):
        m, acc = m_ref[h], acc_ref[h]
        for fn in chunk_fns:
            m, acc = _softmax_update(*fn(h), m, acc)
        m_ref[h] = m
        acc_ref[h] = acc


def _for_chunk_groups(n, body):
    def group(i, carry):
        body(tuple(4 * i + r for r in range(4)))
        return carry

    lax.fori_loop(0, lax.shift_right_logical(n, 2), group, 0)
    done = n & ~3
    for left in (1, 2, 3):
        @pl.when((n & 3) == left)
        def _(left=left):
            body(tuple(done + r for r in range(left)))


def _init_softmax_state(m_ref, acc_ref):
    m_ref[...] = jnp.full(m_ref.shape, NEG_INF, F32)
    acc_ref[...] = jnp.zeros(acc_ref.shape, F32)


def _finish_heads(acc_even, acc_odd, lane):
    o0 = acc_even * (1.0 / acc_even[:, B_V_DIM:B_V_DIM + 1])
    o1 = acc_odd * (1.0 / acc_odd[:, B_V_DIM:B_V_DIM + 1])
    return jnp.where(lane < B_V_DIM, o0, pltpu.roll(o1, B_V_DIM, axis=1))


def _store_heads(o_ref, acc_ref, n_heads):
    lane = lax.broadcasted_iota(jnp.int32, acc_ref.shape[1:], 1)
    for pair in range(n_heads // 2):
        o_ref[0, :, pair * LANES:(pair + 1) * LANES] = _finish_heads(
            acc_ref[2 * pair], acc_ref[2 * pair + 1], lane).astype(o_ref.dtype)


_COARSE_PASSES = 16
_FINE_PASSES = 17
_PASSES_PER_CHUNK = 4


def _attn_kernel(ik_ref, ak_ref, av_ref, kb_ref, vb_ref, posk_col_ref, posk_row_ref,
                 iq0_ref, iq1_ref, iw0_ref, iw1_ref, aq0_ref, aq1_ref, qb0_ref, qb1_ref,
                 pqr0_ref, pqr1_ref, pqc0_ref, pqc1_ref,
                 oa0_ref, oa1_ref, ob0_ref, ob1_ref,
                 isc_ref, xb_ref, bias_ref, wt_ref, ma_ref, acca_ref, mb_ref, accb_ref,
                 *, topk, nq):
    pi = pl.program_id(1)
    tq = aq0_ref.shape[2]
    half = tq // 2
    nk0 = pi + 1
    n_virtual = nq + 1
    blocks = (
        dict(idx=0, nk=nk0, off=0, iq=iq0_ref, iw=iw0_ref, aq=aq0_ref, pqr=pqr0_ref,
             oa=oa0_ref, ob=ob0_ref),
        dict(idx=1, nk=nq - pi, off=nk0, iq=iq1_ref, iw=iw1_ref, aq=aq1_ref, pqr=pqr1_ref,
             oa=oa1_ref, ob=ob1_ref),
    )

    for blk in blocks:
        wt_ref[blk["idx"]] = blk["iw"][0].T[_T_IW:_T_IW + IDX_HEADS, :]

    def pick(v, of_block0, of_block1):
        if v == 0:
            return of_block0()
        if v >= nq // 2:
            return of_block1()
        return jnp.where(v >= nk0, of_block1(), of_block0())

    def route(v, part, acc0, acc1):
        if v == 0:
            return acc0 + part, acc1
        if v >= nq // 2:
            return acc0, acc1 + part
        in1 = v >= nk0
        zero = jnp.zeros_like(part)
        return acc0 + jnp.where(in1, zero, part), acc1 + jnp.where(in1, part, zero)

    def idx_chunk(v):
        c = pick(v, lambda: jnp.int32(v), lambda: v - nk0)
        posq = pick(v, lambda: pqr0_ref[0], lambda: pqr1_ref[0])
        wt = pick(v, lambda: wt_ref[0], lambda: wt_ref[1])
        iqs = [pick(v, lambda: iq0_ref[0, h], lambda: iq1_ref[0, h])
               for h in range(IDX_HEADS)]
        for sub in range(2):
            k0 = pl.multiple_of(c * tq + sub * half, half)
            ikc = ik_ref[0, pl.ds(k0, half), :]
            acc = jnp.zeros((half, tq), F32)
            for h in range(IDX_HEADS):
                d = _dot_nt(ikc, iqs[h])
                acc = acc + wt[h:h + 1, :] * jnp.maximum(d, 0.0)
            causal = posk_col_ref[0, pl.ds(k0, half), :] <= posq
            score = jnp.where(causal, acc, NEG_INF)
            isc_ref[v, sub * half:(sub + 1) * half, :] = score
            xb_ref[v, sub * half:(sub + 1) * half, :] = score.astype(BF16)

    for v in range(n_virtual):
        idx_chunk(v)

    _init_softmax_state(mb_ref.at[1], accb_ref.at[1])

    def latent_chunk(v):
        diagonal = v in (0, n_virtual - 1)
        if v == 0:
            block, c = 0, nk0 - 1
        elif v >= nq // 2:
            block, c = 1, v - nk0
        else:
            block = None
            in1 = v >= nk0
            c = jnp.where(in1, v - nk0, v - 1)
            b = in1.astype(jnp.int32)
        if block is not None:
            b = block
            q_ref, pqc_ref = (qb1_ref, pqc1_ref) if block else (qb0_ref, pqc0_ref)
        k0 = pl.multiple_of(c * tq, tq)
        if diagonal:
            causal = posk_row_ref[0, :, pl.ds(k0, tq)] <= pqc_ref[0]
        for h in range(B_HEADS):
            if block is not None:
                q = q_ref[0, h]
            else:
                q = jnp.where(in1, qb1_ref[0, h], qb0_ref[0, h])
            s = _dot_nt(q, kb_ref[0, h, pl.ds(k0, tq), :])
            if diagonal:
                s = jnp.where(causal, s, NEG_INF)
            if v == 0:
                m_old = jnp.full((tq, LANES), NEG_INF, F32)
                acc_old = jnp.zeros((tq, LANES), F32)
            else:
                m_old, acc_old = mb_ref[b, h], accb_ref[b, h]
            m, acc = _softmax_update(s, vb_ref[0, h, pl.ds(k0, tq), :],
                                     m_old, acc_old, rows_may_be_empty=False)
            mb_ref[b, h] = m
            accb_ref[b, h] = acc

    packed = 2 * SUBLANES
    one_b, zero_b = jnp.asarray(1, BF16), jnp.asarray(0, BF16)

    def count_ge(thr0, thr1, coarse):
        rows = packed if coarse else SUBLANES
        dtype = BF16 if coarse else F32
        acc0 = jnp.zeros((rows, tq), dtype)
        acc1 = jnp.zeros((rows, tq), dtype)
        for v in range(n_virtual):
            thr = pick(v, lambda: thr0, lambda: thr1)
            if coarse:
                hit = jnp.where(xb_ref[v] >= thr, one_b, zero_b)
            else:
                hit = jnp.where(isc_ref[v] >= thr, 1.0, 0.0)
            part = hit[:rows]
            for r in range(1, tq // rows):
                part = part + hit[r * rows:(r + 1) * rows]
            acc0, acc1 = route(v, part, acc0, acc1)
        return (jnp.sum(acc0.astype(F32), axis=0, keepdims=True),
                jnp.sum(acc1.astype(F32), axis=0, keepdims=True))

    def coarse_ordinal(j):
        low = jnp.where(j < 0x8000, jnp.int32(0xFFFF), jnp.int32(0))
        return lax.shift_left(j, 16) | low

    def coarse_pass(it, js):
        cands = [j | lax.shift_left(jnp.int32(1), 15 - it) for j in js]
        thrs = [_ordinal_to_float(coarse_ordinal(c)).astype(BF16) for c in cands]
        cnts = count_ge(*thrs, coarse=True)
        return tuple(jnp.where(n >= topk, c, j) for n, c, j in zip(cnts, cands, js))

    def fine_pass(it, ds, bases):
        cands = [d | lax.shift_left(jnp.int32(1), 16 - it) for d in ds]
        thrs = [_ordinal_to_float(b + c) for b, c in zip(bases, cands)]
        cnts = count_ge(*thrs, coarse=False)
        return tuple(jnp.where(n >= topk, c, d) for n, c, d in zip(cnts, cands, ds))

    zeros = (jnp.zeros((1, tq), jnp.int32),) * 2
    first_fine = _COARSE_PASSES // _PASSES_PER_CHUNK
    js = zeros
    for v in range(first_fine):
        latent_chunk(v)
        for r in range(_PASSES_PER_CHUNK):
            js = coarse_pass(_PASSES_PER_CHUNK * v + r, js)
    js = [jnp.maximum(j, _ORD_NEG_INF >> 16) for j in js]
    bases = [coarse_ordinal(j) - jnp.int32(1 << 16) for j in js]
    ds = zeros
    for v in range(first_fine, n_virtual):
        latent_chunk(v)
        first_pass = _PASSES_PER_CHUNK * (v - first_fine)
        for it in range(first_pass, min(first_pass + _PASSES_PER_CHUNK, _FINE_PASSES)):
            ds = fine_pass(it, ds, bases)

    t_us = [base + d for base, d in zip(bases, ds)]
    shorts = [(t_u >= 0) & (t_u < _ORD_LOWEST_FINITE) for t_u in t_us]
    thrs = [_ordinal_to_float(jnp.where(short, jnp.int32(_ORD_LOWEST_FINITE), t_u))
            for short, t_u in zip(shorts, t_us)]
    n_ges = [jnp.zeros((1, tq), F32)] * 2
    for v in range(n_virtual):
        sel = isc_ref[v] >= pick(v, lambda: thrs[0], lambda: thrs[1])
        bias_ref[v] = jnp.where(sel, 0.0, NEG_INF).T
        cnt = jnp.sum(jnp.where(sel, 1.0, 0.0), axis=0, keepdims=True)
        n_ges = route(v, cnt, *n_ges)

    for blk, thr, short, n_ge in zip(blocks, thrs, shorts, n_ges):
        off, nk = blk["off"], blk["nk"]
        tied = jnp.logical_and(n_ge > topk, jnp.logical_not(short))

        @pl.when(jnp.max(jnp.where(tied, 1.0, 0.0)) > 0.0)
        def _(thr=thr, off=off, nk=nk):
            def count_gt(c, acc):
                hit = jnp.where(isc_ref[off + c] > thr, 1.0, 0.0)
                return acc + jnp.sum(hit, axis=0, keepdims=True)
            n_gt = lax.fori_loop(0, nk, count_gt, jnp.zeros((1, tq), F32))
            need = topk - n_gt
            row = lax.broadcasted_iota(jnp.int32, (tq, tq), 0)
            col = lax.broadcasted_iota(jnp.int32, (tq, tq), 1)
            below = jnp.where(col < row, 1.0, 0.0).astype(BF16)

            def fix(c, seen):
                x = isc_ref[off + c]
                eq = jnp.where(x == thr, 1.0, 0.0)
                rank = _dot(below, eq.astype(BF16)) + seen
                sel = (x > thr) | ((x == thr) & (rank < need))
                bias_ref[off + c] = jnp.where(sel, 0.0, NEG_INF).T
                return seen + jnp.sum(eq, axis=0, keepdims=True)

            lax.fori_loop(0, nk, fix, jnp.zeros((1, tq), F32))

    _init_softmax_state(ma_ref, acca_ref)
    for blk in blocks:
        def attend(chunks, blk=blk):
            def chunk_fn(c):
                k0 = pl.multiple_of(c * tq, tq)
                bias = bias_ref[blk["off"] + c]

                def fn(h):
                    g = h // A_GROUP
                    s = _dot_nt(blk["aq"][0, h], ak_ref[0, g, pl.ds(k0, tq), :]) + bias
                    return s, av_ref[0, g, pl.ds(k0, tq), :]
                return fn

            _attend_chunks(A_HEADS, ma_ref.at[blk["idx"]], acca_ref.at[blk["idx"]],
                           [chunk_fn(c) for c in chunks])

        _for_chunk_groups(blk["nk"], attend)
        _store_heads(blk["oa"], acca_ref.at[blk["idx"]], A_HEADS)
        _store_heads(blk["ob"], accb_ref.at[blk["idx"]], B_HEADS)


def _attn_call(ik, iq, iw, aq, ak, av, qb, kb, vb, positions, tq, topk):
    b, heads, s, _ = aq.shape
    nq = s // tq
    n_pairs = nq // 2
    assert s % tq == 0 and nq % 2 == 0
    assert nq + 1 == (_COARSE_PASSES + _FINE_PASSES - 1) // _PASSES_PER_CHUNK + 1
    assert (nq + 1) * tq // (2 * SUBLANES) <= 256
    pos_col = positions[:, :, None]
    pos_row = positions[:, None, :]

    def per_seq(*shape):
        nd = len(shape)
        return pl.BlockSpec((1,) + shape, lambda i, j: (i,) + (0,) * nd)

    blk_index = (lambda j: j, lambda j: nq - 1 - j)

    def qhead(h, w, which):
        return pl.BlockSpec((1, h, tq, w), lambda i, j: (i, 0, blk_index[which](j), 0))

    def qrows(w, which):
        return pl.BlockSpec((1, tq, w), lambda i, j: (i, blk_index[which](j), 0))

    def qlanes(which):
        return pl.BlockSpec((1, 1, tq), lambda i, j: (i, 0, blk_index[which](j)))

    in_specs = [
        per_seq(s, IDX_DIM), per_seq(A_KV_HEADS, s, A_HEAD_DIM), per_seq(A_KV_HEADS, s, LANES),
        per_seq(B_HEADS, s, LANES), per_seq(B_HEADS, s, LANES),
        per_seq(s, 1), per_seq(1, s),
        qhead(IDX_HEADS, IDX_DIM, 0), qhead(IDX_HEADS, IDX_DIM, 1),
        qrows(LANES, 0), qrows(LANES, 1),
        qhead(A_HEADS, A_HEAD_DIM, 0), qhead(A_HEADS, A_HEAD_DIM, 1),
        qhead(B_HEADS, LANES, 0), qhead(B_HEADS, LANES, 1),
        qlanes(0), qlanes(1), qrows(1, 0), qrows(1, 1),
    ]
    width = heads * A_HEAD_DIM
    out_sds = jax.ShapeDtypeStruct((b, s // 2, width), BF16)
    out_lo = pl.BlockSpec((1, tq, width), lambda i, j: (i, j, 0))
    out_hi = pl.BlockSpec((1, tq, width), lambda i, j: (i, n_pairs - 1 - j, 0))
    state = pltpu.VMEM((2, heads, tq, LANES), F32)
    return pl.pallas_call(
        functools.partial(_attn_kernel, topk=topk, nq=nq),
        grid=(b, n_pairs),
        in_specs=in_specs,
        out_specs=[out_lo, out_hi, out_lo, out_hi],
        out_shape=[out_sds] * 4,
        scratch_shapes=[
            pltpu.VMEM((nq + 1, tq, tq), F32),
            pltpu.VMEM((nq + 1, tq, tq), BF16),
            pltpu.VMEM((nq + 1, tq, tq), F32),
            pltpu.VMEM((2, IDX_HEADS, tq), F32),
            state, state, state, state,
        ],
        compiler_params=pltpu.CompilerParams(
            dimension_semantics=("arbitrary", "arbitrary"),
            vmem_limit_bytes=VMEM_LIMIT_BYTES),
        name="attn",
    )(ik, ak, av, kb, vb, pos_col, pos_row, iq, iq, iw, iw, aq, aq, qb, qb,
      pos_row, pos_row, pos_col, pos_col)


def _shift_rows(u, prev_tail, shift, row8):
    rolled = pltpu.roll(u, shift, axis=0)
    head = jnp.where(row8 < shift, pltpu.roll(prev_tail, shift, axis=0),
                     rolled[:SUBLANES])
    return jnp.concatenate([head, rolled[SUBLANES:]], axis=0)


def _ffn_kernel(x_ref, oa_lo_ref, oa_hi_ref, ob_lo_ref, ob_hi_ref, ga_ref, gb_ref,
                wba_ref, wbb_ref, wout_ref, gffn_ref, wup_ref, cw_ref, cb_ref, wdown_ref,
                gfin_ref, out_ref, tail_ref, *, n_chunks, final_norm):
    j = pl.program_id(1)
    x = x_ref[0]
    tm = x.shape[0]
    first_half = j < pl.num_programs(1) // 2
    oa = jnp.where(first_half, oa_lo_ref[0], oa_hi_ref[0])
    ob = jnp.where(first_half, ob_lo_ref[0], ob_hi_ref[0])
    ya = _dot(oa, wba_ref[...])
    yb = _dot(ob, wbb_ref[...])
    merged = ga_ref[0].astype(F32) * ya + gb_ref[0].astype(F32) * yb
    h1 = x + _dot(merged.astype(BF16), wout_ref[...])
    ms = jnp.mean(h1 * h1, axis=-1, keepdims=True)
    hn = (h1 * lax.rsqrt(ms + EPS) * gffn_ref[...]).astype(BF16)

    fc = wdown_ref.shape[1]
    row8 = lax.broadcasted_iota(jnp.int32, (SUBLANES, fc), 0)

    @pl.when(j == 0)
    def _():
        tail_ref[...] = jnp.zeros_like(tail_ref)

    acc = h1
    for c in range(n_chunks):
        halves = []
        for part in range(2):
            idx = part * n_chunks + c
            u = _dot(hn, wup_ref[:, idx * fc:(idx + 1) * fc])
            prev_tail = tail_ref[idx]
            tail_ref[idx] = u[tm - SUBLANES:]
            y = (cb_ref[idx] + cw_ref[0, idx] * _shift_rows(u, prev_tail, 2, row8)
                 + cw_ref[1, idx] * _shift_rows(u, prev_tail, 1, row8)
                 + cw_ref[2, idx] * u)
            halves.append(y)
        gate, val = halves
        act = (gate * jax.nn.sigmoid(gate) * val).astype(BF16)
        acc = acc + _dot(act, wdown_ref[c])
    if final_norm:
        ms = jnp.mean(acc * acc, axis=-1, keepdims=True)
        acc = acc * lax.rsqrt(ms + EPS) * gfin_ref[...]
    out_ref[0] = acc


def _ffn_call(x, oa_lo, oa_hi, ob_lo, ob_hi, ga, gb, wba, wbb, wout, gffn, wup, cw, cb,
              wdown, gfin, tm, final_norm):
    b, s, d = x.shape
    n_chunks, fc, _ = wdown.shape
    n_parts = 2 * n_chunks
    assert wup.shape == (d, n_parts * fc) and fc % LANES == 0
    n_tiles = s // tm
    assert s % tm == 0 and n_tiles % 2 == 0
    n_half = n_tiles // 2
    tok = lambda w: pl.BlockSpec((1, tm, w), lambda i, j: (i, j, 0))
    lo = lambda w: pl.BlockSpec((1, tm, w), lambda i, j: (i, jnp.minimum(j, n_half - 1), 0))
    hi = lambda w: pl.BlockSpec((1, tm, w), lambda i, j: (i, jnp.maximum(j - n_half, 0), 0))
    wa, wb = oa_lo.shape[-1], ob_lo.shape[-1]
    weights = (wba, wbb, wout, gffn, wup, cw, cb, wdown, gfin)
    return pl.pallas_call(
        functools.partial(_ffn_kernel, n_chunks=n_chunks, final_norm=final_norm),
        grid=(b, n_tiles),
        in_specs=[tok(d), lo(wa), hi(wa), lo(wb), hi(wb), tok(d), tok(d)]
        + [_const_spec(w.shape) for w in weights],
        out_specs=tok(d),
        out_shape=jax.ShapeDtypeStruct((b, s, d), x.dtype),
        scratch_shapes=[pltpu.VMEM((n_parts, SUBLANES, fc), F32)],
        compiler_params=pltpu.CompilerParams(
            dimension_semantics=("arbitrary", "arbitrary"),
            vmem_limit_bytes=VMEM_LIMIT_BYTES),
        name="ffn",
    )(x, oa_lo, oa_hi, ob_lo, ob_hi, ga, gb, *weights)


def _prepare_ffn_weights(w_branch_a, w_branch_b, w_out, norm_ffn_g, w_up, conv_w,
                         conv_b, w_down, norm_final_g, n_chunks):
    d, two_f = w_up.shape
    fc = two_f // (2 * n_chunks)
    wup = w_up.astype(BF16)
    cw = conv_w.reshape(CONV_WIDTH, 2 * n_chunks, 1, fc)
    cb = conv_b.reshape(2 * n_chunks, 1, fc)
    wdown = w_down.reshape(n_chunks, fc, d).astype(BF16)
    return (w_branch_a.astype(BF16), w_branch_b.astype(BF16), w_out.astype(BF16),
            norm_ffn_g.reshape(1, -1), wup, cw, cb, wdown, norm_final_g.reshape(1, -1))


TM_PROJ = 512
TQ_ATTN = 256
TM_FFN = 512
FFN_CHUNKS = 1


def kernel(x, positions, norm_mix_g, w_in, idx_k_norm_g, q_a_norm_g, kv_a_norm_g, w_uq, w_uk, w_uv, w_branch_a, w_branch_b, w_out, norm_ffn_g, w_up, conv_w, conv_b, w_down, norm_final_g):
    b, s, _ = x.shape
    depth = w_in.shape[0]
    topk = min(INDEX_TOPK_MAX, s // 4)
    trig = _rope_trig(positions)
    expand = _rope_expansion()
    h = x
    for l in range(depth):
        pw = _prepare_proj_weights(norm_mix_g[l], w_in[l], idx_k_norm_g[l], q_a_norm_g[l],
                                   kv_a_norm_g[l], w_uq[l], w_uk[l], w_uv[l])
        aq, ak, av, iq, ik, iw, qb, kb, vb, ga, gb = _proj_call(h, trig, expand, *pw, tm=TM_PROJ)
        mixed = _attn_call(ik, iq, iw, aq, ak, av, qb, kb, vb, positions, TQ_ATTN, topk)
        fw = _prepare_ffn_weights(w_branch_a[l], w_branch_b[l], w_out[l], norm_ffn_g[l],
                                  w_up[l], conv_w[l], conv_b[l], w_down[l], norm_final_g,
                                  FFN_CHUNKS)
        h = _ffn_call(h, *mixed, ga, gb, *fw, tm=TM_FFN, final_norm=(l == depth - 1))
    return h
```
